```python
import jax, jax.numpy as jnp
from jax import lax
import numpy as np

D_MODEL = 1024
BATCH = 2
SEQ = 16384
DEPTH = 2

CONV_WIDTH = 512
CONV_HEADS = 8
CONV_HEAD_DIM = CONV_WIDTH // CONV_HEADS
CONV_K = 3
GMLP_WIDTH = 512
GMLP_HEADS = 4
GMLP_HEAD_DIM = GMLP_WIDTH // GMLP_HEADS
CHUNK = 128
MIX_WIDTH = CONV_WIDTH + GMLP_WIDTH
IN_PROJ_WIDTH = 3 * CONV_WIDTH + 2 * GMLP_WIDTH
D_FF_DENSE = 2816
N_EXPERTS = 8
TOP_K = 2
D_FF_EXPERT = 3584
N_DENSE = (DEPTH + 1) // 2
N_MOE = DEPTH // 2
ALPHA = (2 * DEPTH) ** 0.25
BETA = (8 * DEPTH) ** -0.25
LN_EPS = 1e-5
RMS_EPS = 1e-6

kernel_name = "hybrid_conv_sgu_moe_deepnorm"


def layer_norm(x, g, b):
    xf = x.astype(jnp.float32)
    mu = jnp.mean(xf, axis=-1, keepdims=True)
    var = jnp.mean(jnp.square(xf - mu), axis=-1, keepdims=True)
    return ((xf - mu) * lax.rsqrt(var + LN_EPS)).astype(x.dtype) * g + b


def head_rms_norm(y, n_heads, head_dim):
    bsz, s, _ = y.shape
    yf = y.astype(jnp.float32).reshape(bsz, s, n_heads, head_dim)
    yf = yf * lax.rsqrt(jnp.mean(jnp.square(yf), axis=-1, keepdims=True) + RMS_EPS)
    return yf.reshape(bsz, s, n_heads * head_dim).astype(y.dtype)


def causal_short_conv(h, w):
    hp = jnp.pad(h, ((0, 0), (CONV_K - 1, 0), (0, 0)))
    return w[0] * hp[:, :-2] + w[1] * hp[:, 1:-1] + w[2] * hp[:, 2:]


def chunked_spatial_gating(u, v, v_g, v_b, w_s, b_s):
    bsz, s, _ = v.shape
    v = layer_norm(v, v_g, v_b)
    v = v.reshape(bsz, s // CHUNK, CHUNK, GMLP_HEADS, GMLP_HEAD_DIM)
    causal = jnp.tril(jnp.ones((CHUNK, CHUNK), dtype=bool))
    w_causal = jnp.where(causal, w_s, jnp.zeros_like(w_s))
    z = jnp.einsum("hts,bnshd->bnthd", w_causal, v) + b_s.T[None, None, :, :, None]
    return u * z.reshape(bsz, s, GMLP_WIDTH)


def hybrid_mixer(x, w_in, conv_w, v_g, v_b, w_s, b_s, out_g, w_out):
    proj = x @ w_in
    b_gate, c_gate, h, u, v = jnp.split(
        proj, [CONV_WIDTH, 2 * CONV_WIDTH, 3 * CONV_WIDTH, 3 * CONV_WIDTH + GMLP_WIDTH], axis=-1)
    y_conv = b_gate * causal_short_conv(c_gate * h, conv_w)
    y_sg = chunked_spatial_gating(jax.nn.gelu(u, approximate=False), jax.nn.gelu(v, approximate=False),
                                  v_g, v_b, w_s, b_s)
    y = jnp.concatenate([head_rms_norm(y_conv, CONV_HEADS, CONV_HEAD_DIM),
                         head_rms_norm(y_sg, GMLP_HEADS, GMLP_HEAD_DIM)], axis=-1) * out_g
    return y @ w_out


def swiglu(x, wg, wu, wd):
    return (jax.nn.silu(x @ wg) * (x @ wu)) @ wd


def moe_swiglu(x, w_router, we_gate, we_up, we_down):
    logits = (x @ w_router).astype(jnp.float32)
    top_vals, top_idx = lax.top_k(logits, TOP_K)
    gates = jax.nn.softmax(top_vals, axis=-1)
    combine = jnp.einsum("bske,bsk->bse", jax.nn.one_hot(top_idx, N_EXPERTS, dtype=jnp.float32), gates)
    combine = combine.astype(x.dtype)
    out = jnp.zeros_like(x)
    for e in range(N_EXPERTS):
        out = out + combine[..., e:e + 1] * swiglu(x, we_gate[e], we_up[e], we_down[e])
    return out


def setup_inputs(seed: int = 0) -> dict:
    key = jax.random.key(seed)
    ks = jax.random.split(key, 24)
    n = jax.random.normal
    f32 = jnp.float32
    return {
        "x": n(ks[0], (BATCH, SEQ, D_MODEL), f32),
        "w_in": n(ks[1], (DEPTH, D_MODEL, IN_PROJ_WIDTH), f32) * D_MODEL ** -0.5,
        "conv_w": n(ks[2], (DEPTH, CONV_K, CONV_WIDTH), f32) * CONV_K ** -0.5,
        "v_g": 1.0 + 0.01 * n(ks[3], (DEPTH, GMLP_WIDTH), f32),
        "v_b": 0.01 * n(ks[4], (DEPTH, GMLP_WIDTH), f32),
        "w_s": n(ks[5], (DEPTH, GMLP_HEADS, CHUNK, CHUNK), f32) * CHUNK ** -0.5,
        "b_s": 1.0 + 0.01 * n(ks[6], (DEPTH, GMLP_HEADS, CHUNK), f32),
        "out_g": 1.0 + 0.01 * n(ks[7], (DEPTH, MIX_WIDTH), f32),
        "w_out": n(ks[8], (DEPTH, MIX_WIDTH, D_MODEL), f32) * (BETA * MIX_WIDTH ** -0.5),
        "ln1_g": 1.0 + 0.01 * n(ks[9], (DEPTH, D_MODEL), f32),
        "ln1_b": 0.01 * n(ks[10], (DEPTH, D_MODEL), f32),
        "ln2_g": 1.0 + 0.01 * n(ks[11], (DEPTH, D_MODEL), f32),
        "ln2_b": 0.01 * n(ks[12], (DEPTH, D_MODEL), f32),
        "w_gate": n(ks[13], (N_DENSE, D_MODEL, D_FF_DENSE), f32) * D_MODEL ** -0.5,
        "w_up": n(ks[14], (N_DENSE, D_MODEL, D_FF_DENSE), f32) * D_MODEL ** -0.5,
        "w_down": n(ks[15], (N_DENSE, D_FF_DENSE, D_MODEL), f32) * (BETA * D_FF_DENSE ** -0.5),
        "w_router": n(ks[16], (N_MOE, D_MODEL, N_EXPERTS), f32) * D_MODEL ** -0.5,
        "we_gate": n(ks[17], (N_MOE, N_EXPERTS, D_MODEL, D_FF_EXPERT), f32) * D_MODEL ** -0.5,
        "we_up": n(ks[18], (N_MOE, N_EXPERTS, D_MODEL, D_FF_EXPERT), f32) * D_MODEL ** -0.5,
        "we_down": n(ks[19], (N_MOE, N_EXPERTS, D_FF_EXPERT, D_MODEL), f32) * (BETA * D_FF_EXPERT ** -0.5),
    }


def reference(x, w_in, conv_w, v_g, v_b, w_s, b_s, out_g, w_out, ln1_g, ln1_b, ln2_g, ln2_b,
              w_gate, w_up, w_down, w_router, we_gate, we_up, we_down):
    for i in range(DEPTH):
        mix = hybrid_mixer(x, w_in[i], conv_w[i], v_g[i], v_b[i], w_s[i], b_s[i], out_g[i], w_out[i])
        x = layer_norm(ALPHA * x + mix, ln1_g[i], ln1_b[i])
        if i % 2 == 0:
            ffn = swiglu(x, w_gate[i // 2], w_up[i // 2], w_down[i // 2])
        else:
            ffn = moe_swiglu(x, w_router[i // 2], we_gate[i // 2], we_up[i // 2], we_down[i // 2])
        x = layer_norm(ALPHA * x + ffn, ln2_g[i], ln2_b[i])
    return x
```

```python
import functools

import jax
import jax.numpy as jnp
from jax import lax
from jax.experimental import pallas as pl
from jax.experimental.pallas import tpu as pltpu

F32 = jnp.float32
BF16 = jnp.bfloat16

CONV_WIDTH = 512
CONV_HEAD_DIM = 64
GMLP_WIDTH = 512
GMLP_HEADS = 4
GMLP_HEAD_DIM = 128
CHUNK = 128
N_EXPERTS = 8
TOP_K = 2
LN_EPS = 1e-5
RMS_EPS = 1e-6

LANES = 128
SUBLANES = 8
VMEM_LIMIT_BYTES = 56 * 1024 * 1024

TM_MIX = 512
TM_FFN = 512
TM_ROUTE = 512
TM_DISPATCH = 512
TM_EXPERT = 512
TM_COMBINE = 512
FF_CHUNKS = 2


def _layer_norm(r, g, b):
    mu = jnp.mean(r, axis=-1, keepdims=True)
    d = r - mu
    var = jnp.mean(d * d, axis=-1, keepdims=True)
    return d * lax.rsqrt(var + LN_EPS) * g + b


def _gelu(x):
    return 0.5 * x * (1.0 + lax.erf(x * (2.0 ** -0.5)))


def _silu(x):
    return x * (1.0 / (1.0 + jnp.exp(-x)))


def _params(semantics):
    return pltpu.CompilerParams(dimension_semantics=semantics, vmem_limit_bytes=VMEM_LIMIT_BYTES)


def _const_spec(shape):
    nd = len(shape)
    return pl.BlockSpec(shape, lambda *_: (0,) * nd)


def _mixer_kernel(x_ref, win_ref, convw_ref, vg_ref, vb_ref, ws_ref, bias_ref, outg_ref, wout_ref,
                  lng_ref, lnb_ref, o_ref, ch_ref, *, tm, seq, alpha):
    i = pl.program_id(0)
    x = x_ref[...]
    proj = jnp.dot(x.astype(BF16), win_ref[...], preferred_element_type=F32)
    cw_, gw_ = CONV_WIDTH, GMLP_WIDTH
    b_gate = proj[:, 0:cw_]
    c_gate = proj[:, cw_:2 * cw_]
    hh = proj[:, 2 * cw_:3 * cw_]
    u = proj[:, 3 * cw_:3 * cw_ + gw_]
    v = proj[:, 3 * cw_ + gw_:3 * cw_ + 2 * gw_]

    @pl.when(lax.rem(i * tm, seq) == 0)
    def _():
        ch_ref[0:SUBLANES, :] = jnp.zeros((SUBLANES, cw_), F32)

    ch_ref[SUBLANES:SUBLANES + tm, :] = c_gate * hh
    convw = convw_ref[...]
    conv = (convw[0:1, :] * ch_ref[SUBLANES - 2:SUBLANES - 2 + tm, :]
            + convw[1:2, :] * ch_ref[SUBLANES - 1:SUBLANES - 1 + tm, :]
            + convw[2:3, :] * ch_ref[SUBLANES:SUBLANES + tm, :])
    ch_ref[0:SUBLANES, :] = ch_ref[tm:tm + SUBLANES, :]
    y_conv = b_gate * conv

    ug = _gelu(u)
    vn = _layer_norm(_gelu(v), vg_ref[...], vb_ref[...]).astype(BF16)
    nch = tm // CHUNK
    trow = lax.broadcasted_iota(jnp.int32, (CHUNK, CHUNK), 0)
    tcol = lax.broadcasted_iota(jnp.int32, (CHUNK, CHUNK), 1)
    causal = tcol <= trow
    zs = []
    for h in range(GMLP_HEADS):
        w = jnp.where(causal, ws_ref[h], 0.0).astype(BF16)
        lo, hi = h * GMLP_HEAD_DIM, (h + 1) * GMLP_HEAD_DIM
        rhs = jnp.concatenate([vn[c * CHUNK:(c + 1) * CHUNK, lo:hi] for c in range(nch)], axis=1)
        zs.append(jnp.dot(w, rhs, preferred_element_type=F32))
    z = jnp.concatenate(
        [jnp.concatenate([zs[h][:, c * GMLP_HEAD_DIM:(c + 1) * GMLP_HEAD_DIM]
                          for h in range(GMLP_HEADS)], axis=1) for c in range(nch)], axis=0)
    bias = jnp.concatenate([bias_ref[...]] * nch, axis=0)
    y_sg = ug * (z + bias)

    lane = lax.broadcasted_iota(jnp.int32, (tm, LANES), 1)
    low_half = lane < CONV_HEAD_DIM
    parts = []
    for j in range(cw_ // LANES):
        yt = y_conv[:, j * LANES:(j + 1) * LANES]
        sq = yt * yt
        ms_lo = jnp.sum(jnp.where(low_half, sq, 0.0), axis=-1, keepdims=True) * (1.0 / CONV_HEAD_DIM)
        ms_hi = jnp.sum(jnp.where(low_half, 0.0, sq), axis=-1, keepdims=True) * (1.0 / CONV_HEAD_DIM)
        parts.append(yt * jnp.where(low_half, lax.rsqrt(ms_lo + RMS_EPS), lax.rsqrt(ms_hi + RMS_EPS)))
    for h in range(GMLP_HEADS):
        yt = y_sg[:, h * GMLP_HEAD_DIM:(h + 1) * GMLP_HEAD_DIM]
        ms = jnp.mean(yt * yt, axis=-1, keepdims=True)
        parts.append(yt * lax.rsqrt(ms + RMS_EPS))
    y = jnp.concatenate(parts, axis=1) * outg_ref[...]

    mix = jnp.dot(y.astype(BF16), wout_ref[...], preferred_element_type=F32)
    o_ref[...] = _layer_norm(alpha * x + mix, lng_ref[...], lnb_ref[...])


def _mixer_ln(x, w_in, conv_w, v_g, v_b, w_s, bias_t, out_g, w_out, ln_g, ln_b, *, seq, alpha):
    n, d = x.shape
    tm = TM_MIX
    assert n % tm == 0 and seq % tm == 0 and tm % CHUNK == 0
    kern = functools.partial(_mixer_kernel, tm=tm, seq=seq, alpha=alpha)
    row = pl.BlockSpec((tm, d), lambda i: (i, 0))
    return pl.pallas_call(
        kern,
        out_shape=jax.ShapeDtypeStruct((n, d), F32),
        grid=(n // tm,),
        in_specs=[row, _const_spec(w_in.shape), _const_spec(conv_w.shape), _const_spec(v_g.shape),
                  _const_spec(v_b.shape), _const_spec(w_s.shape), _const_spec(bias_t.shape),
                  _const_spec(out_g.shape), _const_spec(w_out.shape), _const_spec(ln_g.shape),
                  _const_spec(ln_b.shape)],
        out_specs=row,
        scratch_shapes=[pltpu.VMEM((tm + SUBLANES, CONV_WIDTH), F32)],
        compiler_params=_params(("arbitrary",)),
        name="mixer_ln",
    )(x, w_in, conv_w, v_g, v_b, w_s, bias_t, out_g, w_out, ln_g, ln_b)


def _ffn_kernel(x_ref, wg_ref, wu_ref, wd_ref, lng_ref, lnb_ref, o_ref, *, alpha):
    x = x_ref[...]
    xb = x.astype(BF16)
    g = jnp.dot(xb, wg_ref[...], preferred_element_type=F32)
    u = jnp.dot(xb, wu_ref[...], preferred_element_type=F32)
    h = (_silu(g) * u).astype(BF16)
    ffn = jnp.dot(h, wd_ref[...], preferred_element_type=F32)
    o_ref[...] = _layer_norm(alpha * x + ffn, lng_ref[...], lnb_ref[...])


def _ffn_ln(x, wg, wu, wd, ln_g, ln_b, *, alpha):
    n, d = x.shape
    tm = TM_FFN
    assert n % tm == 0
    row = pl.BlockSpec((tm, d), lambda i: (i, 0))
    return pl.pallas_call(
        functools.partial(_ffn_kernel, alpha=alpha),
        out_shape=jax.ShapeDtypeStruct((n, d), F32),
        grid=(n // tm,),
        in_specs=[row, _const_spec(wg.shape), _const_spec(wu.shape), _const_spec(wd.shape),
                  _const_spec(ln_g.shape), _const_spec(ln_b.shape)],
        out_specs=row,
        compiler_params=_params(("arbitrary",)),
        name="ffn_ln",
    )(x, wg, wu, wd, ln_g, ln_b)


_M_E0, _M_E1, _M_R0, _M_R1, _M_G0, _M_G1 = range(6)


def _router_kernel(x_ref, wrh_ref, wrl_ref, meta_ref, tot_ref, base_ref, *, tm):
    i = pl.program_id(0)

    @pl.when(i == 0)
    def _():
        base_ref[...] = jnp.zeros(base_ref.shape, F32)

    x = x_ref[...]
    xh = x.astype(BF16)
    xl = (x - xh.astype(F32)).astype(BF16)
    wrh = wrh_ref[...]
    logits = (jnp.dot(xh, wrh, preferred_element_type=F32)
              + jnp.dot(xl, wrh, preferred_element_type=F32)
              + jnp.dot(xh, wrl_ref[...], preferred_element_type=F32))
    lane = lax.broadcasted_iota(jnp.int32, (tm, LANES), 1)
    lanef = lane.astype(F32)
    neg = jnp.float32(-jnp.inf)
    lg = jnp.where(lane < N_EXPERTS, logits, neg)
    m0 = jnp.max(lg, axis=-1, keepdims=True)
    e0 = jnp.min(jnp.where(lg == m0, lanef, float(LANES)), axis=-1, keepdims=True)
    lg1 = jnp.where(lanef == e0, neg, lg)
    m1 = jnp.max(lg1, axis=-1, keepdims=True)
    e1 = jnp.min(jnp.where(lg1 == m1, lanef, float(LANES)), axis=-1, keepdims=True)
    t = jnp.exp(m1 - m0)
    g0 = 1.0 / (1.0 + t)
    g1 = t / (1.0 + t)

    oh0 = lanef == e0
    oh1 = lanef == e1
    cnt = jnp.where(oh0 | oh1, 1.0, 0.0)
    r_i = lax.broadcasted_iota(jnp.int32, (tm, tm), 0)
    c_i = lax.broadcasted_iota(jnp.int32, (tm, tm), 1)
    tri = jnp.where(c_i < r_i, 1.0, 0.0).astype(BF16)
    prefix = jnp.dot(tri, cnt.astype(BF16), preferred_element_type=F32) + base_ref[0:1, :]
    r0 = jnp.sum(jnp.where(oh0, prefix, 0.0), axis=-1, keepdims=True)
    r1 = jnp.sum(jnp.where(oh1, prefix, 0.0), axis=-1, keepdims=True)
    total = base_ref[0:1, :] + jnp.sum(cnt, axis=0, keepdims=True)
    base_ref[...] = jnp.broadcast_to(total, base_ref.shape)
    tot_ref[...] = jnp.broadcast_to(total, tot_ref.shape)

    meta = jnp.zeros((tm, LANES), F32)
    for k, val in ((_M_E0, e0), (_M_E1, e1), (_M_R0, r0), (_M_R1, r1), (_M_G0, g0), (_M_G1, g1)):
        meta = jnp.where(lane == k, val, meta)
    meta_ref[...] = meta


def _router(x, wr_hi, wr_lo):
    n, d = x.shape
    tm = TM_ROUTE
    assert n % tm == 0
    return pl.pallas_call(
        functools.partial(_router_kernel, tm=tm),
        out_shape=(jax.ShapeDtypeStruct((n, LANES), F32), jax.ShapeDtypeStruct((SUBLANES, LANES), F32)),
        grid=(n // tm,),
        in_specs=[pl.BlockSpec((tm, d), lambda i: (i, 0)), _const_spec(wr_hi.shape), _const_spec(wr_lo.shape)],
        out_specs=(pl.BlockSpec((tm, LANES), lambda i: (i, 0)), _const_spec((SUBLANES, LANES))),
        scratch_shapes=[pltpu.VMEM((SUBLANES, LANES), F32)],
        compiler_params=_params(("arbitrary",)),
        name="router",
    )(x, wr_hi, wr_lo)


def _row_copy(src_hbm, src_row, dst_hbm, dst_row, sem):
    return pltpu.make_async_copy(src_hbm.at[pl.ds(src_row, 1)], dst_hbm.at[pl.ds(dst_row, 1)], sem)


def _dispatch_kernel(fill_ref, pos_ref, x_hbm, xs_hbm, zero_ref, sem, zsem, *, tm, te):
    i = pl.program_id(0)

    @pl.when(i == 0)
    def _():
        zero_ref[...] = jnp.zeros(zero_ref.shape, F32)

        def fill(j):
            return pltpu.make_async_copy(zero_ref, xs_hbm.at[pl.ds(pl.multiple_of(fill_ref[j], te), te)], zsem)

        n_fill = 2 * N_EXPERTS
        for j in range(n_fill):
            pl.when(fill_ref[n_fill + j] != 0)(lambda j=j: fill(j).start())
        for j in range(n_fill):
            pl.when(fill_ref[n_fill + j] != 0)(lambda j=j: fill(j).wait())

    base = i * tm

    def issue(t, c):
        _row_copy(x_hbm, base + t, xs_hbm, pos_ref[2 * t], sem).start()
        _row_copy(x_hbm, base + t, xs_hbm, pos_ref[2 * t + 1], sem).start()
        return c

    lax.fori_loop(0, tm, issue, 0)

    def drain(t, c):
        _row_copy(x_hbm, base + t, xs_hbm, pos_ref[2 * t], sem).wait()
        _row_copy(x_hbm, base + t, xs_hbm, pos_ref[2 * t + 1], sem).wait()
        return c

    lax.fori_loop(0, tm, drain, 0)


def _dispatch(x, pos_flat, fill_starts, n_rows):
    n, d = x.shape
    tm, te = TM_DISPATCH, TM_EXPERT
    assert n % tm == 0
    grid_spec = pltpu.PrefetchScalarGridSpec(
        num_scalar_prefetch=1,
        grid=(n // tm,),
        in_specs=[pl.BlockSpec((TOP_K * tm,), lambda i, tail: (i,), memory_space=pltpu.SMEM),
                  pl.BlockSpec(memory_space=pl.ANY)],
        out_specs=pl.BlockSpec(memory_space=pl.ANY),
        scratch_shapes=[pltpu.VMEM((te, d), F32), pltpu.SemaphoreType.DMA, pltpu.SemaphoreType.DMA],
    )
    return pl.pallas_call(
        functools.partial(_dispatch_kernel, tm=tm, te=te),
        out_shape=jax.ShapeDtypeStruct((n_rows, d), F32),
        grid_spec=grid_spec,
        compiler_params=_params(("arbitrary",)),
        name="dispatch",
    )(fill_starts, pos_flat, x)


def _experts_kernel(tile_ref, expert_ref, nused_ref, xs_ref, wg_ref, wu_ref, wd_ref, ys_ref, xb_ref, acc_ref):
    i = pl.program_id(0)
    f = pl.program_id(1)

    used = i < nused_ref[0]

    @pl.when(jnp.logical_not(used) & (f == 0))
    def _():
        ys_ref[...] = jnp.zeros(ys_ref.shape, F32)

    @pl.when(used)
    def _():
        @pl.when(f == 0)
        def _():
            xb_ref[...] = xs_ref[...].astype(BF16)

        xb = xb_ref[...]
        g = jnp.dot(xb, wg_ref[0], preferred_element_type=F32)
        u = jnp.dot(xb, wu_ref[0], preferred_element_type=F32)
        h = (_silu(g) * u).astype(BF16)
        part = jnp.dot(h, wd_ref[0], preferred_element_type=F32)

        @pl.when(f == 0)
        def _():
            acc_ref[...] = part

        @pl.when(f > 0)
        def _():
            acc_ref[...] += part

        @pl.when(f == pl.num_programs(1) - 1)
        def _():
            ys_ref[...] = acc_ref[...]


def _experts(xs, we_gate, we_up, we_down, tile_idx, tile_expert, n_used):
    n_rows, d = xs.shape
    te = TM_EXPERT
    ff = we_gate.shape[-1]
    fc = ff // FF_CHUNKS
    assert n_rows % te == 0 and ff % FF_CHUNKS == 0 and fc % LANES == 0
    n_tiles = n_rows // te
    grid_spec = pltpu.PrefetchScalarGridSpec(
        num_scalar_prefetch=3,
        grid=(n_tiles, FF_CHUNKS),
        in_specs=[
            pl.BlockSpec((te, d), lambda i, f, tile, ex, nu: (tile[i], 0)),
            pl.BlockSpec((1, d, fc), lambda i, f, tile, ex, nu: (ex[i], 0, jnp.where(i < nu[0], f, FF_CHUNKS - 1))),
            pl.BlockSpec((1, d, fc), lambda i, f, tile, ex, nu: (ex[i], 0, jnp.where(i < nu[0], f, FF_CHUNKS - 1))),
            pl.BlockSpec((1, fc, d), lambda i, f, tile, ex, nu: (ex[i], jnp.where(i < nu[0], f, FF_CHUNKS - 1), 0)),
        ],
        out_specs=pl.BlockSpec((te, d), lambda i, f, tile, ex, nu: (i, 0)),
        scratch_shapes=[pltpu.VMEM((te, d), BF16), pltpu.VMEM((te, d), F32)],
    )
    return pl.pallas_call(
        _experts_kernel,
        out_shape=jax.ShapeDtypeStruct((n_rows, d), F32),
        grid_spec=grid_spec,
        compiler_params=_params(("arbitrary", "arbitrary")),
        name="experts",
    )(tile_idx, tile_expert, n_used, xs, we_gate, we_up, we_down)


def _combine_kernel(pos_ref, posn_ref, x_ref, meta_ref, lng_ref, lnb_ref, ys_hbm, o_ref, buf_ref, sem, *,
                    tm, alpha):
    i = pl.program_id(0)
    n_steps = pl.num_programs(0)
    slot = lax.rem(i, 2)

    def gather(p_ref, s, start):
        def body(t, c):
            for k in range(TOP_K):
                cp = pltpu.make_async_copy(ys_hbm.at[pl.ds(p_ref[TOP_K * t + k], 1)],
                                           buf_ref.at[s, k, pl.ds(t, 1)], sem.at[s])
                if start:
                    cp.start()
                else:
                    cp.wait()
            return c
        lax.fori_loop(0, tm, body, 0)

    @pl.when(i == 0)
    def _():
        gather(pos_ref, 0, True)

    @pl.when(i + 1 < n_steps)
    def _():
        gather(posn_ref, 1 - slot, True)

    gather(pos_ref, slot, False)

    meta = meta_ref[...]
    g0 = meta[:, _M_G0:_M_G0 + 1]
    g1 = meta[:, _M_G1:_M_G1 + 1]
    moe = g0 * buf_ref[slot, 0] + g1 * buf_ref[slot, 1]
    o_ref[...] = _layer_norm(alpha * x_ref[...] + moe, lng_ref[...], lnb_ref[...])


def _combine_ln(x, meta, pos_flat, ys, ln_g, ln_b, *, alpha):
    n, d = x.shape
    tm = TM_COMBINE
    assert n % tm == 0
    n_steps = n // tm
    row = pl.BlockSpec((tm, d), lambda i: (i, 0))
    return pl.pallas_call(
        functools.partial(_combine_kernel, tm=tm, alpha=alpha),
        out_shape=jax.ShapeDtypeStruct((n, d), F32),
        grid=(n_steps,),
        in_specs=[pl.BlockSpec((TOP_K * tm,), lambda i: (i,), memory_space=pltpu.SMEM),
                  pl.BlockSpec((TOP_K * tm,), lambda i: (jnp.minimum(i + 1, n_steps - 1),), memory_space=pltpu.SMEM),
                  row, pl.BlockSpec((tm, LANES), lambda i: (i, 0)),
                  _const_spec(ln_g.shape), _const_spec(ln_b.shape),
                  pl.BlockSpec(memory_space=pl.ANY)],
        out_specs=row,
        scratch_shapes=[pltpu.VMEM((2, TOP_K, tm, d), F32), pltpu.SemaphoreType.DMA((2,))],
        compiler_params=_params(("arbitrary",)),
        name="combine_ln",
    )(pos_flat, pos_flat, x, meta, ln_g, ln_b, ys)


def _moe_ln(x, w_router, we_gate, we_up, we_down, ln_g, ln_b, *, alpha):
    n, d = x.shape
    te = TM_EXPERT
    wr = jnp.zeros((d, LANES), F32).at[:, :N_EXPERTS].set(w_router)
    wr_hi = wr.astype(BF16)
    wr_lo = (wr - wr_hi.astype(F32)).astype(BF16)
    meta, totals = _router(x, wr_hi, wr_lo)

    counts = totals[0, :N_EXPERTS].astype(jnp.int32)
    tiles_e = (counts + te - 1) // te
    tile_end = jnp.cumsum(tiles_e)
    starts = (tile_end - tiles_e) * te
    n_used = tile_end[-1]
    n_tiles = (TOP_K * n) // te + N_EXPERTS
    tail_starts = starts + (counts // te) * te
    spare = n_used + jnp.arange(N_EXPERTS, dtype=jnp.int32)
    spare_starts = jnp.minimum(spare, n_tiles - 1) * te
    fill_starts = jnp.concatenate([tail_starts, spare_starts, counts % te != 0, spare < n_tiles]).astype(jnp.int32)
    experts_i = meta[:, _M_E0:_M_E1 + 1].astype(jnp.int32)
    ranks_i = meta[:, _M_R0:_M_R1 + 1].astype(jnp.int32)
    onehot = experts_i[..., None] == jnp.arange(N_EXPERTS, dtype=jnp.int32)
    pos = jnp.sum(jnp.where(onehot, starts, 0), axis=-1) + ranks_i
    pos_flat = pos.reshape(-1)
    j = jnp.minimum(jnp.arange(n_tiles, dtype=jnp.int32), n_used - 1)
    tile_expert = jnp.minimum(jnp.sum(j[:, None] >= tile_end[None, :], axis=-1), N_EXPERTS - 1).astype(jnp.int32)

    xs = _dispatch(x, pos_flat, fill_starts, n_tiles * te)
    ys = _experts(xs, we_gate, we_up, we_down, j, tile_expert, n_used.reshape(1))
    return _combine_ln(x, meta, pos_flat, ys, ln_g, ln_b, alpha=alpha)


def kernel(x, w_in, conv_w, v_g, v_b, w_s, b_s, out_g, w_out, ln1_g, ln1_b, ln2_g, ln2_b,
           w_gate, w_up, w_down, w_router, we_gate, we_up, we_down):
    bsz, seq, d = x.shape
    depth = w_in.shape[0]
    alpha = float((2 * depth) ** 0.25)
    h = x.reshape(bsz * seq, d)
    row = lambda a: a.reshape(1, -1)
    for i in range(depth):
        bias_t = jnp.repeat(b_s[i].T, GMLP_HEAD_DIM, axis=1)
        h = _mixer_ln(h, w_in[i].astype(BF16), conv_w[i], row(v_g[i]), row(v_b[i]), w_s[i], bias_t,
                      row(out_g[i]), w_out[i].astype(BF16), row(ln1_g[i]), row(ln1_b[i]), seq=seq, alpha=alpha)
        if i % 2 == 0:
            j = i // 2
            h = _ffn_ln(h, w_gate[j].astype(BF16), w_up[j].astype(BF16), w_down[j].astype(BF16),
                        row(ln2_g[i]), row(ln2_b[i]), alpha=alpha)
        else:
            j = i // 2
            h = _moe_ln(h, w_router[j], we_gate[j].astype(BF16), we_up[j].astype(BF16),
                        we_down[j].astype(BF16), row(ln2_g[i]), row(ln2_b[i]), alpha=alpha)
    return h.reshape(bsz, seq, d)
```

```python
import functools

import jax
import jax.numpy as jnp
from jax import lax
from jax.experimental import pallas as pl
from jax.experimental.pallas import tpu as pltpu

F32 = jnp.float32
BF16 = jnp.bfloat16

CONV_WIDTH = 512
CONV_HEAD_DIM = 64
GMLP_WIDTH = 512
GMLP_HEADS = 4
GMLP_HEAD_DIM = 128
CHUNK = 128
N_EXPERTS = 8
TOP_K = 2
LN_EPS = 1e-5
RMS_EPS = 1e-6

LANES = 128
SUBLANES = 8
BF16_ROWS = 16
VMEM_LIMIT_BYTES = 56 * 1024 * 1024

TM_MIX = 512
TM_FFN = 512
TM_MOE = 512
TM_EXPERT = 512
FF_CHUNKS = 2

RUN_ALIGN = BF16_ROWS
RUN_BITS = 6
SLOTS = -(-(TOP_K * TM_MOE + N_EXPERTS * (RUN_ALIGN - 1)) // LANES) * LANES
GATE_LANES = LANES
TAIL_BITS = 5


def _layer_norm(r, g, b):
    mu = jnp.mean(r, axis=-1, keepdims=True)
    d = r - mu
    var = jnp.mean(d * d, axis=-1, keepdims=True)
    return d * lax.rsqrt(var + LN_EPS) * g + b


def _gelu(x):
    return 0.5 * x * (1.0 + lax.erf(x * (2.0 ** -0.5)))


def _silu(x):
    return x * (1.0 / (1.0 + jnp.exp(-x)))


def _params(semantics):
    return pltpu.CompilerParams(dimension_semantics=semantics, vmem_limit_bytes=VMEM_LIMIT_BYTES)


def _const_spec(shape):
    nd = len(shape)
    return pl.BlockSpec(shape, lambda *_: (0,) * nd)


def _mixer_kernel(x_ref, win_ref, convw_ref, vg_ref, vb_ref, ws_ref, bias_ref, outg_ref, wout_ref,
                  lng_ref, lnb_ref, o_ref, ch_ref, *, tm, seq, alpha):
    i = pl.program_id(0)
    x = x_ref[...]
    proj = jnp.dot(x.astype(BF16), win_ref[...], preferred_element_type=F32)
    cw_, gw_ = CONV_WIDTH, GMLP_WIDTH
    b_gate = proj[:, 0:cw_]
    c_gate = proj[:, cw_:2 * cw_]
    hh = proj[:, 2 * cw_:3 * cw_]
    u = proj[:, 3 * cw_:3 * cw_ + gw_]
    v = proj[:, 3 * cw_ + gw_:3 * cw_ + 2 * gw_]

    @pl.when(lax.rem(i * tm, seq) == 0)
    def _():
        ch_ref[0:SUBLANES, :] = jnp.zeros((SUBLANES, cw_), F32)

    ch_ref[SUBLANES:SUBLANES + tm, :] = c_gate * hh
    convw = convw_ref[...]
    conv = (convw[0:1, :] * ch_ref[SUBLANES - 2:SUBLANES - 2 + tm, :]
            + convw[1:2, :] * ch_ref[SUBLANES - 1:SUBLANES - 1 + tm, :]
            + convw[2:3, :] * ch_ref[SUBLANES:SUBLANES + tm, :])
    ch_ref[0:SUBLANES, :] = ch_ref[tm:tm + SUBLANES, :]
    y_conv = b_gate * conv

    ug = _gelu(u)
    vn = _layer_norm(_gelu(v), vg_ref[...], vb_ref[...]).astype(BF16)
    nch = tm // CHUNK
    trow = lax.broadcasted_iota(jnp.int32, (CHUNK, CHUNK), 0)
    tcol = lax.broadcasted_iota(jnp.int32, (CHUNK, CHUNK), 1)
    causal = tcol <= trow
    zs = []
    for h in range(GMLP_HEADS):
        w = jnp.where(causal, ws_ref[h], 0.0).astype(BF16)
        lo, hi = h * GMLP_HEAD_DIM, (h + 1) * GMLP_HEAD_DIM
        rhs = jnp.concatenate([vn[c * CHUNK:(c + 1) * CHUNK, lo:hi] for c in range(nch)], axis=1)
        zs.append(jnp.dot(w, rhs, preferred_element_type=F32))
    z = jnp.concatenate(
        [jnp.concatenate([zs[h][:, c * GMLP_HEAD_DIM:(c + 1) * GMLP_HEAD_DIM]
                          for h in range(GMLP_HEADS)], axis=1) for c in range(nch)], axis=0)
    bias = jnp.concatenate([bias_ref[...]] * nch, axis=0)
    y_sg = ug * (z + bias)

    lane = lax.broadcasted_iota(jnp.int32, (tm, LANES), 1)
    low_half = lane < CONV_HEAD_DIM
    parts = []
    for j in range(cw_ // LANES):
        yt = y_conv[:, j * LANES:(j + 1) * LANES]
        sq = yt * yt
        ms_lo = jnp.sum(jnp.where(low_half, sq, 0.0), axis=-1, keepdims=True) * (1.0 / CONV_HEAD_DIM)
        ms_hi = jnp.sum(jnp.where(low_half, 0.0, sq), axis=-1, keepdims=True) * (1.0 / CONV_HEAD_DIM)
        parts.append(yt * jnp.where(low_half, lax.rsqrt(ms_lo + RMS_EPS), lax.rsqrt(ms_hi + RMS_EPS)))
    for h in range(GMLP_HEADS):
        yt = y_sg[:, h * GMLP_HEAD_DIM:(h + 1) * GMLP_HEAD_DIM]
        ms = jnp.mean(yt * yt, axis=-1, keepdims=True)
        parts.append(yt * lax.rsqrt(ms + RMS_EPS))
    y = jnp.concatenate(parts, axis=1) * outg_ref[...]

    mix = jnp.dot(y.astype(BF16), wout_ref[...], preferred_element_type=F32)
    o_ref[...] = _layer_norm(alpha * x + mix, lng_ref[...], lnb_ref[...])


def _mixer_ln(x, w_in, conv_w, v_g, v_b, w_s, bias_t, out_g, w_out, ln_g, ln_b, *, seq, alpha):
    n, d = x.shape
    tm = TM_MIX
    assert n % tm == 0 and seq % tm == 0 and tm % CHUNK == 0
    kern = functools.partial(_mixer_kernel, tm=tm, seq=seq, alpha=alpha)
    row = pl.BlockSpec((tm, d), lambda i: (i, 0))
    return pl.pallas_call(
        kern,
        out_shape=jax.ShapeDtypeStruct((n, d), F32),
        grid=(n // tm,),
        in_specs=[row, _const_spec(w_in.shape), _const_spec(conv_w.shape), _const_spec(v_g.shape),
                  _const_spec(v_b.shape), _const_spec(w_s.shape), _const_spec(bias_t.shape),
                  _const_spec(out_g.shape), _const_spec(w_out.shape), _const_spec(ln_g.shape),
                  _const_spec(ln_b.shape)],
        out_specs=row,
        scratch_shapes=[pltpu.VMEM((tm + SUBLANES, CONV_WIDTH), F32)],
        compiler_params=_params(("arbitrary",)),
        name="mixer_ln",
    )(x, w_in, conv_w, v_g, v_b, w_s, bias_t, out_g, w_out, ln_g, ln_b)


def _ffn_kernel(x_ref, wg_ref, wu_ref, wd_ref, lng_ref, lnb_ref, o_ref, *, alpha):
    x = x_ref[...]
    xb = x.astype(BF16)
    g = jnp.dot(xb, wg_ref[...], preferred_element_type=F32)
    u = jnp.dot(xb, wu_ref[...], preferred_element_type=F32)
    h = (_silu(g) * u).astype(BF16)
    ffn = jnp.dot(h, wd_ref[...], preferred_element_type=F32)
    o_ref[...] = _layer_norm(alpha * x + ffn, lng_ref[...], lnb_ref[...])


def _ffn_ln(x, wg, wu, wd, ln_g, ln_b, *, alpha):
    n, d = x.shape
    tm = TM_FFN
    assert n % tm == 0
    row = pl.BlockSpec((tm, d), lambda i: (i, 0))
    return pl.pallas_call(
        functools.partial(_ffn_kernel, alpha=alpha),
        out_shape=jax.ShapeDtypeStruct((n, d), F32),
        grid=(n // tm,),
        in_specs=[row, _const_spec(wg.shape), _const_spec(wu.shape), _const_spec(wd.shape),
                  _const_spec(ln_g.shape), _const_spec(ln_b.shape)],
        out_specs=row,
        compiler_params=_params(("arbitrary",)),
        name="ffn_ln",
    )(x, wg, wu, wd, ln_g, ln_b)


_M_E0, _M_E1, _M_S0, _M_S1, _M_G0, _M_G1 = range(6)
_R_LSTART, _R_LEN, _R_GOFF = range(3)


def _router_kernel(x_ref, wrh_ref, wrl_ref, meta_ref, runs_ref, tot_ref, base_ref, *, tm):
    i = pl.program_id(0)

    @pl.when(i == 0)
    def _():
        base_ref[...] = jnp.zeros(base_ref.shape, F32)

    x = x_ref[...]
    xh = x.astype(BF16)
    xl = (x - xh.astype(F32)).astype(BF16)
    wrh = wrh_ref[...]
    logits = (jnp.dot(xh, wrh, preferred_element_type=F32)
              + jnp.dot(xl, wrh, preferred_element_type=F32)
              + jnp.dot(xh, wrl_ref[...], preferred_element_type=F32))
    lane = lax.broadcasted_iota(jnp.int32, (tm, LANES), 1)
    lanef = lane.astype(F32)
    neg = jnp.float32(-jnp.inf)
    lg = jnp.where(lane < N_EXPERTS, logits, neg)
    m0 = jnp.max(lg, axis=-1, keepdims=True)
    e0 = jnp.min(jnp.where(lg == m0, lanef, float(LANES)), axis=-1, keepdims=True)
    lg1 = jnp.where(lanef == e0, neg, lg)
    m1 = jnp.max(lg1, axis=-1, keepdims=True)
    e1 = jnp.min(jnp.where(lg1 == m1, lanef, float(LANES)), axis=-1, keepdims=True)
    t = jnp.exp(m1 - m0)
    g0 = 1.0 / (1.0 + t)
    g1 = t / (1.0 + t)

    oh0 = lanef == e0
    oh1 = lanef == e1
    cnt = jnp.where(oh0 | oh1, 1.0, 0.0)
    r_i = lax.broadcasted_iota(jnp.int32, (tm, tm), 0)
    c_i = lax.broadcasted_iota(jnp.int32, (tm, tm), 1)
    tri = jnp.where(c_i < r_i, 1.0, 0.0).astype(BF16)
    rank = jnp.dot(tri, cnt.astype(BF16), preferred_element_type=F32)

    count = jnp.broadcast_to(jnp.sum(cnt, axis=0, keepdims=True), (SUBLANES, LANES))
    run_len = jnp.floor((count + (RUN_ALIGN - 1.0)) * (1.0 / RUN_ALIGN)) * RUN_ALIGN
    lane8 = lax.broadcasted_iota(jnp.int32, (SUBLANES, LANES), 1)
    incl = run_len
    for sh in (1, 2, 4):
        incl = incl + jnp.where(lane8 >= sh, pltpu.roll(incl, sh, axis=1), 0.0)
    lstart = incl - run_len
    goff = base_ref[...]
    base_ref[...] = goff + run_len
    tot_ref[...] = goff + run_len
    row8 = lax.broadcasted_iota(jnp.int32, (SUBLANES, LANES), 0)
    runs_ref[...] = jnp.where(row8 == _R_LSTART, lstart,
                              jnp.where(row8 == _R_LEN, run_len, jnp.where(row8 == _R_GOFF, goff, 0.0)))

    slot = rank + lstart[0:1, :]
    s0 = jnp.sum(jnp.where(oh0, slot, 0.0), axis=-1, keepdims=True)
    s1 = jnp.sum(jnp.where(oh1, slot, 0.0), axis=-1, keepdims=True)
    meta = jnp.zeros((tm, LANES), F32)
    for k, val in ((_M_E0, e0), (_M_E1, e1), (_M_S0, s0), (_M_S1, s1), (_M_G0, g0), (_M_G1, g1)):
        meta = jnp.where(lane == k, val, meta)
    meta_ref[...] = meta


def _router(x, wr_hi, wr_lo):
    n, d = x.shape
    tm = TM_MOE
    assert n % tm == 0
    nt = n // tm
    small = (SUBLANES, LANES)
    return pl.pallas_call(
        functools.partial(_router_kernel, tm=tm),
        out_shape=(jax.ShapeDtypeStruct((n, LANES), F32), jax.ShapeDtypeStruct((nt * SUBLANES, LANES), F32),
                   jax.ShapeDtypeStruct(small, F32)),
        grid=(nt,),
        in_specs=[pl.BlockSpec((tm, d), lambda i: (i, 0)), _const_spec(wr_hi.shape), _const_spec(wr_lo.shape)],
        out_specs=(pl.BlockSpec((tm, LANES), lambda i: (i, 0)), pl.BlockSpec(small, lambda i: (i, 0)),
                   _const_spec(small)),
        scratch_shapes=[pltpu.VMEM(small, F32)],
        compiler_params=_params(("arbitrary",)),
        name="router",
    )(x, wr_hi, wr_lo)


def _run_copies(tab_ref, tile, local_ref, glob_hbm, sem, *, to_global, start):
    base = tile * (3 * N_EXPERTS)
    for e in range(N_EXPERTS):
        lstart = tab_ref[base + e]
        length = tab_ref[base + N_EXPERTS + e]
        gstart = tab_ref[base + 2 * N_EXPERTS + e]
        for b in range(RUN_BITS):
            size = RUN_ALIGN << b
            off = (length >> (b + 5)) << (b + 5)

            def piece(size=size, off=off, lstart=lstart, gstart=gstart):
                loc = local_ref.at[pl.ds(pl.multiple_of(lstart + off, RUN_ALIGN), size)]
                glo = glob_hbm.at[pl.ds(pl.multiple_of(gstart + off, RUN_ALIGN), size)]
                cp = pltpu.make_async_copy(loc, glo, sem) if to_global else pltpu.make_async_copy(glo, loc, sem)
                if start:
                    cp.start()
                else:
                    cp.wait()

            pl.when(((length >> (b + 4)) & 1) == 1)(piece)


def _fill_copies(fill_ref, zero_ref, glob_hbm, sem, *, te, n_spare, start):
    def go(cp):
        if start:
            cp.start()
        else:
            cp.wait()

    for e in range(N_EXPERTS):
        tstart = fill_ref[e]
        length = fill_ref[N_EXPERTS + e]
        for b in range(TAIL_BITS):
            size = RUN_ALIGN << b
            off = (length >> (b + 5)) << (b + 5)

            def piece(size=size, off=off, tstart=tstart):
                go(pltpu.make_async_copy(zero_ref.at[pl.ds(0, size)],
                                         glob_hbm.at[pl.ds(pl.multiple_of(tstart + off, RUN_ALIGN), size)], sem))

            pl.when(((length >> (b + 4)) & 1) == 1)(piece)
    spare0 = fill_ref[2 * N_EXPERTS]
    for j in range(n_spare):
        def tile_fill(j=j):
            go(pltpu.make_async_copy(zero_ref, glob_hbm.at[pl.ds(pl.multiple_of(spare0 + j * te, te), te)], sem))

        pl.when(j < fill_ref[2 * N_EXPERTS + 1])(tile_fill)


def _dispatch_kernel(tab_ref, fill_ref, x_ref, meta_ref, xs_hbm, loc_ref, zero_ref, sem, zsem, *,
                     tm, te, n_spare):
    i = pl.program_id(0)
    d = x_ref.shape[1]

    @pl.when(i == 0)
    def _():
        zero_ref[...] = jnp.zeros(zero_ref.shape, BF16)
        _fill_copies(fill_ref, zero_ref, xs_hbm, zsem, te=te, n_spare=n_spare, start=True)

    meta = meta_ref[...]
    meta_t = meta.T
    s0 = meta_t[_M_S0:_M_S0 + 1, :]
    s1 = meta_t[_M_S1:_M_S1 + 1, :]
    slot = lax.broadcasted_iota(jnp.int32, (SLOTS, tm), 0).astype(F32)
    p0 = slot == s0
    p1 = slot == s1
    onehot = jnp.where(p0 | p1, 1.0, 0.0).astype(BF16)
    loc_ref[:, 0:d] = jnp.dot(onehot, x_ref[...].astype(BF16), preferred_element_type=F32).astype(BF16)

    lane = lax.broadcasted_iota(jnp.int32, (tm, GATE_LANES), 1)

    def pieces(g):
        h = g.astype(BF16).astype(F32)
        r1 = g - h
        m = r1.astype(BF16).astype(F32)
        l = (r1 - m).astype(BF16).astype(F32)
        return jnp.where(lane == 0, h, jnp.where(lane == 1, m, jnp.where(lane == 2, l, 0.0))).astype(BF16)

    gate = (jnp.dot(jnp.where(p0, 1.0, 0.0).astype(BF16), pieces(meta[:, _M_G0:_M_G0 + 1]),
                    preferred_element_type=F32)
            + jnp.dot(jnp.where(p1, 1.0, 0.0).astype(BF16), pieces(meta[:, _M_G1:_M_G1 + 1]),
                      preferred_element_type=F32))
    loc_ref[:, d:d + GATE_LANES] = gate.astype(BF16)

    _run_copies(tab_ref, i, loc_ref, xs_hbm, sem, to_global=True, start=True)
    _run_copies(tab_ref, i, loc_ref, xs_hbm, sem, to_global=True, start=False)

    @pl.when(i == pl.num_programs(0) - 1)
    def _():
        _fill_copies(fill_ref, zero_ref, xs_hbm, zsem, te=te, n_spare=n_spare, start=False)


def _dispatch(x, meta, run_tab, fill_tab, n_rows, n_spare):
    n, d = x.shape
    tm, te = TM_MOE, TM_EXPERT
    width = d + GATE_LANES
    grid_spec = pltpu.PrefetchScalarGridSpec(
        num_scalar_prefetch=2,
        grid=(n // tm,),
        in_specs=[pl.BlockSpec((tm, d), lambda i, *_: (i, 0)),
                  pl.BlockSpec((tm, LANES), lambda i, *_: (i, 0))],
        out_specs=pl.BlockSpec(memory_space=pl.ANY),
        scratch_shapes=[pltpu.VMEM((SLOTS, width), BF16), pltpu.VMEM((te, width), BF16),
                        pltpu.SemaphoreType.DMA, pltpu.SemaphoreType.DMA],
    )
    return pl.pallas_call(
        functools.partial(_dispatch_kernel, tm=tm, te=te, n_spare=n_spare),
        out_shape=jax.ShapeDtypeStruct((n_rows, width), BF16),
        grid_spec=grid_spec,
        compiler_params=_params(("arbitrary",)),
        name="dispatch",
    )(run_tab, fill_tab, x, meta)


def _experts_kernel(tile_ref, expert_ref, nused_ref, xs_ref, wg_ref, wu_ref, wd_ref, ys_ref, acc_ref):
    i = pl.program_id(0)
    f = pl.program_id(1)
    d = ys_ref.shape[1]
    used = i < nused_ref[0]

    @pl.when(jnp.logical_not(used) & (f == 0))
    def _():
        ys_ref[...] = jnp.zeros(ys_ref.shape, F32)

    @pl.when(used)
    def _():
        xb = xs_ref[:, 0:d]
        g = jnp.dot(xb, wg_ref[0], preferred_element_type=F32)
        u = jnp.dot(xb, wu_ref[0], preferred_element_type=F32)
        h = (_silu(g) * u).astype(BF16)
        part = jnp.dot(h, wd_ref[0], preferred_element_type=F32)

        @pl.when(f == 0)
        def _():
            acc_ref[...] = part

        @pl.when(f > 0)
        def _():
            acc_ref[...] += part

        @pl.when(f == pl.num_programs(1) - 1)
        def _():
            gate = jnp.sum(xs_ref[:, d:d + GATE_LANES].astype(F32), axis=-1, keepdims=True)
            ys_ref[...] = acc_ref[...] * gate


def _experts(xs, we_gate, we_up, we_down, tile_idx, tile_expert, n_used):
    n_rows, width = xs.shape
    d = width - GATE_LANES
    te = TM_EXPERT
    ff = we_gate.shape[-1]
    fc = ff // FF_CHUNKS
    assert n_rows % te == 0 and ff % FF_CHUNKS == 0 and fc % LANES == 0
    n_tiles = n_rows // te
    last = FF_CHUNKS - 1
    grid_spec = pltpu.PrefetchScalarGridSpec(
        num_scalar_prefetch=3,
        grid=(n_tiles, FF_CHUNKS),
        in_specs=[
            pl.BlockSpec((te, width), lambda i, f, tile, ex, nu: (tile[i], 0)),
            pl.BlockSpec((1, d, fc), lambda i, f, tile, ex, nu: (ex[i], 0, jnp.where(i < nu[0], f, last))),
            pl.BlockSpec((1, d, fc), lambda i, f, tile, ex, nu: (ex[i], 0, jnp.where(i < nu[0], f, last))),
            pl.BlockSpec((1, fc, d), lambda i, f, tile, ex, nu: (ex[i], jnp.where(i < nu[0], f, last), 0)),
        ],
        out_specs=pl.BlockSpec((te, d), lambda i, f, tile, ex, nu: (i, 0)),
        scratch_shapes=[pltpu.VMEM((te, d), F32)],
    )
    return pl.pallas_call(
        _experts_kernel,
        out_shape=jax.ShapeDtypeStruct((n_rows, d), F32),
        grid_spec=grid_spec,
        compiler_params=_params(("arbitrary", "arbitrary")),
        name="experts",
    )(tile_idx, tile_expert, n_used, xs, we_gate, we_up, we_down)


def _combine_kernel(tab_ref, x_ref, meta_ref, lng_ref, lnb_ref, ys_hbm, o_ref, loc_ref, sem, *, tm, alpha):
    i = pl.program_id(0)
    n_steps = pl.num_programs(0)
    slot = lax.rem(i, 2)

    @pl.when(i == 0)
    def _():
        loc_ref[...] = jnp.zeros(loc_ref.shape, F32)
        _run_copies(tab_ref, 0, loc_ref.at[0], ys_hbm, sem.at[0], to_global=False, start=True)

    @pl.when(i + 1 < n_steps)
    def _():
        nxt = 1 - slot
        _run_copies(tab_ref, i + 1, loc_ref.at[nxt], ys_hbm, sem.at[nxt], to_global=False, start=True)

    _run_copies(tab_ref, i, loc_ref.at[slot], ys_hbm, sem.at[slot], to_global=False, start=False)

    meta = meta_ref[...]
    s0 = meta[:, _M_S0:_M_S0 + 1]
    s1 = meta[:, _M_S1:_M_S1 + 1]
    lane = lax.broadcasted_iota(jnp.int32, (tm, SLOTS), 1).astype(F32)
    pick = jnp.where((lane == s0) | (lane == s1), 1.0, 0.0).astype(BF16)
    y = loc_ref[slot]
    y_hi = y.astype(BF16)
    y_lo = (y - y_hi.astype(F32)).astype(BF16)
    moe = (jnp.dot(pick, y_hi, preferred_element_type=F32)
           + jnp.dot(pick, y_lo, preferred_element_type=F32))
    o_ref[...] = _layer_norm(alpha * x_ref[...] + moe, lng_ref[...], lnb_ref[...])


def _combine_ln(x, meta, run_tab, ys, ln_g, ln_b, *, alpha):
    n, d = x.shape
    tm = TM_MOE
    n_steps = n // tm
    row = pl.BlockSpec((tm, d), lambda i, *_: (i, 0))
    grid_spec = pltpu.PrefetchScalarGridSpec(
        num_scalar_prefetch=1,
        grid=(n_steps,),
        in_specs=[row, pl.BlockSpec((tm, LANES), lambda i, *_: (i, 0)),
                  pl.BlockSpec(ln_g.shape, lambda i, *_: (0, 0)), pl.BlockSpec(ln_b.shape, lambda i, *_: (0, 0)),
                  pl.BlockSpec(memory_space=pl.ANY)],
        out_specs=row,
        scratch_shapes=[pltpu.VMEM((2, SLOTS, d), F32), pltpu.SemaphoreType.DMA((2,))],
    )
    return pl.pallas_call(
        functools.partial(_combine_kernel, tm=tm, alpha=alpha),
        out_shape=jax.ShapeDtypeStruct((n, d), F32),
        grid_spec=grid_spec,
        compiler_params=_params(("arbitrary",)),
        name="combine_ln",
    )(run_tab, x, meta, ln_g, ln_b, ys)


def _moe_ln(x, w_router, we_gate, we_up, we_down, ln_g, ln_b, *, alpha):
    n, d = x.shape
    tm, te = TM_MOE, TM_EXPERT
    assert n % tm == 0 and tm <= RUN_ALIGN << (RUN_BITS - 1) and te <= RUN_ALIGN << TAIL_BITS
    nt = n // tm
    wr = jnp.pad(w_router, ((0, 0), (0, LANES - N_EXPERTS)))
    wr_hi = wr.astype(BF16)
    wr_lo = (wr - wr_hi.astype(F32)).astype(BF16)
    meta, runs, totals = _router(x, wr_hi, wr_lo)

    padded = totals[0, :N_EXPERTS].astype(jnp.int32)
    tiles_e = (padded + te - 1) // te
    tile_end = jnp.cumsum(tiles_e)
    region = (tile_end - tiles_e) * te
    n_used = tile_end[-1]
    max_rows = TOP_K * n + nt * N_EXPERTS * (RUN_ALIGN - 1) + N_EXPERTS * (te - RUN_ALIGN)
    n_tiles = -(-max_rows // te)
    n_spare = n_tiles - (TOP_K * n) // te
    runs_i = runs.reshape(nt, SUBLANES, LANES)[:, :3, :N_EXPERTS].astype(jnp.int32)
    runs_i = runs_i.at[:, _R_GOFF, :].add(region)
    run_tab = runs_i.reshape(-1)
    fill_tab = jnp.concatenate([region + padded, tiles_e * te - padded,
                                (n_used * te)[None], (n_tiles - n_used)[None]]).astype(jnp.int32)
    j = jnp.minimum(jnp.arange(n_tiles, dtype=jnp.int32), n_used - 1)
    tile_expert = jnp.minimum(jnp.sum(j[:, None] >= tile_end[None, :], axis=-1), N_EXPERTS - 1).astype(jnp.int32)

    xs = _dispatch(x, meta, run_tab, fill_tab, n_tiles * te, n_spare)
    ys = _experts(xs, we_gate, we_up, we_down, j, tile_expert, n_used.reshape(1))
    return _combine_ln(x, meta, run_tab, ys, ln_g, ln_b, alpha=alpha)


def kernel(x, w_in, conv_w, v_g, v_b, w_s, b_s, out_g, w_out, ln1_g, ln1_b, ln2_g, ln2_b,
           w_gate, w_up, w_down, w_router, we_gate, we_up, we_down):
    bsz, seq, d = x.shape
    depth = w_in.shape[0]
    alpha = float((2 * depth) ** 0.25)
    h = x.reshape(bsz * seq, d)
    row = lambda a: a.reshape(1, -1)
    for i in range(depth):
        bias_t = jnp.repeat(b_s[i].T, GMLP_HEAD_DIM, axis=1)
        h = _mixer_ln(h, w_in[i].astype(BF16), conv_w[i], row(v_g[i]), row(v_b[i]), w_s[i], bias_t,
                      row(out_g[i]), w_out[i].astype(BF16), row(ln1_g[i]), row(ln1_b[i]), seq=seq, alpha=alpha)
        if i % 2 == 0:
            j = i // 2
            h = _ffn_ln(h, w_gate[j].astype(BF16), w_up[j].astype(BF16), w_down[j].astype(BF16),
                        row(ln2_g[i]), row(ln2_b[i]), alpha=alpha)
        else:
            j = i // 2
            h = _moe_ln(h, w_router[j], we_gate[j].astype(BF16), we_up[j].astype(BF16),
                        we_down[j].astype(BF16), row(ln2_g[i]), row(ln2_b[i]), alpha=alpha)
    return h.reshape(bsz, seq, d)
```

```python
import functools

import jax
import jax.numpy as jnp
from jax import lax
from jax.experimental import pallas as pl
from jax.experimental.pallas import tpu as pltpu

F32 = jnp.float32
BF16 = jnp.bfloat16

CONV_WIDTH = 512
CONV_HEAD_DIM = 64
GMLP_WIDTH = 512
GMLP_HEADS = 4
GMLP_HEAD_DIM = 128
CHUNK = 128
N_EXPERTS = 8
TOP_K = 2
LN_EPS = 1e-5
RMS_EPS = 1e-6

LANES = 128
SUBLANES = 8
BF16_ROWS = 16
VMEM_LIMIT_BYTES = 56 * 1024 * 1024

TM_MIX = 1024
SUB_MIX = 512
TM_FFN = 512
TM_MOE = 512
TM_EXPERT = 1024
SUB_EXPERT = 512
FF_CHUNKS = 2

RUN_ALIGN = BF16_ROWS
RUN_BITS = (TM_MOE // RUN_ALIGN).bit_length()
SLOTS = -(-(TOP_K * TM_MOE + N_EXPERTS * (RUN_ALIGN - 1)) // LANES) * LANES
GATE_LANES = LANES
TAIL_BITS = (TM_EXPERT // RUN_ALIGN - 1).bit_length()


def _layer_norm(r, g, b):
    mu = jnp.mean(r, axis=-1, keepdims=True)
    d = r - mu
    var = jnp.mean(d * d, axis=-1, keepdims=True)
    return d * lax.rsqrt(var + LN_EPS) * g + b


def _gelu(x):
    return 0.5 * x * (1.0 + lax.erf(x * (2.0 ** -0.5)))


def _silu(x):
    return x * (1.0 / (1.0 + jnp.exp(-x)))


def _params(semantics):
    return pltpu.CompilerParams(dimension_semantics=semantics, vmem_limit_bytes=VMEM_LIMIT_BYTES)


def _const_spec(shape):
    nd = len(shape)
    return pl.BlockSpec(shape, lambda *_: (0,) * nd)


def _mixer_kernel(x_ref, win_ref, convw_ref, vg_ref, vb_ref, ws_ref, bias_ref, outg_ref, wout_ref,
                  lng_ref, lnb_ref, o_ref, ch_ref, *, tm, sub, seq, alpha):
    i = pl.program_id(0)
    cw_, gw_ = CONV_WIDTH, GMLP_WIDTH

    @pl.when(lax.rem(i * tm, seq) == 0)
    def _():
        ch_ref[0:SUBLANES, :] = jnp.zeros((SUBLANES, cw_), F32)

    trow = lax.broadcasted_iota(jnp.int32, (CHUNK, CHUNK), 0)
    tcol = lax.broadcasted_iota(jnp.int32, (CHUNK, CHUNK), 1)
    causal = tcol <= trow
    ws = [jnp.where(causal, ws_ref[h], 0.0).astype(BF16) for h in range(GMLP_HEADS)]
    lane = lax.broadcasted_iota(jnp.int32, (sub, LANES), 1)
    low_half = lane < CONV_HEAD_DIM
    nch = sub // CHUNK

    for r0 in range(0, tm, sub):
        x = x_ref[r0:r0 + sub, :]
        proj = jnp.dot(x.astype(BF16), win_ref[...], preferred_element_type=F32)
        b_gate = proj[:, 0:cw_]
        c_gate = proj[:, cw_:2 * cw_]
        hh = proj[:, 2 * cw_:3 * cw_]
        u = proj[:, 3 * cw_:3 * cw_ + gw_]
        v = proj[:, 3 * cw_ + gw_:3 * cw_ + 2 * gw_]

        c0 = SUBLANES + r0
        ch_ref[c0:c0 + sub, :] = c_gate * hh
        convw = convw_ref[...]
        conv = (convw[0:1, :] * ch_ref[c0 - 2:c0 - 2 + sub, :]
                + convw[1:2, :] * ch_ref[c0 - 1:c0 - 1 + sub, :]
                + convw[2:3, :] * ch_ref[c0:c0 + sub, :])
        y_conv = b_gate * conv

        ug = _gelu(u)
        vn = _layer_norm(_gelu(v), vg_ref[...], vb_ref[...]).astype(BF16)
        zs = []
        for h in range(GMLP_HEADS):
            lo, hi = h * GMLP_HEAD_DIM, (h + 1) * GMLP_HEAD_DIM
            rhs = jnp.concatenate([vn[c * CHUNK:(c + 1) * CHUNK, lo:hi] for c in range(nch)], axis=1)
            zs.append(jnp.dot(ws[h], rhs, preferred_element_type=F32))
        z = jnp.concatenate(
            [jnp.concatenate([zs[h][:, c * GMLP_HEAD_DIM:(c + 1) * GMLP_HEAD_DIM]
                              for h in range(GMLP_HEADS)], axis=1) for c in range(nch)], axis=0)
        bias = jnp.concatenate([bias_ref[...]] * nch, axis=0)
        y_sg = ug * (z + bias)

        parts = []
        for j in range(cw_ // LANES):
            yt = y_conv[:, j * LANES:(j + 1) * LANES]
            sq = yt * yt
            ms_lo = jnp.sum(jnp.where(low_half, sq, 0.0), axis=-1, keepdims=True) * (1.0 / CONV_HEAD_DIM)
            ms_hi = jnp.sum(jnp.where(low_half, 0.0, sq), axis=-1, keepdims=True) * (1.0 / CONV_HEAD_DIM)
            parts.append(yt * jnp.where(low_half, lax.rsqrt(ms_lo + RMS_EPS), lax.rsqrt(ms_hi + RMS_EPS)))
        for h in range(GMLP_HEADS):
            yt = y_sg[:, h * GMLP_HEAD_DIM:(h + 1) * GMLP_HEAD_DIM]
            ms = jnp.mean(yt * yt, axis=-1, keepdims=True)
            parts.append(yt * lax.rsqrt(ms + RMS_EPS))
        y = jnp.concatenate(parts, axis=1) * outg_ref[...]

        mix = jnp.dot(y.astype(BF16), wout_ref[...], preferred_element_type=F32)
        o_ref[r0:r0 + sub, :] = _layer_norm(alpha * x + mix, lng_ref[...], lnb_ref[...])

    ch_ref[0:SUBLANES, :] = ch_ref[tm:tm + SUBLANES, :]


def _mixer_ln(x, w_in, conv_w, v_g, v_b, w_s, bias_t, out_g, w_out, ln_g, ln_b, *, seq, alpha):
    n, d = x.shape
    tm, sub = TM_MIX, SUB_MIX
    assert n % tm == 0 and seq % tm == 0 and tm % sub == 0 and sub % CHUNK == 0
    kern = functools.partial(_mixer_kernel, tm=tm, sub=sub, seq=seq, alpha=alpha)
    row = pl.BlockSpec((tm, d), lambda i: (i, 0))
    return pl.pallas_call(
        kern,
        out_shape=jax.ShapeDtypeStruct((n, d), F32),
        grid=(n // tm,),
        in_specs=[row, _const_spec(w_in.shape), _const_spec(conv_w.shape), _const_spec(v_g.shape),
                  _const_spec(v_b.shape), _const_spec(w_s.shape), _const_spec(bias_t.shape),
                  _const_spec(out_g.shape), _const_spec(w_out.shape), _const_spec(ln_g.shape),
                  _const_spec(ln_b.shape)],
        out_specs=row,
        scratch_shapes=[pltpu.VMEM((tm + SUBLANES, CONV_WIDTH), F32)],
        compiler_params=_params(("arbitrary",)),
        name="mixer_ln",
    )(x, w_in, conv_w, v_g, v_b, w_s, bias_t, out_g, w_out, ln_g, ln_b)


def _ffn_kernel(x_ref, wg_ref, wu_ref, wd_ref, lng_ref, lnb_ref, o_ref, *, alpha):
    x = x_ref[...]
    xb = x.astype(BF16)
    g = jnp.dot(xb, wg_ref[...], preferred_element_type=F32)
    u = jnp.dot(xb, wu_ref[...], preferred_element_type=F32)
    h = (_silu(g) * u).astype(BF16)
    ffn = jnp.dot(h, wd_ref[...], preferred_element_type=F32)
    o_ref[...] = _layer_norm(alpha * x + ffn, lng_ref[...], lnb_ref[...])


def _ffn_ln(x, wg, wu, wd, ln_g, ln_b, *, alpha):
    n, d = x.shape
    tm = TM_FFN
    assert n % tm == 0
    row = pl.BlockSpec((tm, d), lambda i: (i, 0))
    return pl.pallas_call(
        functools.partial(_ffn_kernel, alpha=alpha),
        out_shape=jax.ShapeDtypeStruct((n, d), F32),
        grid=(n // tm,),
        in_specs=[row, _const_spec(wg.shape), _const_spec(wu.shape), _const_spec(wd.shape),
                  _const_spec(ln_g.shape), _const_spec(ln_b.shape)],
        out_specs=row,
        compiler_params=_params(("arbitrary",)),
        name="ffn_ln",
    )(x, wg, wu, wd, ln_g, ln_b)


_M_E0, _M_E1, _M_S0, _M_S1, _M_G0, _M_G1 = range(6)
_R_LSTART, _R_LEN, _R_GOFF = range(3)


def _router_kernel(x_ref, wrh_ref, wrl_ref, meta_ref, runs_ref, tot_ref, base_ref, *, tm):
    i = pl.program_id(0)

    @pl.when(i == 0)
    def _():
        base_ref[...] = jnp.zeros(base_ref.shape, F32)

    x = x_ref[...]
    xh = x.astype(BF16)
    xl = (x - xh.astype(F32)).astype(BF16)
    wrh = wrh_ref[...]
    logits = (jnp.dot(xh, wrh, preferred_element_type=F32)
              + jnp.dot(xl, wrh, preferred_element_type=F32)
              + jnp.dot(xh, wrl_ref[...], preferred_element_type=F32))
    lane = lax.broadcasted_iota(jnp.int32, (tm, LANES), 1)
    lanef = lane.astype(F32)
    neg = jnp.float32(-jnp.inf)
    lg = jnp.where(lane < N_EXPERTS, logits, neg)
    m0 = jnp.max(lg, axis=-1, keepdims=True)
    e0 = jnp.min(jnp.where(lg == m0, lanef, float(LANES)), axis=-1, keepdims=True)
    lg1 = jnp.where(lanef == e0, neg, lg)
    m1 = jnp.max(lg1, axis=-1, keepdims=True)
    e1 = jnp.min(jnp.where(lg1 == m1, lanef, float(LANES)), axis=-1, keepdims=True)
    t = jnp.exp(m1 - m0)
    g0 = 1.0 / (1.0 + t)
    g1 = t / (1.0 + t)

    oh0 = lanef == e0
    oh1 = lanef == e1
    cnt = jnp.where(oh0 | oh1, 1.0, 0.0)
    r_i = lax.broadcasted_iota(jnp.int32, (tm, tm), 0)
    c_i = lax.broadcasted_iota(jnp.int32, (tm, tm), 1)
    tri = jnp.where(c_i < r_i, 1.0, 0.0).astype(BF16)
    rank = jnp.dot(tri, cnt.astype(BF16), preferred_element_type=F32)

    count = jnp.broadcast_to(jnp.sum(cnt, axis=0, keepdims=True), (SUBLANES, LANES))
    run_len = jnp.floor((count + (RUN_ALIGN - 1.0)) * (1.0 / RUN_ALIGN)) * RUN_ALIGN
    lane8 = lax.broadcasted_iota(jnp.int32, (SUBLANES, LANES), 1)
    incl = run_len
    for sh in (1, 2, 4):
        incl = incl + jnp.where(lane8 >= sh, pltpu.roll(incl, sh, axis=1), 0.0)
    lstart = incl - run_len
    goff = base_ref[...]
    base_ref[...] = goff + run_len
    tot_ref[...] = goff + run_len
    row8 = lax.broadcasted_iota(jnp.int32, (SUBLANES, LANES), 0)
    runs_ref[...] = jnp.where(row8 == _R_LSTART, lstart,
                              jnp.where(row8 == _R_LEN, run_len, jnp.where(row8 == _R_GOFF, goff, 0.0)))

    slot = rank + lstart[0:1, :]
    s0 = jnp.sum(jnp.where(oh0, slot, 0.0), axis=-1, keepdims=True)
    s1 = jnp.sum(jnp.where(oh1, slot, 0.0), axis=-1, keepdims=True)
    meta = jnp.zeros((tm, LANES), F32)
    for k, val in ((_M_E0, e0), (_M_E1, e1), (_M_S0, s0), (_M_S1, s1), (_M_G0, g0), (_M_G1, g1)):
        meta = jnp.where(lane == k, val, meta)
    meta_ref[...] = meta


def _router(x, wr_hi, wr_lo):
    n, d = x.shape
    tm = TM_MOE
    assert n % tm == 0
    nt = n // tm
    small = (SUBLANES, LANES)
    return pl.pallas_call(
        functools.partial(_router_kernel, tm=tm),
        out_shape=(jax.ShapeDtypeStruct((n, LANES), F32), jax.ShapeDtypeStruct((nt * SUBLANES, LANES), F32),
                   jax.ShapeDtypeStruct(small, F32)),
        grid=(nt,),
        in_specs=[pl.BlockSpec((tm, d), lambda i: (i, 0)), _const_spec(wr_hi.shape), _const_spec(wr_lo.shape)],
        out_specs=(pl.BlockSpec((tm, LANES), lambda i: (i, 0)), pl.BlockSpec(small, lambda i: (i, 0)),
                   _const_spec(small)),
        scratch_shapes=[pltpu.VMEM(small, F32)],
        compiler_params=_params(("arbitrary",)),
        name="router",
    )(x, wr_hi, wr_lo)


def _run_copies(tab_ref, tile, local_ref, glob_hbm, sem, *, to_global, start):
    base = tile * (3 * N_EXPERTS)
    for e in range(N_EXPERTS):
        lstart = tab_ref[base + e]
        length = tab_ref[base + N_EXPERTS + e]
        gstart = tab_ref[base + 2 * N_EXPERTS + e]
        for b in range(RUN_BITS):
            size = RUN_ALIGN << b
            off = (length >> (b + 5)) << (b + 5)

            def piece(size=size, off=off, lstart=lstart, gstart=gstart):
                loc = local_ref.at[pl.ds(pl.multiple_of(lstart + off, RUN_ALIGN), size)]
                glo = glob_hbm.at[pl.ds(pl.multiple_of(gstart + off, RUN_ALIGN), size)]
                cp = pltpu.make_async_copy(loc, glo, sem) if to_global else pltpu.make_async_copy(glo, loc, sem)
                if start:
                    cp.start()
                else:
                    cp.wait()

            pl.when(((length >> (b + 4)) & 1) == 1)(piece)


def _fill_copies(fill_ref, zero_ref, glob_hbm, sem, *, te, n_spare, start):
    def go(cp):
        if start:
            cp.start()
        else:
            cp.wait()

    for e in range(N_EXPERTS):
        tstart = fill_ref[e]
        length = fill_ref[N_EXPERTS + e]
        for b in range(TAIL_BITS):
            size = RUN_ALIGN << b
            off = (length >> (b + 5)) << (b + 5)

            def piece(size=size, off=off, tstart=tstart):
                go(pltpu.make_async_copy(zero_ref.at[pl.ds(0, size)],
                                         glob_hbm.at[pl.ds(pl.multiple_of(tstart + off, RUN_ALIGN), size)], sem))

            pl.when(((length >> (b + 4)) & 1) == 1)(piece)
    spare0 = fill_ref[2 * N_EXPERTS]
    for j in range(n_spare):
        def tile_fill(j=j):
            go(pltpu.make_async_copy(zero_ref, glob_hbm.at[pl.ds(pl.multiple_of(spare0 + j * te, te), te)], sem))

        pl.when(j < fill_ref[2 * N_EXPERTS + 1])(tile_fill)


def _dispatch_kernel(tab_ref, fill_ref, x_ref, meta_ref, xs_hbm, loc_ref, zero_ref, sem, zsem, *,
                     tm, te, n_spare):
    i = pl.program_id(0)
    d = x_ref.shape[1]

    @pl.when(i == 0)
    def _():
        zero_ref[...] = jnp.zeros(zero_ref.shape, BF16)
        _fill_copies(fill_ref, zero_ref, xs_hbm, zsem, te=te, n_spare=n_spare, start=True)

    meta = meta_ref[...]
    meta_t = meta.T
    s0 = meta_t[_M_S0:_M_S0 + 1, :]
    s1 = meta_t[_M_S1:_M_S1 + 1, :]
    slot = lax.broadcasted_iota(jnp.int32, (SLOTS, tm), 0).astype(F32)
    p0 = slot == s0
    p1 = slot == s1
    onehot = jnp.where(p0 | p1, 1.0, 0.0).astype(BF16)
    loc_ref[:, 0:d] = jnp.dot(onehot, x_ref[...].astype(BF16), preferred_element_type=F32).astype(BF16)

    lane = lax.broadcasted_iota(jnp.int32, (tm, GATE_LANES), 1)

    def pieces(g):
        h = g.astype(BF16).astype(F32)
        r1 = g - h
        m = r1.astype(BF16).astype(F32)
        l = (r1 - m).astype(BF16).astype(F32)
        return jnp.where(lane == 0, h, jnp.where(lane == 1, m, jnp.where(lane == 2, l, 0.0))).astype(BF16)

    gate = (jnp.dot(jnp.where(p0, 1.0, 0.0).astype(BF16), pieces(meta[:, _M_G0:_M_G0 + 1]),
                    preferred_element_type=F32)
            + jnp.dot(jnp.where(p1, 1.0, 0.0).astype(BF16), pieces(meta[:, _M_G1:_M_G1 + 1]),
                      preferred_element_type=F32))
    loc_ref[:, d:d + GATE_LANES] = gate.astype(BF16)

    _run_copies(tab_ref, i, loc_ref, xs_hbm, sem, to_global=True, start=True)
    _run_copies(tab_ref, i, loc_ref, xs_hbm, sem, to_global=True, start=False)

    @pl.when(i == pl.num_programs(0) - 1)
    def _():
        _fill_copies(fill_ref, zero_ref, xs_hbm, zsem, te=te, n_spare=n_spare, start=False)


def _dispatch(x, meta, run_tab, fill_tab, n_rows, n_spare):
    n, d = x.shape
    tm, te = TM_MOE, TM_EXPERT
    width = d + GATE_LANES
    grid_spec = pltpu.PrefetchScalarGridSpec(
        num_scalar_prefetch=2,
        grid=(n // tm,),
        in_specs=[pl.BlockSpec((tm, d), lambda i, *_: (i, 0)),
                  pl.BlockSpec((tm, LANES), lambda i, *_: (i, 0))],
        out_specs=pl.BlockSpec(memory_space=pl.ANY),
        scratch_shapes=[pltpu.VMEM((SLOTS, width), BF16), pltpu.VMEM((te, width), BF16),
                        pltpu.SemaphoreType.DMA, pltpu.SemaphoreType.DMA],
    )
    return pl.pallas_call(
        functools.partial(_dispatch_kernel, tm=tm, te=te, n_spare=n_spare),
        out_shape=jax.ShapeDtypeStruct((n_rows, width), BF16),
        grid_spec=grid_spec,
        compiler_params=_params(("arbitrary",)),
        name="dispatch",
    )(run_tab, fill_tab, x, meta)


def _experts_kernel(tile_ref, expert_ref, rows_ref, xs_ref, wg_ref, wu_ref, wd_ref, ys_ref, acc_ref, *, sub):
    i = pl.program_id(0)
    f = pl.program_id(1)
    te, d = ys_ref.shape
    rows = rows_ref[i]

    @pl.when((rows == 0) & (f == 0))
    def _():
        ys_ref[...] = jnp.zeros(ys_ref.shape, ys_ref.dtype)

    def partial_out(r0):
        xb = xs_ref[r0:r0 + sub, 0:d]
        g = jnp.dot(xb, wg_ref[0], preferred_element_type=F32)
        u = jnp.dot(xb, wu_ref[0], preferred_element_type=F32)
        h = (_silu(g) * u).astype(BF16)
        return jnp.dot(h, wd_ref[0], preferred_element_type=F32)

    def first_chunk(n_rows):
        for r0 in range(0, n_rows, sub):
            acc_ref[r0:r0 + sub, :] = partial_out(r0)

    def last_chunk(n_rows):
        for r0 in range(0, n_rows, sub):
            gate = jnp.sum(xs_ref[r0:r0 + sub, d:d + GATE_LANES].astype(F32), axis=-1, keepdims=True)
            ys_ref[r0:r0 + sub, :] = ((acc_ref[r0:r0 + sub, :] + partial_out(r0)) * gate).astype(ys_ref.dtype)
        if n_rows < te:
            ys_ref[n_rows:te, :] = jnp.zeros((te - n_rows, d), ys_ref.dtype)

    for nb in range(1, te // sub + 1):
        fits = (rows > (nb - 1) * sub) & (rows <= nb * sub)
        pl.when(fits & (f == 0))(functools.partial(first_chunk, nb * sub))
        pl.when(fits & (f == 1))(functools.partial(last_chunk, nb * sub))


def _experts(xs, we_gate, we_up, we_down, tile_idx, tile_expert, tile_rows):
    n_rows, width = xs.shape
    d = width - GATE_LANES
    te = TM_EXPERT
    ff = we_gate.shape[-1]
    fc = ff // FF_CHUNKS
    assert n_rows % te == 0 and ff % FF_CHUNKS == 0 and fc % LANES == 0 and FF_CHUNKS == 2
    n_tiles = n_rows // te
    last = FF_CHUNKS - 1
    grid_spec = pltpu.PrefetchScalarGridSpec(
        num_scalar_prefetch=3,
        grid=(n_tiles, FF_CHUNKS),
        in_specs=[
            pl.BlockSpec((te, width), lambda i, f, tile, ex, rows: (tile[i], 0)),
            pl.BlockSpec((1, d, fc), lambda i, f, tile, ex, rows: (ex[i], 0, jnp.where(rows[i] > 0, f, last))),
            pl.BlockSpec((1, d, fc), lambda i, f, tile, ex, rows: (ex[i], 0, jnp.where(rows[i] > 0, f, last))),
            pl.BlockSpec((1, fc, d), lambda i, f, tile, ex, rows: (ex[i], jnp.where(rows[i] > 0, f, last), 0)),
        ],
        out_specs=pl.BlockSpec((te, d), lambda i, f, tile, ex, rows: (i, 0)),
        scratch_shapes=[pltpu.VMEM((te, d), F32)],
    )
    return pl.pallas_call(
        functools.partial(_experts_kernel, sub=SUB_EXPERT),
        out_shape=jax.ShapeDtypeStruct((n_rows, d), BF16),
        grid_spec=grid_spec,
        compiler_params=_params(("arbitrary", "arbitrary")),
        name="experts",
    )(tile_idx, tile_expert, tile_rows, xs, we_gate, we_up, we_down)


def _combine_kernel(tab_ref, x_ref, meta_ref, lng_ref, lnb_ref, ys_hbm, o_ref, loc_ref, sem, *, tm, alpha):
    i = pl.program_id(0)
    n_steps = pl.num_programs(0)
    slot = lax.rem(i, 2)

    @pl.when(i == 0)
    def _():
        loc_ref[...] = jnp.zeros(loc_ref.shape, loc_ref.dtype)
        _run_copies(tab_ref, 0, loc_ref.at[0], ys_hbm, sem.at[0], to_global=False, start=True)

    @pl.when(i + 1 < n_steps)
    def _():
        nxt = 1 - slot
        _run_copies(tab_ref, i + 1, loc_ref.at[nxt], ys_hbm, sem.at[nxt], to_global=False, start=True)

    _run_copies(tab_ref, i, loc_ref.at[slot], ys_hbm, sem.at[slot], to_global=False, start=False)

    meta = meta_ref[...]
    s0 = meta[:, _M_S0:_M_S0 + 1]
    s1 = meta[:, _M_S1:_M_S1 + 1]
    lane = lax.broadcasted_iota(jnp.int32, (tm, SLOTS), 1).astype(F32)
    pick = jnp.where((lane == s0) | (lane == s1), 1.0, 0.0).astype(BF16)
    moe = jnp.dot(pick, loc_ref[slot], preferred_element_type=F32)
    o_ref[...] = _layer_norm(alpha * x_ref[...] + moe, lng_ref[...], lnb_ref[...])


def _combine_ln(x, meta, run_tab, ys, ln_g, ln_b, *, alpha):
    n, d = x.shape
    tm = TM_MOE
    n_steps = n // tm
    row = pl.BlockSpec((tm, d), lambda i, *_: (i, 0))
    grid_spec = pltpu.PrefetchScalarGridSpec(
        num_scalar_prefetch=1,
        grid=(n_steps,),
        in_specs=[row, pl.BlockSpec((tm, LANES), lambda i, *_: (i, 0)),
                  pl.BlockSpec(ln_g.shape, lambda i, *_: (0, 0)), pl.BlockSpec(ln_b.shape, lambda i, *_: (0, 0)),
                  pl.BlockSpec(memory_space=pl.ANY)],
        out_specs=row,
        scratch_shapes=[pltpu.VMEM((2, SLOTS, d), BF16), pltpu.SemaphoreType.DMA((2,))],
    )
    return pl.pallas_call(
        functools.partial(_combine_kernel, tm=tm, alpha=alpha),
        out_shape=jax.ShapeDtypeStruct((n, d), F32),
        grid_spec=grid_spec,
        compiler_params=_params(("arbitrary",)),
        name="combine_ln",
    )(run_tab, x, meta, ln_g, ln_b, ys)


def _moe_ln(x, w_router, we_gate, we_up, we_down, ln_g, ln_b, *, alpha):
    n, d = x.shape
    tm, te = TM_MOE, TM_EXPERT
    assert n % tm == 0 and te % SUB_EXPERT == 0
    nt = n // tm
    wr = jnp.pad(w_router, ((0, 0), (0, LANES - N_EXPERTS)))
    wr_hi = wr.astype(BF16)
    wr_lo = (wr - wr_hi.astype(F32)).astype(BF16)
    meta, runs, totals = _router(x, wr_hi, wr_lo)

    padded = totals[0, :N_EXPERTS].astype(jnp.int32)
    tiles_e = (padded + te - 1) // te
    tile_end = jnp.cumsum(tiles_e)
    region = (tile_end - tiles_e) * te
    n_used = tile_end[-1]
    max_rows = TOP_K * n + nt * N_EXPERTS * (RUN_ALIGN - 1) + N_EXPERTS * (te - RUN_ALIGN)
    n_tiles = -(-max_rows // te)
    n_spare = n_tiles - (TOP_K * n) // te
    runs_i = runs.reshape(nt, SUBLANES, LANES)[:, :3, :N_EXPERTS].astype(jnp.int32)
    runs_i = runs_i.at[:, _R_GOFF, :].add(region)
    run_tab = runs_i.reshape(-1)
    fill_tab = jnp.concatenate([region + padded, tiles_e * te - padded,
                                (n_used * te)[None], (n_tiles - n_used)[None]]).astype(jnp.int32)
    tiles = jnp.arange(n_tiles, dtype=jnp.int32)
    j = jnp.minimum(tiles, n_used - 1)
    tile_expert = jnp.minimum(jnp.sum(j[:, None] >= tile_end[None, :], axis=-1), N_EXPERTS - 1).astype(jnp.int32)
    region_end = jnp.sum(jnp.where(tile_expert[:, None] == jnp.arange(N_EXPERTS), region + padded, 0), axis=-1)
    tile_rows = jnp.where(tiles < n_used, jnp.minimum(region_end - tiles * te, te), 0).astype(jnp.int32)

    xs = _dispatch(x, meta, run_tab, fill_tab, n_tiles * te, n_spare)
    ys = _experts(xs, we_gate, we_up, we_down, j, tile_expert, tile_rows)
    return _combine_ln(x, meta, run_tab, ys, ln_g, ln_b, alpha=alpha)


def kernel(x, w_in, conv_w, v_g, v_b, w_s, b_s, out_g, w_out, ln1_g, ln1_b, ln2_g, ln2_b,
           w_gate, w_up, w_down, w_router, we_gate, we_up, we_down):
    bsz, seq, d = x.shape
    depth = w_in.shape[0]
    alpha = float((2 * depth) ** 0.25)
    h = x.reshape(bsz * seq, d)
    row = lambda a: a.reshape(1, -1)
    for i in range(depth):
        bias_t = jnp.repeat(b_s[i].T, GMLP_HEAD_DIM, axis=1)
        h = _mixer_ln(h, w_in[i].astype(BF16), conv_w[i], row(v_g[i]), row(v_b[i]), w_s[i], bias_t,
                      row(out_g[i]), w_out[i].astype(BF16), row(ln1_g[i]), row(ln1_b[i]), seq=seq, alpha=alpha)
        if i % 2 == 0:
            j = i // 2
            h = _ffn_ln(h, w_gate[j].astype(BF16), w_up[j].astype(BF16), w_down[j].astype(BF16),
                        row(ln2_g[i]), row(ln2_b[i]), alpha=alpha)
        else:
            j = i // 2
            h = _moe_ln(h, w_router[j], we_gate[j].astype(BF16), we_up[j].astype(BF16),
                        we_down[j].astype(BF16), row(ln2_g[i]), row(ln2_b[i]), alpha=alpha)
    return h.reshape(bsz, seq, d)
```

```python
import functools

import jax
import jax.numpy as jnp
from jax import lax
from jax.experimental import pallas as pl
from jax.experimental.pallas import tpu as pltpu

F32 = jnp.float32
BF16 = jnp.bfloat16

CONV_WIDTH = 512
CONV_HEAD_DIM = 64
GMLP_WIDTH = 512
GMLP_HEADS = 4
GMLP_HEAD_DIM = 128
CHUNK = 128
N_EXPERTS = 8
TOP_K = 2
LN_EPS = 1e-5
RMS_EPS = 1e-6

LANES = 128
SUBLANES = 8
BF16_ROWS = 16
VMEM_LIMIT_BYTES = 56 * 1024 * 1024

TM_MIX = 1024
SUB_MIX = 512
TM_FFN = 512
SUB_FFN = 512
TM_MOE = 512
TM_EXPERT = 1024
SUB_EXPERT = 512
FF_CHUNKS = 2

RUN_ALIGN = BF16_ROWS
RUN_BITS = (TM_MOE // RUN_ALIGN).bit_length()
SLOTS = -(-(TOP_K * TM_MOE + N_EXPERTS * (RUN_ALIGN - 1)) // LANES) * LANES
GATE_LANES = LANES
TAIL_BITS = (TM_EXPERT // RUN_ALIGN - 1).bit_length()


def _layer_norm(r, g, b):
    mu = jnp.mean(r, axis=-1, keepdims=True)
    d = r - mu
    var = jnp.mean(d * d, axis=-1, keepdims=True)
    return d * lax.rsqrt(var + LN_EPS) * g + b


def _gelu(x):
    return 0.5 * x * (1.0 + lax.erf(x * (2.0 ** -0.5)))


def _silu(x):
    return x * (1.0 / (1.0 + jnp.exp(-x)))


def _params(semantics):
    return pltpu.CompilerParams(dimension_semantics=semantics, vmem_limit_bytes=VMEM_LIMIT_BYTES)


def _const_spec(shape):
    nd = len(shape)
    return pl.BlockSpec(shape, lambda *_: (0,) * nd)


def _cast_plan(arrays, n_steps):
    views, specs, shapes = [], [], []
    for a in arrays:
        v = a.reshape(-1, a.shape[-1])
        rows = v.shape[0]
        n_blocks = next(nb for nb in range(n_steps, 0, -1)
                        if n_steps % nb == 0 and rows % (nb * BF16_ROWS) == 0)
        steps_per_block = n_steps // n_blocks
        views.append(v)
        specs.append(pl.BlockSpec((rows // n_blocks, v.shape[1]), lambda i, r=steps_per_block: (i // r, 0)))
        shapes.append(jax.ShapeDtypeStruct(v.shape, BF16))
    return views, specs, shapes


def _cast_along(src_refs, dst_refs):
    for src, dst in zip(src_refs, dst_refs):
        dst[...] = src[...].astype(dst.dtype)


def _mixer_kernel(*refs, n_cast, tm, sub, seq, alpha):
    (x_ref, win_ref, convw_ref, vg_ref, vb_ref, ws_ref, bias_ref, outg_ref, wout_ref,
     lng_ref, lnb_ref) = refs[:11]
    o_ref, ch_ref = refs[11 + n_cast], refs[-1]
    _cast_along(refs[11:11 + n_cast], refs[12 + n_cast:12 + 2 * n_cast])
    i = pl.program_id(0)
    cw_, gw_ = CONV_WIDTH, GMLP_WIDTH

    @pl.when(lax.rem(i * tm, seq) == 0)
    def _():
        ch_ref[0:SUBLANES, :] = jnp.zeros((SUBLANES, cw_), F32)

    trow = lax.broadcasted_iota(jnp.int32, (CHUNK, CHUNK), 0)
    tcol = lax.broadcasted_iota(jnp.int32, (CHUNK, CHUNK), 1)
    causal = tcol <= trow
    ws = [jnp.where(causal, ws_ref[h], 0.0).astype(BF16) for h in range(GMLP_HEADS)]
    lane = lax.broadcasted_iota(jnp.int32, (sub, LANES), 1)
    low_half = lane < CONV_HEAD_DIM
    nch = sub // CHUNK

    for r0 in range(0, tm, sub):
        x = x_ref[r0:r0 + sub, :]
        proj = jnp.dot(x.astype(BF16), win_ref[...], preferred_element_type=F32)
        b_gate = proj[:, 0:cw_]
        c_gate = proj[:, cw_:2 * cw_]
        hh = proj[:, 2 * cw_:3 * cw_]
        u = proj[:, 3 * cw_:3 * cw_ + gw_]
        v = proj[:, 3 * cw_ + gw_:3 * cw_ + 2 * gw_]

        c0 = SUBLANES + r0
        ch_ref[c0:c0 + sub, :] = c_gate * hh
        convw = convw_ref[...]
        conv = (convw[0:1, :] * ch_ref[c0 - 2:c0 - 2 + sub, :]
                + convw[1:2, :] * ch_ref[c0 - 1:c0 - 1 + sub, :]
                + convw[2:3, :] * ch_ref[c0:c0 + sub, :])
        y_conv = b_gate * conv

        ug = _gelu(u)
        vn = _layer_norm(_gelu(v), vg_ref[...], vb_ref[...]).astype(BF16)
        zs = []
        for h in range(GMLP_HEADS):
            lo, hi = h * GMLP_HEAD_DIM, (h + 1) * GMLP_HEAD_DIM
            rhs = jnp.concatenate([vn[c * CHUNK:(c + 1) * CHUNK, lo:hi] for c in range(nch)], axis=1)
            zs.append(jnp.dot(ws[h], rhs, preferred_element_type=F32))
        z = jnp.concatenate(
            [jnp.concatenate([zs[h][:, c * GMLP_HEAD_DIM:(c + 1) * GMLP_HEAD_DIM]
                              for h in range(GMLP_HEADS)], axis=1) for c in range(nch)], axis=0)
        bias = jnp.concatenate([bias_ref[...]] * nch, axis=0)
        y_sg = ug * (z + bias)

        parts = []
        for j in range(cw_ // LANES):
            yt = y_conv[:, j * LANES:(j + 1) * LANES]
            sq = yt * yt
            ms_lo = jnp.sum(jnp.where(low_half, sq, 0.0), axis=-1, keepdims=True) * (1.0 / CONV_HEAD_DIM)
            ms_hi = jnp.sum(jnp.where(low_half, 0.0, sq), axis=-1, keepdims=True) * (1.0 / CONV_HEAD_DIM)
            parts.append(yt * jnp.where(low_half, lax.rsqrt(ms_lo + RMS_EPS), lax.rsqrt(ms_hi + RMS_EPS)))
        for h in range(GMLP_HEADS):
            yt = y_sg[:, h * GMLP_HEAD_DIM:(h + 1) * GMLP_HEAD_DIM]
            ms = jnp.mean(yt * yt, axis=-1, keepdims=True)
            parts.append(yt * lax.rsqrt(ms + RMS_EPS))
        y = jnp.concatenate(parts, axis=1) * outg_ref[...]

        mix = jnp.dot(y.astype(BF16), wout_ref[...], preferred_element_type=F32)
        o_ref[r0:r0 + sub, :] = _layer_norm(alpha * x + mix, lng_ref[...], lnb_ref[...])

    ch_ref[0:SUBLANES, :] = ch_ref[tm:tm + SUBLANES, :]


def _mixer_ln(x, w_in, conv_w, v_g, v_b, w_s, bias_t, out_g, w_out, ln_g, ln_b, cast=(), *, seq, alpha):
    n, d = x.shape
    tm, sub = TM_MIX, SUB_MIX
    assert n % tm == 0 and seq % tm == 0 and tm % sub == 0 and sub % CHUNK == 0
    views, cast_specs, cast_shapes = _cast_plan(cast, n // tm)
    kern = functools.partial(_mixer_kernel, n_cast=len(cast), tm=tm, sub=sub, seq=seq, alpha=alpha)
    row = pl.BlockSpec((tm, d), lambda i: (i, 0))
    out = pl.pallas_call(
        kern,
        out_shape=[jax.ShapeDtypeStruct((n, d), F32)] + cast_shapes,
        grid=(n // tm,),
        in_specs=[row, _const_spec(w_in.shape), _const_spec(conv_w.shape), _const_spec(v_g.shape),
                  _const_spec(v_b.shape), _const_spec(w_s.shape), _const_spec(bias_t.shape),
                  _const_spec(out_g.shape), _const_spec(w_out.shape), _const_spec(ln_g.shape),
                  _const_spec(ln_b.shape)] + cast_specs,
        out_specs=[row] + cast_specs,
        scratch_shapes=[pltpu.VMEM((tm + SUBLANES, CONV_WIDTH), F32)],
        compiler_params=_params(("arbitrary",)),
        name="mixer_ln",
    )(x, w_in, conv_w, v_g, v_b, w_s, bias_t, out_g, w_out, ln_g, ln_b, *views)
    return out[0], [o.reshape(a.shape) for o, a in zip(out[1:], cast)]


def _ffn_kernel(*refs, n_cast, sub, alpha):
    x_ref, wg_ref, wu_ref, wd_ref, lng_ref, lnb_ref = refs[:6]
    o_ref = refs[6 + n_cast]
    _cast_along(refs[6:6 + n_cast], refs[7 + n_cast:7 + 2 * n_cast])
    for r0 in range(0, x_ref.shape[0], sub):
        x = x_ref[r0:r0 + sub, :]
        xb = x.astype(BF16)
        g = jnp.dot(xb, wg_ref[...], preferred_element_type=F32)
        u = jnp.dot(xb, wu_ref[...], preferred_element_type=F32)
        h = (_silu(g) * u).astype(BF16)
        ffn = jnp.dot(h, wd_ref[...], preferred_element_type=F32)
        o_ref[r0:r0 + sub, :] = _layer_norm(alpha * x + ffn, lng_ref[...], lnb_ref[...])


def _ffn_ln(x, wg, wu, wd, ln_g, ln_b, cast=(), *, alpha):
    n, d = x.shape
    tm = TM_FFN
    assert n % tm == 0 and tm % SUB_FFN == 0
    views, cast_specs, cast_shapes = _cast_plan(cast, n // tm)
    row = pl.BlockSpec((tm, d), lambda i: (i, 0))
    out = pl.pallas_call(
        functools.partial(_ffn_kernel, n_cast=len(cast), sub=SUB_FFN, alpha=alpha),
        out_shape=[jax.ShapeDtypeStruct((n, d), F32)] + cast_shapes,
        grid=(n // tm,),
        in_specs=[row, _const_spec(wg.shape), _const_spec(wu.shape), _const_spec(wd.shape),
                  _const_spec(ln_g.shape), _const_spec(ln_b.shape)] + cast_specs,
        out_specs=[row] + cast_specs,
        compiler_params=_params(("arbitrary",)),
        name="ffn_ln",
    )(x, wg, wu, wd, ln_g, ln_b, *views)
    return out[0], [o.reshape(a.shape) for o, a in zip(out[1:], cast)]


_M_E0, _M_E1, _M_S0, _M_S1, _M_G0, _M_G1 = range(6)
_R_LSTART, _R_LEN, _R_GOFF = range(3)


def _router_kernel(x_ref, wrh_ref, wrl_ref, meta_ref, runs_ref, tot_ref, base_ref, *, tm):
    i = pl.program_id(0)

    @pl.when(i == 0)
    def _():
        base_ref[...] = jnp.zeros(base_ref.shape, F32)

    x = x_ref[...]
    xh = x.astype(BF16)
    xl = (x - xh.astype(F32)).astype(BF16)
    wrh = wrh_ref[...]
    logits = (jnp.dot(xh, wrh, preferred_element_type=F32)
              + jnp.dot(xl, wrh, preferred_element_type=F32)
              + jnp.dot(xh, wrl_ref[...], preferred_element_type=F32))
    lane = lax.broadcasted_iota(jnp.int32, (tm, LANES), 1)
    lanef = lane.astype(F32)
    neg = jnp.float32(-jnp.inf)
    lg = jnp.where(lane < N_EXPERTS, logits, neg)
    m0 = jnp.max(lg, axis=-1, keepdims=True)
    e0 = jnp.min(jnp.where(lg == m0, lanef, float(LANES)), axis=-1, keepdims=True)
    lg1 = jnp.where(lanef == e0, neg, lg)
    m1 = jnp.max(lg1, axis=-1, keepdims=True)
    e1 = jnp.min(jnp.where(lg1 == m1, lanef, float(LANES)), axis=-1, keepdims=True)
    t = jnp.exp(m1 - m0)
    g0 = 1.0 / (1.0 + t)
    g1 = t / (1.0 + t)

    oh0 = lanef == e0
    oh1 = lanef == e1
    cnt = jnp.where(oh0 | oh1, 1.0, 0.0)
    r_i = lax.broadcasted_iota(jnp.int32, (tm, tm), 0)
    c_i = lax.broadcasted_iota(jnp.int32, (tm, tm), 1)
    tri = jnp.where(c_i < r_i, 1.0, 0.0).astype(BF16)
    rank = jnp.dot(tri, cnt.astype(BF16), preferred_element_type=F32)

    count = jnp.broadcast_to(jnp.sum(cnt, axis=0, keepdims=True), (SUBLANES, LANES))
    run_len = jnp.floor((count + (RUN_ALIGN - 1.0)) * (1.0 / RUN_ALIGN)) * RUN_ALIGN
    lane8 = lax.broadcasted_iota(jnp.int32, (SUBLANES, LANES), 1)
    incl = run_len
    for sh in (1, 2, 4):
        incl = incl + jnp.where(lane8 >= sh, pltpu.roll(incl, sh, axis=1), 0.0)
    lstart = incl - run_len
    goff = base_ref[...]
    base_ref[...] = goff + run_len
    tot_ref[...] = goff + run_len
    row8 = lax.broadcasted_iota(jnp.int32, (SUBLANES, LANES), 0)
    runs_ref[...] = jnp.where(row8 == _R_LSTART, lstart,
                              jnp.where(row8 == _R_LEN, run_len, jnp.where(row8 == _R_GOFF, goff, 0.0)))

    slot = rank + lstart[0:1, :]
    s0 = jnp.sum(jnp.where(oh0, slot, 0.0), axis=-1, keepdims=True)
    s1 = jnp.sum(jnp.where(oh1, slot, 0.0), axis=-1, keepdims=True)
    meta = jnp.zeros((tm, LANES), F32)
    for k, val in ((_M_E0, e0), (_M_E1, e1), (_M_S0, s0), (_M_S1, s1), (_M_G0, g0), (_M_G1, g1)):
        meta = jnp.where(lane == k, val, meta)
    meta_ref[...] = meta


def _router(x, wr_hi, wr_lo):
    n, d = x.shape
    tm = TM_MOE
    assert n % tm == 0
    nt = n // tm
    small = (SUBLANES, LANES)
    return pl.pallas_call(
        functools.partial(_router_kernel, tm=tm),
        out_shape=(jax.ShapeDtypeStruct((n, LANES), F32), jax.ShapeDtypeStruct((nt * SUBLANES, LANES), F32),
                   jax.ShapeDtypeStruct(small, F32)),
        grid=(nt,),
        in_specs=[pl.BlockSpec((tm, d), lambda i: (i, 0)), _const_spec(wr_hi.shape), _const_spec(wr_lo.shape)],
        out_specs=(pl.BlockSpec((tm, LANES), lambda i: (i, 0)), pl.BlockSpec(small, lambda i: (i, 0)),
                   _const_spec(small)),
        scratch_shapes=[pltpu.VMEM(small, F32)],
        compiler_params=_params(("arbitrary",)),
        name="router",
    )(x, wr_hi, wr_lo)


def _run_copies(tab_ref, tile, local_ref, glob_hbm, sem, *, to_global, start):
    base = tile * (3 * N_EXPERTS)
    for e in range(N_EXPERTS):
        lstart = tab_ref[base + e]
        length = tab_ref[base + N_EXPERTS + e]
        gstart = tab_ref[base + 2 * N_EXPERTS + e]
        for b in range(RUN_BITS):
            size = RUN_ALIGN << b
            off = (length >> (b + 5)) << (b + 5)

            def piece(size=size, off=off, lstart=lstart, gstart=gstart):
                loc = local_ref.at[pl.ds(pl.multiple_of(lstart + off, RUN_ALIGN), size)]
                glo = glob_hbm.at[pl.ds(pl.multiple_of(gstart + off, RUN_ALIGN), size)]
                cp = pltpu.make_async_copy(loc, glo, sem) if to_global else pltpu.make_async_copy(glo, loc, sem)
                if start:
                    cp.start()
                else:
                    cp.wait()

            pl.when(((length >> (b + 4)) & 1) == 1)(piece)


def _fill_copies(fill_ref, zero_ref, glob_hbm, sem, *, te, n_spare, start):
    def go(cp):
        if start:
            cp.start()
        else:
            cp.wait()

    for e in range(N_EXPERTS):
        tstart = fill_ref[e]
        length = fill_ref[N_EXPERTS + e]
        for b in range(TAIL_BITS):
            size = RUN_ALIGN << b
            off = (length >> (b + 5)) << (b + 5)

            def piece(size=size, off=off, tstart=tstart):
                go(pltpu.make_async_copy(zero_ref.at[pl.ds(0, size)],
                                         glob_hbm.at[pl.ds(pl.multiple_of(tstart + off, RUN_ALIGN), size)], sem))

            pl.when(((length >> (b + 4)) & 1) == 1)(piece)
    spare0 = fill_ref[2 * N_EXPERTS]
    for j in range(n_spare):
        def tile_fill(j=j):
            go(pltpu.make_async_copy(zero_ref, glob_hbm.at[pl.ds(pl.multiple_of(spare0 + j * te, te), te)], sem))

        pl.when(j < fill_ref[2 * N_EXPERTS + 1])(tile_fill)


def _dispatch_kernel(tab_ref, fill_ref, x_ref, meta_ref, xs_hbm, loc_ref, zero_ref, sem, zsem, *,
                     tm, te, n_spare):
    i = pl.program_id(0)
    d = x_ref.shape[1]

    @pl.when(i == 0)
    def _():
        zero_ref[...] = jnp.zeros(zero_ref.shape, BF16)
        _fill_copies(fill_ref, zero_ref, xs_hbm, zsem, te=te, n_spare=n_spare, start=True)

    meta = meta_ref[...]
    meta_t = meta.T
    s0 = meta_t[_M_S0:_M_S0 + 1, :]
    s1 = meta_t[_M_S1:_M_S1 + 1, :]
    slot = lax.broadcasted_iota(jnp.int32, (SLOTS, tm), 0).astype(F32)
    p0 = slot == s0
    p1 = slot == s1
    onehot = jnp.where(p0 | p1, 1.0, 0.0).astype(BF16)
    loc_ref[:, 0:d] = jnp.dot(onehot, x_ref[...].astype(BF16), preferred_element_type=F32).astype(BF16)

    lane = lax.broadcasted_iota(jnp.int32, (tm, GATE_LANES), 1)

    def pieces(g):
        h = g.astype(BF16).astype(F32)
        r1 = g - h
        m = r1.astype(BF16).astype(F32)
        l = (r1 - m).astype(BF16).astype(F32)
        return jnp.where(lane == 0, h, jnp.where(lane == 1, m, jnp.where(lane == 2, l, 0.0))).astype(BF16)

    gate = (jnp.dot(jnp.where(p0, 1.0, 0.0).astype(BF16), pieces(meta[:, _M_G0:_M_G0 + 1]),
                    preferred_element_type=F32)
            + jnp.dot(jnp.where(p1, 1.0, 0.0).astype(BF16), pieces(meta[:, _M_G1:_M_G1 + 1]),
                      preferred_element_type=F32))
    loc_ref[:, d:d + GATE_LANES] = gate.astype(BF16)

    _run_copies(tab_ref, i, loc_ref, xs_hbm, sem, to_global=True, start=True)
    _run_copies(tab_ref, i, loc_ref, xs_hbm, sem, to_global=True, start=False)

    @pl.when(i == pl.num_programs(0) - 1)
    def _():
        _fill_copies(fill_ref, zero_ref, xs_hbm, zsem, te=te, n_spare=n_spare, start=False)


def _dispatch(x, meta, run_tab, fill_tab, n_rows, n_spare):
    n, d = x.shape
    tm, te = TM_MOE, TM_EXPERT
    width = d + GATE_LANES
    grid_spec = pltpu.PrefetchScalarGridSpec(
        num_scalar_prefetch=2,
        grid=(n // tm,),
        in_specs=[pl.BlockSpec((tm, d), lambda i, *_: (i, 0)),
                  pl.BlockSpec((tm, LANES), lambda i, *_: (i, 0))],
        out_specs=pl.BlockSpec(memory_space=pl.ANY),
        scratch_shapes=[pltpu.VMEM((SLOTS, width), BF16), pltpu.VMEM((te, width), BF16),
                        pltpu.SemaphoreType.DMA, pltpu.SemaphoreType.DMA],
    )
    return pl.pallas_call(
        functools.partial(_dispatch_kernel, tm=tm, te=te, n_spare=n_spare),
        out_shape=jax.ShapeDtypeStruct((n_rows, width), BF16),
        grid_spec=grid_spec,
        compiler_params=_params(("arbitrary",)),
        name="dispatch",
    )(run_tab, fill_tab, x, meta)


def _experts_kernel(tile_ref, expert_ref, rows_ref, xs_ref, wg_ref, wu_ref, wd_ref, ys_ref, acc_ref, *, sub):
    i = pl.program_id(0)
    f = pl.program_id(1)
    te, d = ys_ref.shape
    rows = rows_ref[i]

    @pl.when((rows == 0) & (f == 0))
    def _():
        ys_ref[...] = jnp.zeros(ys_ref.shape, ys_ref.dtype)

    def partial_out(r0):
        xb = xs_ref[r0:r0 + sub, 0:d]
        g = jnp.dot(xb, wg_ref[0], preferred_element_type=F32)
        u = jnp.dot(xb, wu_ref[0], preferred_element_type=F32)
        h = (_silu(g) * u).astype(BF16)
        return jnp.dot(h, wd_ref[0], preferred_element_type=F32)

    def first_chunk(n_rows):
        for r0 in range(0, n_rows, sub):
            acc_ref[r0:r0 + sub, :] = partial_out(r0)

    def last_chunk(n_rows):
        for r0 in range(0, n_rows, sub):
            gate = jnp.sum(xs_ref[r0:r0 + sub, d:d + GATE_LANES].astype(F32), axis=-1, keepdims=True)
            ys_ref[r0:r0 + sub, :] = ((acc_ref[r0:r0 + sub, :] + partial_out(r0)) * gate).astype(ys_ref.dtype)
        if n_rows < te:
            ys_ref[n_rows:te, :] = jnp.zeros((te - n_rows, d), ys_ref.dtype)

    for nb in range(1, te // sub + 1):
        fits = (rows > (nb - 1) * sub) & (rows <= nb * sub)
        pl.when(fits & (f == 0))(functools.partial(first_chunk, nb * sub))
        pl.when(fits & (f == 1))(functools.partial(last_chunk, nb * sub))


def _experts(xs, we_gate, we_up, we_down, tile_idx, tile_expert, tile_rows):
    n_rows, width = xs.shape
    d = width - GATE_LANES
    te = TM_EXPERT
    ff = we_gate.shape[-1]
    fc = ff // FF_CHUNKS
    assert n_rows % te == 0 and ff % FF_CHUNKS == 0 and fc % LANES == 0 and FF_CHUNKS == 2
    n_tiles = n_rows // te
    last = FF_CHUNKS - 1
    grid_spec = pltpu.PrefetchScalarGridSpec(
        num_scalar_prefetch=3,
        grid=(n_tiles, FF_CHUNKS),
        in_specs=[
            pl.BlockSpec((te, width), lambda i, f, tile, ex, rows: (tile[i], 0)),
            pl.BlockSpec((1, d, fc), lambda i, f, tile, ex, rows: (ex[i], 0, jnp.where(rows[i] > 0, f, last))),
            pl.BlockSpec((1, d, fc), lambda i, f, tile, ex, rows: (ex[i], 0, jnp.where(rows[i] > 0, f, last))),
            pl.BlockSpec((1, fc, d), lambda i, f, tile, ex, rows: (ex[i], jnp.where(rows[i] > 0, f, last), 0)),
        ],
        out_specs=pl.BlockSpec((te, d), lambda i, f, tile, ex, rows: (i, 0)),
        scratch_shapes=[pltpu.VMEM((te, d), F32)],
    )
    return pl.pallas_call(
        functools.partial(_experts_kernel, sub=SUB_EXPERT),
        out_shape=jax.ShapeDtypeStruct((n_rows, d), BF16),
        grid_spec=grid_spec,
        compiler_params=_params(("arbitrary", "arbitrary")),
        name="experts",
    )(tile_idx, tile_expert, tile_rows, xs, we_gate, we_up, we_down)


def _combine_kernel(tab_ref, x_ref, meta_ref, lng_ref, lnb_ref, ys_hbm, o_ref, loc_ref, sem, *, tm, alpha):
    i = pl.program_id(0)
    n_steps = pl.num_programs(0)
    slot = lax.rem(i, 2)

    @pl.when(i == 0)
    def _():
        loc_ref[...] = jnp.zeros(loc_ref.shape, loc_ref.dtype)
        _run_copies(tab_ref, 0, loc_ref.at[0], ys_hbm, sem.at[0], to_global=False, start=True)

    @pl.when(i + 1 < n_steps)
    def _():
        nxt = 1 - slot
        _run_copies(tab_ref, i + 1, loc_ref.at[nxt], ys_hbm, sem.at[nxt], to_global=False, start=True)

    _run_copies(tab_ref, i, loc_ref.at[slot], ys_hbm, sem.at[slot], to_global=False, start=False)

    meta = meta_ref[...]
    s0 = meta[:, _M_S0:_M_S0 + 1]
    s1 = meta[:, _M_S1:_M_S1 + 1]
    lane = lax.broadcasted_iota(jnp.int32, (tm, SLOTS), 1).astype(F32)
    pick = jnp.where((lane == s0) | (lane == s1), 1.0, 0.0).astype(BF16)
    moe = jnp.dot(pick, loc_ref[slot], preferred_element_type=F32)
    o_ref[...] = _layer_norm(alpha * x_ref[...] + moe, lng_ref[...], lnb_ref[...])


def _combine_ln(x, meta, run_tab, ys, ln_g, ln_b, *, alpha):
    n, d = x.shape
    tm = TM_MOE
    n_steps = n // tm
    row = pl.BlockSpec((tm, d), lambda i, *_: (i, 0))
    grid_spec = pltpu.PrefetchScalarGridSpec(
        num_scalar_prefetch=1,
        grid=(n_steps,),
        in_specs=[row, pl.BlockSpec((tm, LANES), lambda i, *_: (i, 0)),
                  pl.BlockSpec(ln_g.shape, lambda i, *_: (0, 0)), pl.BlockSpec(ln_b.shape, lambda i, *_: (0, 0)),
                  pl.BlockSpec(memory_space=pl.ANY)],
        out_specs=row,
        scratch_shapes=[pltpu.VMEM((2, SLOTS, d), BF16), pltpu.SemaphoreType.DMA((2,))],
    )
    return pl.pallas_call(
        functools.partial(_combine_kernel, tm=tm, alpha=alpha),
        out_shape=jax.ShapeDtypeStruct((n, d), F32),
        grid_spec=grid_spec,
        compiler_params=_params(("arbitrary",)),
        name="combine_ln",
    )(run_tab, x, meta, ln_g, ln_b, ys)


def _moe_ln(x, w_router, we_gate, we_up, we_down, ln_g, ln_b, *, alpha):
    n, d = x.shape
    tm, te = TM_MOE, TM_EXPERT
    assert n % tm == 0 and te % SUB_EXPERT == 0
    nt = n // tm
    wr = jnp.pad(w_router, ((0, 0), (0, LANES - N_EXPERTS)))
    wr_hi = wr.astype(BF16)
    wr_lo = (wr - wr_hi.astype(F32)).astype(BF16)
    meta, runs, totals = _router(x, wr_hi, wr_lo)

    padded = totals[0, :N_EXPERTS].astype(jnp.int32)
    tiles_e = (padded + te - 1) // te
    tile_end = jnp.cumsum(tiles_e)
    region = (tile_end - tiles_e) * te
    n_used = tile_end[-1]
    max_rows = TOP_K * n + nt * N_EXPERTS * (RUN_ALIGN - 1) + N_EXPERTS * (te - RUN_ALIGN)
    n_tiles = -(-max_rows // te)
    n_spare = n_tiles - (TOP_K * n) // te
    runs_i = runs.reshape(nt, SUBLANES, LANES)[:, :3, :N_EXPERTS].astype(jnp.int32)
    runs_i = runs_i.at[:, _R_GOFF, :].add(region)
    run_tab = runs_i.reshape(-1)
    fill_tab = jnp.concatenate([region + padded, tiles_e * te - padded,
                                (n_used * te)[None], (n_tiles - n_used)[None]]).astype(jnp.int32)
    tiles = jnp.arange(n_tiles, dtype=jnp.int32)
    j = jnp.minimum(tiles, n_used - 1)
    tile_expert = jnp.minimum(jnp.sum(j[:, None] >= tile_end[None, :], axis=-1), N_EXPERTS - 1).astype(jnp.int32)
    region_end = jnp.sum(jnp.where(tile_expert[:, None] == jnp.arange(N_EXPERTS), region + padded, 0), axis=-1)
    tile_rows = jnp.where(tiles < n_used, jnp.minimum(region_end - tiles * te, te), 0).astype(jnp.int32)

    xs = _dispatch(x, meta, run_tab, fill_tab, n_tiles * te, n_spare)
    ys = _experts(xs, we_gate, we_up, we_down, j, tile_expert, tile_rows)
    return _combine_ln(x, meta, run_tab, ys, ln_g, ln_b, alpha=alpha)


def kernel(x, w_in, conv_w, v_g, v_b, w_s, b_s, out_g, w_out, ln1_g, ln1_b, ln2_g, ln2_b,
           w_gate, w_up, w_down, w_router, we_gate, we_up, we_down):
    bsz, seq, d = x.shape
    depth = w_in.shape[0]
    alpha = float((2 * depth) ** 0.25)
    h = x.reshape(bsz * seq, d)
    row = lambda a: a.reshape(1, -1)
    mix_w = [w_in[0].astype(BF16), w_out[0].astype(BF16)]
    ffn_w = None
    for i in range(depth):
        j = i // 2
        dense = i % 2 == 0
        bias_t = jnp.repeat(b_s[i].T, GMLP_HEAD_DIM, axis=1)
        h, ffn_w = _mixer_ln(h, mix_w[0], conv_w[i], row(v_g[i]), row(v_b[i]), w_s[i], bias_t,
                             row(out_g[i]), mix_w[1], row(ln1_g[i]), row(ln1_b[i]),
                             cast=(w_gate[j], w_up[j], w_down[j]) if dense else (), seq=seq, alpha=alpha)
        last = i + 1 == depth
        if dense:
            ahead = () if last else (w_in[i + 1], w_out[i + 1], we_gate[j], we_up[j], we_down[j])
            h, ahead_w = _ffn_ln(h, *ffn_w, row(ln2_g[i]), row(ln2_b[i]), cast=ahead, alpha=alpha)
            if not last:
                mix_w, moe_w = ahead_w[:2], ahead_w[2:]
        else:
            h = _moe_ln(h, w_router[j], *moe_w, row(ln2_g[i]), row(ln2_b[i]), alpha=alpha)
            if not last:
                mix_w = [w_in[i + 1].astype(BF16), w_out[i + 1].astype(BF16)]
    return h.reshape(bsz, seq, d)
```

```python
import functools

import jax
import jax.numpy as jnp
from jax import lax
from jax.experimental import pallas as pl
from jax.experimental.pallas import tpu as pltpu

F32 = jnp.float32
BF16 = jnp.bfloat16

CONV_WIDTH = 512
CONV_HEAD_DIM = 64
GMLP_WIDTH = 512
GMLP_HEADS = 4
GMLP_HEAD_DIM = 128
CHUNK = 128
N_EXPERTS = 8
TOP_K = 2
LN_EPS = 1e-5
RMS_EPS = 1e-6

LANES = 128
SUBLANES = 8
BF16_ROWS = 16
VMEM_LIMIT_BYTES = 56 * 1024 * 1024

TM_MIX = 1024
SUB_MIX = 512
TM_FFN = 512
SUB_FFN = 512
TM_MOE = 512
TM_EXPERT = 1024
SUB_EXPERT = 512
FF_CHUNKS = 2

RUN_ALIGN = BF16_ROWS
ALIGN_SHIFT = RUN_ALIGN.bit_length() - 1
MAX_RUN = TM_MOE + RUN_ALIGN
RUN_BITS = (MAX_RUN // RUN_ALIGN).bit_length()
SLOTS = -(-(TOP_K * TM_MOE + N_EXPERTS * 2 * (RUN_ALIGN - 1)) // LANES) * LANES
GATE_LANES = LANES
TAIL_BITS = (TM_EXPERT // RUN_ALIGN - 1).bit_length()


def _layer_norm(r, g, b):
    mu = jnp.mean(r, axis=-1, keepdims=True)
    d = r - mu
    var = jnp.mean(d * d, axis=-1, keepdims=True)
    return d * lax.rsqrt(var + LN_EPS) * g + b


def _gelu(x):
    return 0.5 * x * (1.0 + lax.erf(x * (2.0 ** -0.5)))


def _silu(x):
    return x * (1.0 / (1.0 + jnp.exp(-x)))


def _params(semantics):
    return pltpu.CompilerParams(dimension_semantics=semantics, vmem_limit_bytes=VMEM_LIMIT_BYTES)


def _const_spec(shape):
    nd = len(shape)
    return pl.BlockSpec(shape, lambda *_: (0,) * nd)


def _cast_plan(arrays, n_steps):
    views, specs, shapes = [], [], []
    for a in arrays:
        v = a.reshape(-1, a.shape[-1])
        rows = v.shape[0]
        n_blocks = next(nb for nb in range(n_steps, 0, -1)
                        if n_steps % nb == 0 and rows % (nb * BF16_ROWS) == 0)
        steps_per_block = n_steps // n_blocks
        views.append(v)
        specs.append(pl.BlockSpec((rows // n_blocks, v.shape[1]), lambda i, r=steps_per_block: (i // r, 0)))
        shapes.append(jax.ShapeDtypeStruct(v.shape, BF16))
    return views, specs, shapes


def _cast_along(src_refs, dst_refs):
    for src, dst in zip(src_refs, dst_refs):
        dst[...] = src[...].astype(dst.dtype)


def _mixer_kernel(*refs, n_cast, tm, sub, seq, alpha):
    (x_ref, win_ref, convw_ref, vg_ref, vb_ref, ws_ref, bias_ref, outg_ref, wout_ref,
     lng_ref, lnb_ref) = refs[:11]
    o_ref, ch_ref = refs[11 + n_cast], refs[-1]
    _cast_along(refs[11:11 + n_cast], refs[12 + n_cast:12 + 2 * n_cast])
    i = pl.program_id(0)
    cw_, gw_ = CONV_WIDTH, GMLP_WIDTH

    @pl.when(lax.rem(i * tm, seq) == 0)
    def _():
        ch_ref[0:SUBLANES, :] = jnp.zeros((SUBLANES, cw_), F32)

    trow = lax.broadcasted_iota(jnp.int32, (CHUNK, CHUNK), 0)
    tcol = lax.broadcasted_iota(jnp.int32, (CHUNK, CHUNK), 1)
    causal = tcol <= trow
    ws = [jnp.where(causal, ws_ref[h], 0.0).astype(BF16) for h in range(GMLP_HEADS)]
    lane = lax.broadcasted_iota(jnp.int32, (sub, LANES), 1)
    low_half = lane < CONV_HEAD_DIM
    nch = sub // CHUNK

    for r0 in range(0, tm, sub):
        x = x_ref[r0:r0 + sub, :]
        proj = jnp.dot(x.astype(BF16), win_ref[...], preferred_element_type=F32)
        b_gate = proj[:, 0:cw_]
        c_gate = proj[:, cw_:2 * cw_]
        hh = proj[:, 2 * cw_:3 * cw_]
        u = proj[:, 3 * cw_:3 * cw_ + gw_]
        v = proj[:, 3 * cw_ + gw_:3 * cw_ + 2 * gw_]

        c0 = SUBLANES + r0
        ch_ref[c0:c0 + sub, :] = c_gate * hh
        convw = convw_ref[...]
        conv = (convw[0:1, :] * ch_ref[c0 - 2:c0 - 2 + sub, :]
                + convw[1:2, :] * ch_ref[c0 - 1:c0 - 1 + sub, :]
                + convw[2:3, :] * ch_ref[c0:c0 + sub, :])
        y_conv = b_gate * conv

        ug = _gelu(u)
        vn = _layer_norm(_gelu(v), vg_ref[...], vb_ref[...]).astype(BF16)
        zs = []
        for h in range(GMLP_HEADS):
            lo, hi = h * GMLP_HEAD_DIM, (h + 1) * GMLP_HEAD_DIM
            rhs = jnp.concatenate([vn[c * CHUNK:(c + 1) * CHUNK, lo:hi] for c in range(nch)], axis=1)
            zs.append(jnp.dot(ws[h], rhs, preferred_element_type=F32))
        z = jnp.concatenate(
            [jnp.concatenate([zs[h][:, c * GMLP_HEAD_DIM:(c + 1) * GMLP_HEAD_DIM]
                              for h in range(GMLP_HEADS)], axis=1) for c in range(nch)], axis=0)
        bias = jnp.concatenate([bias_ref[...]] * nch, axis=0)
        y_sg = ug * (z + bias)

        parts = []
        for j in range(cw_ // LANES):
            yt = y_conv[:, j * LANES:(j + 1) * LANES]
            sq = yt * yt
            ms_lo = jnp.sum(jnp.where(low_half, sq, 0.0), axis=-1, keepdims=True) * (1.0 / CONV_HEAD_DIM)
            ms_hi = jnp.sum(jnp.where(low_half, 0.0, sq), axis=-1, keepdims=True) * (1.0 / CONV_HEAD_DIM)
            parts.append(yt * jnp.where(low_half, lax.rsqrt(ms_lo + RMS_EPS), lax.rsqrt(ms_hi + RMS_EPS)))
        for h in range(GMLP_HEADS):
            yt = y_sg[:, h * GMLP_HEAD_DIM:(h + 1) * GMLP_HEAD_DIM]
            ms = jnp.mean(yt * yt, axis=-1, keepdims=True)
            parts.append(yt * lax.rsqrt(ms + RMS_EPS))
        y = jnp.concatenate(parts, axis=1) * outg_ref[...]

        mix = jnp.dot(y.astype(BF16), wout_ref[...], preferred_element_type=F32)
        o_ref[r0:r0 + sub, :] = _layer_norm(alpha * x + mix, lng_ref[...], lnb_ref[...])

    ch_ref[0:SUBLANES, :] = ch_ref[tm:tm + SUBLANES, :]


def _mixer_ln(x, w_in, conv_w, v_g, v_b, w_s, bias_t, out_g, w_out, ln_g, ln_b, cast=(), *, seq, alpha):
    n, d = x.shape
    tm, sub = TM_MIX, SUB_MIX
    assert n % tm == 0 and seq % tm == 0 and tm % sub == 0 and sub % CHUNK == 0
    views, cast_specs, cast_shapes = _cast_plan(cast, n // tm)
    kern = functools.partial(_mixer_kernel, n_cast=len(cast), tm=tm, sub=sub, seq=seq, alpha=alpha)
    row = pl.BlockSpec((tm, d), lambda i: (i, 0))
    out = pl.pallas_call(
        kern,
        out_shape=[jax.ShapeDtypeStruct((n, d), F32)] + cast_shapes,
        grid=(n // tm,),
        in_specs=[row, _const_spec(w_in.shape), _const_spec(conv_w.shape), _const_spec(v_g.shape),
                  _const_spec(v_b.shape), _const_spec(w_s.shape), _const_spec(bias_t.shape),
                  _const_spec(out_g.shape), _const_spec(w_out.shape), _const_spec(ln_g.shape),
                  _const_spec(ln_b.shape)] + cast_specs,
        out_specs=[row] + cast_specs,
        scratch_shapes=[pltpu.VMEM((tm + SUBLANES, CONV_WIDTH), F32)],
        compiler_params=_params(("arbitrary",)),
        name="mixer_ln",
    )(x, w_in, conv_w, v_g, v_b, w_s, bias_t, out_g, w_out, ln_g, ln_b, *views)
    return out[0], [o.reshape(a.shape) for o, a in zip(out[1:], cast)]


def _ffn_kernel(*refs, n_cast, sub, alpha):
    x_ref, wg_ref, wu_ref, wd_ref, lng_ref, lnb_ref = refs[:6]
    o_ref = refs[6 + n_cast]
    _cast_along(refs[6:6 + n_cast], refs[7 + n_cast:7 + 2 * n_cast])
    for r0 in range(0, x_ref.shape[0], sub):
        x = x_ref[r0:r0 + sub, :]
        xb = x.astype(BF16)
        g = jnp.dot(xb, wg_ref[...], preferred_element_type=F32)
        u = jnp.dot(xb, wu_ref[...], preferred_element_type=F32)
        h = (_silu(g) * u).astype(BF16)
        ffn = jnp.dot(h, wd_ref[...], preferred_element_type=F32)
        o_ref[r0:r0 + sub, :] = _layer_norm(alpha * x + ffn, lng_ref[...], lnb_ref[...])


def _ffn_ln(x, wg, wu, wd, ln_g, ln_b, cast=(), *, alpha):
    n, d = x.shape
    tm = TM_FFN
    assert n % tm == 0 and tm % SUB_FFN == 0
    views, cast_specs, cast_shapes = _cast_plan(cast, n // tm)
    row = pl.BlockSpec((tm, d), lambda i: (i, 0))
    out = pl.pallas_call(
        functools.partial(_ffn_kernel, n_cast=len(cast), sub=SUB_FFN, alpha=alpha),
        out_shape=[jax.ShapeDtypeStruct((n, d), F32)] + cast_shapes,
        grid=(n // tm,),
        in_specs=[row, _const_spec(wg.shape), _const_spec(wu.shape), _const_spec(wd.shape),
                  _const_spec(ln_g.shape), _const_spec(ln_b.shape)] + cast_specs,
        out_specs=[row] + cast_specs,
        compiler_params=_params(("arbitrary",)),
        name="ffn_ln",
    )(x, wg, wu, wd, ln_g, ln_b, *views)
    return out[0], [o.reshape(a.shape) for o, a in zip(out[1:], cast)]


_M_E0, _M_E1, _M_S0, _M_S1, _M_G0, _M_G1 = range(6)
_R_LSTART, _R_LEN, _R_GOFF, _R_PARTIAL = range(4)
RUN_FIELDS = 4


def _router_kernel(x_ref, wrh_ref, wrl_ref, meta_ref, runs_ref, tot_ref, base_ref, *, tm):
    i = pl.program_id(0)

    @pl.when(i == 0)
    def _():
        base_ref[...] = jnp.zeros(base_ref.shape, F32)

    x = x_ref[...]
    xh = x.astype(BF16)
    xl = (x - xh.astype(F32)).astype(BF16)
    wrh = wrh_ref[...]
    logits = (jnp.dot(xh, wrh, preferred_element_type=F32)
              + jnp.dot(xl, wrh, preferred_element_type=F32)
              + jnp.dot(xh, wrl_ref[...], preferred_element_type=F32))
    lane = lax.broadcasted_iota(jnp.int32, (tm, LANES), 1)
    lanef = lane.astype(F32)
    neg = jnp.float32(-jnp.inf)
    lg = jnp.where(lane < N_EXPERTS, logits, neg)
    m0 = jnp.max(lg, axis=-1, keepdims=True)
    e0 = jnp.min(jnp.where(lg == m0, lanef, float(LANES)), axis=-1, keepdims=True)
    lg1 = jnp.where(lanef == e0, neg, lg)
    m1 = jnp.max(lg1, axis=-1, keepdims=True)
    e1 = jnp.min(jnp.where(lg1 == m1, lanef, float(LANES)), axis=-1, keepdims=True)
    t = jnp.exp(m1 - m0)
    g0 = 1.0 / (1.0 + t)
    g1 = t / (1.0 + t)

    oh0 = lanef == e0
    oh1 = lanef == e1
    cnt = jnp.where(oh0 | oh1, 1.0, 0.0)
    r_i = lax.broadcasted_iota(jnp.int32, (tm, tm), 0)
    c_i = lax.broadcasted_iota(jnp.int32, (tm, tm), 1)
    tri = jnp.where(c_i < r_i, 1.0, 0.0).astype(BF16)
    rank = jnp.dot(tri, cnt.astype(BF16), preferred_element_type=F32)

    def align_down(v):
        return jnp.floor(v * (1.0 / RUN_ALIGN)) * RUN_ALIGN

    count = jnp.broadcast_to(jnp.sum(cnt, axis=0, keepdims=True), (SUBLANES, LANES))
    before = base_ref[...]
    phase = before - align_down(before)
    end = phase + count
    run_len = jnp.where(count > 0, align_down(end + (RUN_ALIGN - 1.0)), 0.0)
    partial = jnp.where((count > 0) & (end != align_down(end)), 1.0, 0.0)
    lane8 = lax.broadcasted_iota(jnp.int32, (SUBLANES, LANES), 1)
    incl = run_len
    for sh in (1, 2, 4):
        incl = incl + jnp.where(lane8 >= sh, pltpu.roll(incl, sh, axis=1), 0.0)
    lstart = incl - run_len
    base_ref[...] = before + count
    tot_ref[...] = before + count
    row8 = lax.broadcasted_iota(jnp.int32, (SUBLANES, LANES), 0)
    table = jnp.zeros((SUBLANES, LANES), F32)
    for r, val in ((_R_LSTART, lstart), (_R_LEN, run_len), (_R_GOFF, before - phase), (_R_PARTIAL, partial)):
        table = jnp.where(row8 == r, val, table)
    runs_ref[...] = table

    slot = rank + (lstart + phase)[0:1, :]
    s0 = jnp.sum(jnp.where(oh0, slot, 0.0), axis=-1, keepdims=True)
    s1 = jnp.sum(jnp.where(oh1, slot, 0.0), axis=-1, keepdims=True)
    meta = jnp.zeros((tm, LANES), F32)
    for k, val in ((_M_E0, e0), (_M_E1, e1), (_M_S0, s0), (_M_S1, s1), (_M_G0, g0), (_M_G1, g1)):
        meta = jnp.where(lane == k, val, meta)
    meta_ref[...] = meta


def _router(x, wr_hi, wr_lo):
    n, d = x.shape
    tm = TM_MOE
    assert n % tm == 0
    nt = n // tm
    small = (SUBLANES, LANES)
    return pl.pallas_call(
        functools.partial(_router_kernel, tm=tm),
        out_shape=(jax.ShapeDtypeStruct((n, LANES), F32), jax.ShapeDtypeStruct((nt * SUBLANES, LANES), F32),
                   jax.ShapeDtypeStruct(small, F32)),
        grid=(nt,),
        in_specs=[pl.BlockSpec((tm, d), lambda i: (i, 0)), _const_spec(wr_hi.shape), _const_spec(wr_lo.shape)],
        out_specs=(pl.BlockSpec((tm, LANES), lambda i: (i, 0)), pl.BlockSpec(small, lambda i: (i, 0)),
                   _const_spec(small)),
        scratch_shapes=[pltpu.VMEM(small, F32)],
        compiler_params=_params(("arbitrary",)),
        name="router",
    )(x, wr_hi, wr_lo)


def _run_copies(tab_ref, tile, local_ref, glob_hbm, sem, *, to_global, start):
    base = tile * (RUN_FIELDS * N_EXPERTS)
    for e in range(N_EXPERTS):
        lstart = tab_ref[base + e]
        length = tab_ref[base + N_EXPERTS + e]
        gstart = tab_ref[base + 2 * N_EXPERTS + e]
        for b in range(RUN_BITS):
            size = RUN_ALIGN << b
            off = (length >> (b + ALIGN_SHIFT + 1)) << (b + ALIGN_SHIFT + 1)

            def piece(size=size, off=off, lstart=lstart, gstart=gstart):
                loc = local_ref.at[pl.ds(pl.multiple_of(lstart + off, RUN_ALIGN), size)]
                glo = glob_hbm.at[pl.ds(pl.multiple_of(gstart + off, RUN_ALIGN), size)]
                cp = pltpu.make_async_copy(loc, glo, sem) if to_global else pltpu.make_async_copy(glo, loc, sem)
                if start:
                    cp.start()
                else:
                    cp.wait()

            pl.when(((length >> (b + ALIGN_SHIFT)) & 1) == 1)(piece)


def _fill_copies(fill_ref, zero_ref, glob_hbm, sem, *, te, n_spare, start):
    def go(cp):
        if start:
            cp.start()
        else:
            cp.wait()

    for e in range(N_EXPERTS):
        tstart = fill_ref[e]
        length = fill_ref[N_EXPERTS + e]
        for b in range(TAIL_BITS):
            size = RUN_ALIGN << b
            off = (length >> (b + ALIGN_SHIFT + 1)) << (b + ALIGN_SHIFT + 1)

            def piece(size=size, off=off, tstart=tstart):
                go(pltpu.make_async_copy(zero_ref.at[pl.ds(0, size)],
                                         glob_hbm.at[pl.ds(pl.multiple_of(tstart + off, RUN_ALIGN), size)], sem))

            pl.when(((length >> (b + ALIGN_SHIFT)) & 1) == 1)(piece)
    spare0 = fill_ref[2 * N_EXPERTS]
    for j in range(n_spare):
        def tile_fill(j=j):
            go(pltpu.make_async_copy(zero_ref, glob_hbm.at[pl.ds(pl.multiple_of(spare0 + j * te, te), te)], sem))

        pl.when(j < fill_ref[2 * N_EXPERTS + 1])(tile_fill)


def _dispatch_kernel(tab_ref, fill_ref, x_ref, meta_ref, xs_hbm, loc_ref, zero_ref, carry_ref, sem, zsem, *,
                     tm, te, n_spare):
    i = pl.program_id(0)
    d = x_ref.shape[1]

    @pl.when(i == 0)
    def _():
        zero_ref[...] = jnp.zeros(zero_ref.shape, BF16)
        carry_ref[...] = jnp.zeros(carry_ref.shape, BF16)
        _fill_copies(fill_ref, zero_ref, xs_hbm, zsem, te=te, n_spare=n_spare, start=True)

    meta = meta_ref[...]
    meta_t = meta.T
    s0 = meta_t[_M_S0:_M_S0 + 1, :]
    s1 = meta_t[_M_S1:_M_S1 + 1, :]
    slot = lax.broadcasted_iota(jnp.int32, (SLOTS, tm), 0).astype(F32)
    p0 = slot == s0
    p1 = slot == s1
    onehot = jnp.where(p0 | p1, 1.0, 0.0).astype(BF16)
    loc_ref[:, 0:d] = jnp.dot(onehot, x_ref[...].astype(BF16), preferred_element_type=F32).astype(BF16)

    lane = lax.broadcasted_iota(jnp.int32, (tm, GATE_LANES), 1)

    def pieces(g):
        h = g.astype(BF16).astype(F32)
        r1 = g - h
        m = r1.astype(BF16).astype(F32)
        l = (r1 - m).astype(BF16).astype(F32)
        return jnp.where(lane == 0, h, jnp.where(lane == 1, m, jnp.where(lane == 2, l, 0.0))).astype(BF16)

    gate = (jnp.dot(jnp.where(p0, 1.0, 0.0).astype(BF16), pieces(meta[:, _M_G0:_M_G0 + 1]),
                    preferred_element_type=F32)
            + jnp.dot(jnp.where(p1, 1.0, 0.0).astype(BF16), pieces(meta[:, _M_G1:_M_G1 + 1]),
                      preferred_element_type=F32))
    loc_ref[:, d:d + GATE_LANES] = gate.astype(BF16)

    base = i * (RUN_FIELDS * N_EXPERTS)
    for e in range(N_EXPERTS):
        lstart = tab_ref[base + _R_LSTART * N_EXPERTS + e]
        length = tab_ref[base + _R_LEN * N_EXPERTS + e]
        partial = tab_ref[base + _R_PARTIAL * N_EXPERTS + e]

        def merge(e=e, lstart=lstart, length=length, partial=partial):
            first = pl.ds(pl.multiple_of(lstart, RUN_ALIGN), RUN_ALIGN)
            loc_ref[first, :] = loc_ref[first, :] + carry_ref[e]
            last = pl.ds(pl.multiple_of(lstart + length - RUN_ALIGN, RUN_ALIGN), RUN_ALIGN)

            @pl.when(partial == 1)
            def _():
                carry_ref[e] = loc_ref[last, :]

            @pl.when(partial == 0)
            def _():
                carry_ref[e] = jnp.zeros(carry_ref.shape[1:], BF16)

        pl.when(length > 0)(merge)

    _run_copies(tab_ref, i, loc_ref, xs_hbm, sem, to_global=True, start=True)
    _run_copies(tab_ref, i, loc_ref, xs_hbm, sem, to_global=True, start=False)

    @pl.when(i == pl.num_programs(0) - 1)
    def _():
        _fill_copies(fill_ref, zero_ref, xs_hbm, zsem, te=te, n_spare=n_spare, start=False)


def _dispatch(x, meta, run_tab, fill_tab, n_rows, n_spare):
    n, d = x.shape
    tm, te = TM_MOE, TM_EXPERT
    width = d + GATE_LANES
    grid_spec = pltpu.PrefetchScalarGridSpec(
        num_scalar_prefetch=2,
        grid=(n // tm,),
        in_specs=[pl.BlockSpec((tm, d), lambda i, *_: (i, 0)),
                  pl.BlockSpec((tm, LANES), lambda i, *_: (i, 0))],
        out_specs=pl.BlockSpec(memory_space=pl.ANY),
        scratch_shapes=[pltpu.VMEM((SLOTS, width), BF16), pltpu.VMEM((te, width), BF16),
                        pltpu.VMEM((N_EXPERTS, RUN_ALIGN, width), BF16),
                        pltpu.SemaphoreType.DMA, pltpu.SemaphoreType.DMA],
    )
    return pl.pallas_call(
        functools.partial(_dispatch_kernel, tm=tm, te=te, n_spare=n_spare),
        out_shape=jax.ShapeDtypeStruct((n_rows, width), BF16),
        grid_spec=grid_spec,
        compiler_params=_params(("arbitrary",)),
        name="dispatch",
    )(run_tab, fill_tab, x, meta)


def _experts_kernel(tile_ref, expert_ref, rows_ref, xs_ref, wg_ref, wu_ref, wd_ref, ys_ref, acc_ref, *, sub):
    i = pl.program_id(0)
    f = pl.program_id(1)
    te, d = ys_ref.shape
    rows = rows_ref[i]

    @pl.when((rows == 0) & (f == 0))
    def _():
        ys_ref[...] = jnp.zeros(ys_ref.shape, ys_ref.dtype)

    def partial_out(r0):
        xb = xs_ref[r0:r0 + sub, 0:d]
        g = jnp.dot(xb, wg_ref[0], preferred_element_type=F32)
        u = jnp.dot(xb, wu_ref[0], preferred_element_type=F32)
        h = (_silu(g) * u).astype(BF16)
        return jnp.dot(h, wd_ref[0], preferred_element_type=F32)

    def first_chunk(n_rows):
        for r0 in range(0, n_rows, sub):
            acc_ref[r0:r0 + sub, :] = partial_out(r0)

    def last_chunk(n_rows):
        for r0 in range(0, n_rows, sub):
            gate = jnp.sum(xs_ref[r0:r0 + sub, d:d + GATE_LANES].astype(F32), axis=-1, keepdims=True)
            ys_ref[r0:r0 + sub, :] = ((acc_ref[r0:r0 + sub, :] + partial_out(r0)) * gate).astype(ys_ref.dtype)
        if n_rows < te:
            ys_ref[n_rows:te, :] = jnp.zeros((te - n_rows, d), ys_ref.dtype)

    for nb in range(1, te // sub + 1):
        fits = (rows > (nb - 1) * sub) & (rows <= nb * sub)
        pl.when(fits & (f == 0))(functools.partial(first_chunk, nb * sub))
        pl.when(fits & (f == 1))(functools.partial(last_chunk, nb * sub))


def _experts(xs, we_gate, we_up, we_down, tile_idx, tile_expert, tile_rows):
    n_rows, width = xs.shape
    d = width - GATE_LANES
    te = TM_EXPERT
    ff = we_gate.shape[-1]
    fc = ff // FF_CHUNKS
    assert n_rows % te == 0 and ff % FF_CHUNKS == 0 and fc % LANES == 0 and FF_CHUNKS == 2
    n_tiles = n_rows // te
    last = FF_CHUNKS - 1
    grid_spec = pltpu.PrefetchScalarGridSpec(
        num_scalar_prefetch=3,
        grid=(n_tiles, FF_CHUNKS),
        in_specs=[
            pl.BlockSpec((te, width), lambda i, f, tile, ex, rows: (tile[i], 0)),
            pl.BlockSpec((1, d, fc), lambda i, f, tile, ex, rows: (ex[i], 0, jnp.where(rows[i] > 0, f, last))),
            pl.BlockSpec((1, d, fc), lambda i, f, tile, ex, rows: (ex[i], 0, jnp.where(rows[i] > 0, f, last))),
            pl.BlockSpec((1, fc, d), lambda i, f, tile, ex, rows: (ex[i], jnp.where(rows[i] > 0, f, last), 0)),
        ],
        out_specs=pl.BlockSpec((te, d), lambda i, f, tile, ex, rows: (i, 0)),
        scratch_shapes=[pltpu.VMEM((te, d), F32)],
    )
    return pl.pallas_call(
        functools.partial(_experts_kernel, sub=SUB_EXPERT),
        out_shape=jax.ShapeDtypeStruct((n_rows, d), BF16),
        grid_spec=grid_spec,
        compiler_params=_params(("arbitrary", "arbitrary")),
        name="experts",
    )(tile_idx, tile_expert, tile_rows, xs, we_gate, we_up, we_down)


def _combine_kernel(tab_ref, x_ref, meta_ref, lng_ref, lnb_ref, ys_hbm, o_ref, loc_ref, sem, *, tm, alpha):
    i = pl.program_id(0)
    n_steps = pl.num_programs(0)
    slot = lax.rem(i, 2)

    @pl.when(i == 0)
    def _():
        loc_ref[...] = jnp.zeros(loc_ref.shape, loc_ref.dtype)
        _run_copies(tab_ref, 0, loc_ref.at[0], ys_hbm, sem.at[0], to_global=False, start=True)

    @pl.when(i + 1 < n_steps)
    def _():
        nxt = 1 - slot
        _run_copies(tab_ref, i + 1, loc_ref.at[nxt], ys_hbm, sem.at[nxt], to_global=False, start=True)

    _run_copies(tab_ref, i, loc_ref.at[slot], ys_hbm, sem.at[slot], to_global=False, start=False)

    meta = meta_ref[...]
    s0 = meta[:, _M_S0:_M_S0 + 1]
    s1 = meta[:, _M_S1:_M_S1 + 1]
    lane = lax.broadcasted_iota(jnp.int32, (tm, SLOTS), 1).astype(F32)
    pick = jnp.where((lane == s0) | (lane == s1), 1.0, 0.0).astype(BF16)
    moe = jnp.dot(pick, loc_ref[slot], preferred_element_type=F32)
    o_ref[...] = _layer_norm(alpha * x_ref[...] + moe, lng_ref[...], lnb_ref[...])


def _combine_ln(x, meta, run_tab, ys, ln_g, ln_b, *, alpha):
    n, d = x.shape
    tm = TM_MOE
    n_steps = n // tm
    row = pl.BlockSpec((tm, d), lambda i, *_: (i, 0))
    grid_spec = pltpu.PrefetchScalarGridSpec(
        num_scalar_prefetch=1,
        grid=(n_steps,),
        in_specs=[row, pl.BlockSpec((tm, LANES), lambda i, *_: (i, 0)),
                  pl.BlockSpec(ln_g.shape, lambda i, *_: (0, 0)), pl.BlockSpec(ln_b.shape, lambda i, *_: (0, 0)),
                  pl.BlockSpec(memory_space=pl.ANY)],
        out_specs=row,
        scratch_shapes=[pltpu.VMEM((2, SLOTS, d), BF16), pltpu.SemaphoreType.DMA((2,))],
    )
    return pl.pallas_call(
        functools.partial(_combine_kernel, tm=tm, alpha=alpha),
        out_shape=jax.ShapeDtypeStruct((n, d), F32),
        grid_spec=grid_spec,
        compiler_params=_params(("arbitrary",)),
        name="combine_ln",
    )(run_tab, x, meta, ln_g, ln_b, ys)


def _moe_ln(x, w_router, we_gate, we_up, we_down, ln_g, ln_b, *, alpha):
    n, d = x.shape
    tm, te = TM_MOE, TM_EXPERT
    assert n % tm == 0 and te % SUB_EXPERT == 0
    nt = n // tm
    wr = jnp.pad(w_router, ((0, 0), (0, LANES - N_EXPERTS)))
    wr_hi = wr.astype(BF16)
    wr_lo = (wr - wr_hi.astype(F32)).astype(BF16)
    meta, runs, totals = _router(x, wr_hi, wr_lo)

    counts = totals[0, :N_EXPERTS].astype(jnp.int32)
    padded = (counts + RUN_ALIGN - 1) // RUN_ALIGN * RUN_ALIGN
    tiles_e = (padded + te - 1) // te
    tile_end = jnp.cumsum(tiles_e)
    region = (tile_end - tiles_e) * te
    n_used = tile_end[-1]
    max_rows = TOP_K * n + N_EXPERTS * (te - 1)
    n_tiles = -(-max_rows // te)
    n_spare = n_tiles - (TOP_K * n) // te
    runs_i = runs.reshape(nt, SUBLANES, LANES)[:, :RUN_FIELDS, :N_EXPERTS].astype(jnp.int32)
    runs_i = runs_i.at[:, _R_GOFF, :].add(region)
    run_tab = runs_i.reshape(-1)
    fill_tab = jnp.concatenate([region + padded, tiles_e * te - padded,
                                (n_used * te)[None], (n_tiles - n_used)[None]]).astype(jnp.int32)
    tiles = jnp.arange(n_tiles, dtype=jnp.int32)
    j = jnp.minimum(tiles, n_used - 1)
    tile_expert = jnp.minimum(jnp.sum(j[:, None] >= tile_end[None, :], axis=-1), N_EXPERTS - 1).astype(jnp.int32)
    region_end = jnp.sum(jnp.where(tile_expert[:, None] == jnp.arange(N_EXPERTS), region + padded, 0), axis=-1)
    tile_rows = jnp.where(tiles < n_used, jnp.minimum(region_end - tiles * te, te), 0).astype(jnp.int32)

    xs = _dispatch(x, meta, run_tab, fill_tab, n_tiles * te, n_spare)
    ys = _experts(xs, we_gate, we_up, we_down, j, tile_expert, tile_rows)
    return _combine_ln(x, meta, run_tab, ys, ln_g, ln_b, alpha=alpha)


def kernel(x, w_in, conv_w, v_g, v_b, w_s, b_s, out_g, w_out, ln1_g, ln1_b, ln2_g, ln2_b,
           w_gate, w_up, w_down, w_router, we_gate, we_up, we_down):
    bsz, seq, d = x.shape
    depth = w_in.shape[0]
    alpha = float((2 * depth) ** 0.25)
    h = x.reshape(bsz * seq, d)
    row = lambda a: a.reshape(1, -1)
    mix_w = [w_in[0].astype(BF16), w_out[0].astype(BF16)]
    ffn_w = None
    for i in range(depth):
        j = i // 2
        dense = i % 2 == 0
        bias_t = jnp.repeat(b_s[i].T, GMLP_HEAD_DIM, axis=1)
        h, ffn_w = _mixer_ln(h, mix_w[0], conv_w[i], row(v_g[i]), row(v_b[i]), w_s[i], bias_t,
                             row(out_g[i]), mix_w[1], row(ln1_g[i]), row(ln1_b[i]),
                             cast=(w_gate[j], w_up[j], w_down[j]) if dense else (), seq=seq, alpha=alpha)
        last = i + 1 == depth
        if dense:
            ahead = () if last else (w_in[i + 1], w_out[i + 1], we_gate[j], we_up[j], we_down[j])
            h, ahead_w = _ffn_ln(h, *ffn_w, row(ln2_g[i]), row(ln2_b[i]), cast=ahead, alpha=alpha)
            if not last:
                mix_w, moe_w = ahead_w[:2], ahead_w[2:]
        else:
            h = _moe_ln(h, w_router[j], *moe_w, row(ln2_g[i]), row(ln2_b[i]), alpha=alpha)
            if not last:
                mix_w = [w_in[i + 1].astype(BF16), w_out[i + 1].astype(BF16)]
    return h.reshape(bsz, seq, d)
```

```python
import functools

import jax
import jax.numpy as jnp
from jax import lax
from jax.experimental import pallas as pl
from jax.experimental.pallas import tpu as pltpu

F32 = jnp.float32
BF16 = jnp.bfloat16

CONV_WIDTH = 512
CONV_HEAD_DIM = 64
GMLP_WIDTH = 512
GMLP_HEADS = 4
GMLP_HEAD_DIM = 128
CHUNK = 128
N_EXPERTS = 8
TOP_K = 2
LN_EPS = 1e-5
RMS_EPS = 1e-6

LANES = 128
SUBLANES = 8
BF16_ROWS = 16
VMEM_LIMIT_BYTES = 56 * 1024 * 1024

TM_MIX = 1024
SUB_MIX = 512
TM_FFN = 512
SUB_FFN = 512
TM_MOE = 512
MOE_TILES = 2
ROUTER_TILES = 1
TM_EXPERT = 1024
SUB_EXPERT = 512
FF_CHUNKS = 2

RUN_ALIGN = BF16_ROWS
ALIGN_SHIFT = RUN_ALIGN.bit_length() - 1
MAX_RUN = TM_MOE + RUN_ALIGN
RUN_BITS = (MAX_RUN // RUN_ALIGN).bit_length()
SLOTS = -(-(TOP_K * TM_MOE + N_EXPERTS * 2 * (RUN_ALIGN - 1)) // LANES) * LANES
GATE_LANES = LANES
TAIL_BITS = (TM_EXPERT // RUN_ALIGN - 1).bit_length()


def _layer_norm(r, g, b):
    mu = jnp.mean(r, axis=-1, keepdims=True)
    d = r - mu
    var = jnp.mean(d * d, axis=-1, keepdims=True)
    return d * lax.rsqrt(var + LN_EPS) * g + b


def _gelu(x):
    return 0.5 * x * (1.0 + lax.erf(x * (2.0 ** -0.5)))


def _silu(x):
    return x * (1.0 / (1.0 + jnp.exp(-x)))


def _params(semantics):
    return pltpu.CompilerParams(dimension_semantics=semantics, vmem_limit_bytes=VMEM_LIMIT_BYTES)


def _const_spec(shape):
    nd = len(shape)
    return pl.BlockSpec(shape, lambda *_: (0,) * nd)


def _cast_plan(arrays, n_steps):
    views, specs, shapes = [], [], []
    for a in arrays:
        v = a.reshape(-1, a.shape[-1])
        rows = v.shape[0]
        n_blocks = next(nb for nb in range(n_steps, 0, -1)
                        if n_steps % nb == 0 and rows % (nb * BF16_ROWS) == 0)
        steps_per_block = n_steps // n_blocks
        views.append(v)
        specs.append(pl.BlockSpec((rows // n_blocks, v.shape[1]), lambda i, r=steps_per_block: (i // r, 0)))
        shapes.append(jax.ShapeDtypeStruct(v.shape, BF16))
    return views, specs, shapes


def _cast_along(src_refs, dst_refs):
    for src, dst in zip(src_refs, dst_refs):
        dst[...] = src[...].astype(dst.dtype)


def _mixer_kernel(*refs, n_cast, tm, sub, seq, alpha):
    (x_ref, win_ref, convw_ref, vg_ref, vb_ref, ws_ref, bias_ref, outg_ref, wout_ref,
     lng_ref, lnb_ref) = refs[:11]
    o_ref, ch_ref = refs[11 + n_cast], refs[-1]
    _cast_along(refs[11:11 + n_cast], refs[12 + n_cast:12 + 2 * n_cast])
    i = pl.program_id(0)
    cw_, gw_ = CONV_WIDTH, GMLP_WIDTH

    @pl.when(lax.rem(i * tm, seq) == 0)
    def _():
        ch_ref[0:SUBLANES, :] = jnp.zeros((SUBLANES, cw_), F32)

    trow = lax.broadcasted_iota(jnp.int32, (CHUNK, CHUNK), 0)
    tcol = lax.broadcasted_iota(jnp.int32, (CHUNK, CHUNK), 1)
    causal = tcol <= trow
    ws = [jnp.where(causal, ws_ref[h], 0.0).astype(BF16) for h in range(GMLP_HEADS)]
    lane = lax.broadcasted_iota(jnp.int32, (sub, LANES), 1)
    low_half = lane < CONV_HEAD_DIM
    nch = sub // CHUNK

    for r0 in range(0, tm, sub):
        x = x_ref[r0:r0 + sub, :]
        proj = jnp.dot(x.astype(BF16), win_ref[...], preferred_element_type=F32)
        b_gate = proj[:, 0:cw_]
        c_gate = proj[:, cw_:2 * cw_]
        hh = proj[:, 2 * cw_:3 * cw_]
        u = proj[:, 3 * cw_:3 * cw_ + gw_]
        v = proj[:, 3 * cw_ + gw_:3 * cw_ + 2 * gw_]

        c0 = SUBLANES + r0
        ch_ref[c0:c0 + sub, :] = c_gate * hh
        convw = convw_ref[...]
        conv = (convw[0:1, :] * ch_ref[c0 - 2:c0 - 2 + sub, :]
                + convw[1:2, :] * ch_ref[c0 - 1:c0 - 1 + sub, :]
                + convw[2:3, :] * ch_ref[c0:c0 + sub, :])
        y_conv = b_gate * conv

        ug = _gelu(u)
        vn = _layer_norm(_gelu(v), vg_ref[...], vb_ref[...]).astype(BF16)
        zs = []
        for h in range(GMLP_HEADS):
            lo, hi = h * GMLP_HEAD_DIM, (h + 1) * GMLP_HEAD_DIM
            rhs = jnp.concatenate([vn[c * CHUNK:(c + 1) * CHUNK, lo:hi] for c in range(nch)], axis=1)
            zs.append(jnp.dot(ws[h], rhs, preferred_element_type=F32))
        z = jnp.concatenate(
            [jnp.concatenate([zs[h][:, c * GMLP_HEAD_DIM:(c + 1) * GMLP_HEAD_DIM]
                              for h in range(GMLP_HEADS)], axis=1) for c in range(nch)], axis=0)
        bias = jnp.concatenate([bias_ref[...]] * nch, axis=0)
        y_sg = ug * (z + bias)

        parts = []
        for j in range(cw_ // LANES):
            yt = y_conv[:, j * LANES:(j + 1) * LANES]
            sq = yt * yt
            ms_lo = jnp.sum(jnp.where(low_half, sq, 0.0), axis=-1, keepdims=True) * (1.0 / CONV_HEAD_DIM)
            ms_hi = jnp.sum(jnp.where(low_half, 0.0, sq), axis=-1, keepdims=True) * (1.0 / CONV_HEAD_DIM)
            parts.append(yt * jnp.where(low_half, lax.rsqrt(ms_lo + RMS_EPS), lax.rsqrt(ms_hi + RMS_EPS)))
        for h in range(GMLP_HEADS):
            yt = y_sg[:, h * GMLP_HEAD_DIM:(h + 1) * GMLP_HEAD_DIM]
            ms = jnp.mean(yt * yt, axis=-1, keepdims=True)
            parts.append(yt * lax.rsqrt(ms + RMS_EPS))
        y = jnp.concatenate(parts, axis=1) * outg_ref[...]

        mix = jnp.dot(y.astype(BF16), wout_ref[...], preferred_element_type=F32)
        o_ref[r0:r0 + sub, :] = _layer_norm(alpha * x + mix, lng_ref[...], lnb_ref[...])

    ch_ref[0:SUBLANES, :] = ch_ref[tm:tm + SUBLANES, :]


def _mixer_ln(x, w_in, conv_w, v_g, v_b, w_s, bias_t, out_g, w_out, ln_g, ln_b, cast=(), *, seq, alpha):
    n, d = x.shape
    tm, sub = TM_MIX, SUB_MIX
    assert n % tm == 0 and seq % tm == 0 and tm % sub == 0 and sub % CHUNK == 0
    views, cast_specs, cast_shapes = _cast_plan(cast, n // tm)
    kern = functools.partial(_mixer_kernel, n_cast=len(cast), tm=tm, sub=sub, seq=seq, alpha=alpha)
    row = pl.BlockSpec((tm, d), lambda i: (i, 0))
    out = pl.pallas_call(
        kern,
        out_shape=[jax.ShapeDtypeStruct((n, d), F32)] + cast_shapes,
        grid=(n // tm,),
        in_specs=[row, _const_spec(w_in.shape), _const_spec(conv_w.shape), _const_spec(v_g.shape),
                  _const_spec(v_b.shape), _const_spec(w_s.shape), _const_spec(bias_t.shape),
                  _const_spec(out_g.shape), _const_spec(w_out.shape), _const_spec(ln_g.shape),
                  _const_spec(ln_b.shape)] + cast_specs,
        out_specs=[row] + cast_specs,
        scratch_shapes=[pltpu.VMEM((tm + SUBLANES, CONV_WIDTH), F32)],
        compiler_params=_params(("arbitrary",)),
        name="mixer_ln",
    )(x, w_in, conv_w, v_g, v_b, w_s, bias_t, out_g, w_out, ln_g, ln_b, *views)
    return out[0], [o.reshape(a.shape) for o, a in zip(out[1:], cast)]


def _ffn_kernel(*refs, n_cast, sub, alpha):
    x_ref, wg_ref, wu_ref, wd_ref, lng_ref, lnb_ref = refs[:6]
    o_ref = refs[6 + n_cast]
    _cast_along(refs[6:6 + n_cast], refs[7 + n_cast:7 + 2 * n_cast])
    for r0 in range(0, x_ref.shape[0], sub):
        x = x_ref[r0:r0 + sub, :]
        xb = x.astype(BF16)
        g = jnp.dot(xb, wg_ref[...], preferred_element_type=F32)
        u = jnp.dot(xb, wu_ref[...], preferred_element_type=F32)
        h = (_silu(g) * u).astype(BF16)
        ffn = jnp.dot(h, wd_ref[...], preferred_element_type=F32)
        o_ref[r0:r0 + sub, :] = _layer_norm(alpha * x + ffn, lng_ref[...], lnb_ref[...])


def _ffn_ln(x, wg, wu, wd, ln_g, ln_b, cast=(), *, alpha):
    n, d = x.shape
    tm = TM_FFN
    assert n % tm == 0 and tm % SUB_FFN == 0
    views, cast_specs, cast_shapes = _cast_plan(cast, n // tm)
    row = pl.BlockSpec((tm, d), lambda i: (i, 0))
    out = pl.pallas_call(
        functools.partial(_ffn_kernel, n_cast=len(cast), sub=SUB_FFN, alpha=alpha),
        out_shape=[jax.ShapeDtypeStruct((n, d), F32)] + cast_shapes,
        grid=(n // tm,),
        in_specs=[row, _const_spec(wg.shape), _const_spec(wu.shape), _const_spec(wd.shape),
                  _const_spec(ln_g.shape), _const_spec(ln_b.shape)] + cast_specs,
        out_specs=[row] + cast_specs,
        compiler_params=_params(("arbitrary",)),
        name="ffn_ln",
    )(x, wg, wu, wd, ln_g, ln_b, *views)
    return out[0], [o.reshape(a.shape) for o, a in zip(out[1:], cast)]


_M_E0, _M_E1, _M_S0, _M_S1, _M_G0, _M_G1 = range(6)
_R_LSTART, _R_LEN, _R_GOFF, _R_PARTIAL = range(4)
RUN_FIELDS = 4


def _router_kernel(x_ref, wrh_ref, wrl_ref, meta_ref, runs_ref, tot_ref, base_ref, *, tm):
    i = pl.program_id(0)

    @pl.when(i == 0)
    def _():
        base_ref[...] = jnp.zeros(base_ref.shape, F32)

    for k in range(x_ref.shape[0] // tm):
        _route_tile(x_ref[k * tm:(k + 1) * tm, :], wrh_ref, wrl_ref, meta_ref.at[k * tm:(k + 1) * tm],
                    runs_ref.at[k * SUBLANES:(k + 1) * SUBLANES], tot_ref, base_ref)


def _route_tile(x, wrh_ref, wrl_ref, meta_ref, runs_ref, tot_ref, base_ref):
    tm = x.shape[0]
    xh = x.astype(BF16)
    xl = (x - xh.astype(F32)).astype(BF16)
    wrh = wrh_ref[...]
    logits = (jnp.dot(xh, wrh, preferred_element_type=F32)
              + jnp.dot(xl, wrh, preferred_element_type=F32)
              + jnp.dot(xh, wrl_ref[...], preferred_element_type=F32))
    lane = lax.broadcasted_iota(jnp.int32, (tm, LANES), 1)
    lanef = lane.astype(F32)
    neg = jnp.float32(-jnp.inf)
    lg = jnp.where(lane < N_EXPERTS, logits, neg)
    m0 = jnp.max(lg, axis=-1, keepdims=True)
    e0 = jnp.min(jnp.where(lg == m0, lanef, float(LANES)), axis=-1, keepdims=True)
    lg1 = jnp.where(lanef == e0, neg, lg)
    m1 = jnp.max(lg1, axis=-1, keepdims=True)
    e1 = jnp.min(jnp.where(lg1 == m1, lanef, float(LANES)), axis=-1, keepdims=True)
    t = jnp.exp(m1 - m0)
    g0 = 1.0 / (1.0 + t)
    g1 = t / (1.0 + t)

    oh0 = lanef == e0
    oh1 = lanef == e1
    cnt = jnp.where(oh0 | oh1, 1.0, 0.0)
    r_i = lax.broadcasted_iota(jnp.int32, (tm, tm), 0)
    c_i = lax.broadcasted_iota(jnp.int32, (tm, tm), 1)
    tri = jnp.where(c_i < r_i, 1.0, 0.0).astype(BF16)
    rank = jnp.dot(tri, cnt.astype(BF16), preferred_element_type=F32)

    def align_down(v):
        return jnp.floor(v * (1.0 / RUN_ALIGN)) * RUN_ALIGN

    count = jnp.broadcast_to(jnp.sum(cnt, axis=0, keepdims=True), (SUBLANES, LANES))
    before = base_ref[...]
    phase = before - align_down(before)
    end = phase + count
    run_len = jnp.where(count > 0, align_down(end + (RUN_ALIGN - 1.0)), 0.0)
    partial = jnp.where((count > 0) & (end != align_down(end)), 1.0, 0.0)
    lane8 = lax.broadcasted_iota(jnp.int32, (SUBLANES, LANES), 1)
    incl = run_len
    for sh in (1, 2, 4):
        incl = incl + jnp.where(lane8 >= sh, pltpu.roll(incl, sh, axis=1), 0.0)
    lstart = incl - run_len
    base_ref[...] = before + count
    tot_ref[...] = before + count
    row8 = lax.broadcasted_iota(jnp.int32, (SUBLANES, LANES), 0)
    table = jnp.zeros((SUBLANES, LANES), F32)
    for r, val in ((_R_LSTART, lstart), (_R_LEN, run_len), (_R_GOFF, before - phase), (_R_PARTIAL, partial)):
        table = jnp.where(row8 == r, val, table)
    runs_ref[...] = table

    slot = rank + (lstart + phase)[0:1, :]
    s0 = jnp.sum(jnp.where(oh0, slot, 0.0), axis=-1, keepdims=True)
    s1 = jnp.sum(jnp.where(oh1, slot, 0.0), axis=-1, keepdims=True)
    meta = jnp.zeros((tm, LANES), F32)
    for k, val in ((_M_E0, e0), (_M_E1, e1), (_M_S0, s0), (_M_S1, s1), (_M_G0, g0), (_M_G1, g1)):
        meta = jnp.where(lane == k, val, meta)
    meta_ref[...] = meta


def _router(x, wr_hi, wr_lo):
    n, d = x.shape
    tm, k = TM_MOE, ROUTER_TILES
    assert n % (k * tm) == 0
    nt = n // tm
    small = (SUBLANES, LANES)
    return pl.pallas_call(
        functools.partial(_router_kernel, tm=tm),
        out_shape=(jax.ShapeDtypeStruct((n, LANES), F32), jax.ShapeDtypeStruct((nt * SUBLANES, LANES), F32),
                   jax.ShapeDtypeStruct(small, F32)),
        grid=(nt // k,),
        in_specs=[pl.BlockSpec((k * tm, d), lambda i: (i, 0)), _const_spec(wr_hi.shape), _const_spec(wr_lo.shape)],
        out_specs=(pl.BlockSpec((k * tm, LANES), lambda i: (i, 0)),
                   pl.BlockSpec((k * SUBLANES, LANES), lambda i: (i, 0)), _const_spec(small)),
        scratch_shapes=[pltpu.VMEM(small, F32)],
        compiler_params=_params(("arbitrary",)),
        name="router",
    )(x, wr_hi, wr_lo)


def _run_copies(tab_ref, tile, local_ref, glob_hbm, sem, *, to_global, start):
    base = tile * (RUN_FIELDS * N_EXPERTS)
    for e in range(N_EXPERTS):
        lstart = tab_ref[base + e]
        length = tab_ref[base + N_EXPERTS + e]
        gstart = tab_ref[base + 2 * N_EXPERTS + e]
        for b in range(RUN_BITS):
            size = RUN_ALIGN << b
            off = (length >> (b + ALIGN_SHIFT + 1)) << (b + ALIGN_SHIFT + 1)

            def piece(size=size, off=off, lstart=lstart, gstart=gstart):
                loc = local_ref.at[pl.ds(pl.multiple_of(lstart + off, RUN_ALIGN), size)]
                glo = glob_hbm.at[pl.ds(pl.multiple_of(gstart + off, RUN_ALIGN), size)]
                cp = pltpu.make_async_copy(loc, glo, sem) if to_global else pltpu.make_async_copy(glo, loc, sem)
                if start:
                    cp.start()
                else:
                    cp.wait()

            pl.when(((length >> (b + ALIGN_SHIFT)) & 1) == 1)(piece)


def _fill_copies(fill_ref, zero_ref, glob_hbm, sem, *, te, n_spare, start):
    def go(cp):
        if start:
            cp.start()
        else:
            cp.wait()

    for e in range(N_EXPERTS):
        tstart = fill_ref[e]
        length = fill_ref[N_EXPERTS + e]
        for b in range(TAIL_BITS):
            size = RUN_ALIGN << b
            off = (length >> (b + ALIGN_SHIFT + 1)) << (b + ALIGN_SHIFT + 1)

            def piece(size=size, off=off, tstart=tstart):
                go(pltpu.make_async_copy(zero_ref.at[pl.ds(0, size)],
                                         glob_hbm.at[pl.ds(pl.multiple_of(tstart + off, RUN_ALIGN), size)], sem))

            pl.when(((length >> (b + ALIGN_SHIFT)) & 1) == 1)(piece)
    spare0 = fill_ref[2 * N_EXPERTS]
    for j in range(n_spare):
        def tile_fill(j=j):
            go(pltpu.make_async_copy(zero_ref, glob_hbm.at[pl.ds(pl.multiple_of(spare0 + j * te, te), te)], sem))

        pl.when(j < fill_ref[2 * N_EXPERTS + 1])(tile_fill)


def _sort_tile(rows, x_ref, meta_ref, loc_ref, *, tm):
    d = x_ref.shape[1]
    meta = meta_ref[rows, :]
    meta_t = meta.T
    s0 = meta_t[_M_S0:_M_S0 + 1, :]
    s1 = meta_t[_M_S1:_M_S1 + 1, :]
    slot = lax.broadcasted_iota(jnp.int32, (SLOTS, tm), 0).astype(F32)
    p0 = slot == s0
    p1 = slot == s1
    onehot = jnp.where(p0 | p1, 1.0, 0.0).astype(BF16)
    loc_ref[:, 0:d] = jnp.dot(onehot, x_ref[rows, :].astype(BF16), preferred_element_type=F32).astype(BF16)

    lane = lax.broadcasted_iota(jnp.int32, (tm, GATE_LANES), 1)

    def pieces(g):
        h = g.astype(BF16).astype(F32)
        r1 = g - h
        m = r1.astype(BF16).astype(F32)
        l = (r1 - m).astype(BF16).astype(F32)
        return jnp.where(lane == 0, h, jnp.where(lane == 1, m, jnp.where(lane == 2, l, 0.0))).astype(BF16)

    gate = (jnp.dot(jnp.where(p0, 1.0, 0.0).astype(BF16), pieces(meta[:, _M_G0:_M_G0 + 1]),
                    preferred_element_type=F32)
            + jnp.dot(jnp.where(p1, 1.0, 0.0).astype(BF16), pieces(meta[:, _M_G1:_M_G1 + 1]),
                      preferred_element_type=F32))
    loc_ref[:, d:d + GATE_LANES] = gate.astype(BF16)


def _merge_carry(tile, tab_ref, loc_ref, carry_ref):
    base = tile * (RUN_FIELDS * N_EXPERTS)
    for e in range(N_EXPERTS):
        lstart = tab_ref[base + _R_LSTART * N_EXPERTS + e]
        length = tab_ref[base + _R_LEN * N_EXPERTS + e]
        partial = tab_ref[base + _R_PARTIAL * N_EXPERTS + e]

        def merge(e=e, lstart=lstart, length=length, partial=partial):
            first = pl.ds(pl.multiple_of(lstart, RUN_ALIGN), RUN_ALIGN)
            loc_ref[first, :] = loc_ref[first, :] + carry_ref[e]
            last = pl.ds(pl.multiple_of(lstart + length - RUN_ALIGN, RUN_ALIGN), RUN_ALIGN)

            @pl.when(partial == 1)
            def _():
                carry_ref[e] = loc_ref[last, :]

            @pl.when(partial == 0)
            def _():
                carry_ref[e] = jnp.zeros(carry_ref.shape[1:], BF16)

        pl.when(length > 0)(merge)


def _dispatch_kernel(tab_ref, fill_ref, x_ref, meta_ref, xs_hbm, *scratch, tm, te, n_spare):
    *loc_refs, zero_ref, carry_ref, sem, zsem = scratch
    i = pl.program_id(0)

    @pl.when(i == 0)
    def _():
        zero_ref[...] = jnp.zeros(zero_ref.shape, BF16)
        carry_ref[...] = jnp.zeros(carry_ref.shape, BF16)
        _fill_copies(fill_ref, zero_ref, xs_hbm, zsem, te=te, n_spare=n_spare, start=True)

    n_tiles = x_ref.shape[0] // tm
    for k in range(n_tiles):
        tile = n_tiles * i + k
        _sort_tile(pl.ds(k * tm, tm), x_ref, meta_ref, loc_refs[k], tm=tm)
        if k > 0:
            _run_copies(tab_ref, tile - 1, loc_refs[k - 1], xs_hbm, sem, to_global=True, start=False)
        _merge_carry(tile, tab_ref, loc_refs[k], carry_ref)
        _run_copies(tab_ref, tile, loc_refs[k], xs_hbm, sem, to_global=True, start=True)
    _run_copies(tab_ref, n_tiles * i + n_tiles - 1, loc_refs[-1], xs_hbm, sem, to_global=True, start=False)

    @pl.when(i == pl.num_programs(0) - 1)
    def _():
        _fill_copies(fill_ref, zero_ref, xs_hbm, zsem, te=te, n_spare=n_spare, start=False)


def _dispatch(x, meta, run_tab, fill_tab, n_rows, n_spare):
    n, d = x.shape
    tm, te, k = TM_MOE, TM_EXPERT, MOE_TILES
    width = d + GATE_LANES
    grid_spec = pltpu.PrefetchScalarGridSpec(
        num_scalar_prefetch=2,
        grid=(n // (k * tm),),
        in_specs=[pl.BlockSpec((k * tm, d), lambda i, *_: (i, 0)),
                  pl.BlockSpec((k * tm, LANES), lambda i, *_: (i, 0))],
        out_specs=pl.BlockSpec(memory_space=pl.ANY),
        scratch_shapes=[pltpu.VMEM((SLOTS, width), BF16)] * k + [pltpu.VMEM((te, width), BF16),
                        pltpu.VMEM((N_EXPERTS, RUN_ALIGN, width), BF16),
                        pltpu.SemaphoreType.DMA, pltpu.SemaphoreType.DMA],
    )
    return pl.pallas_call(
        functools.partial(_dispatch_kernel, tm=tm, te=te, n_spare=n_spare),
        out_shape=jax.ShapeDtypeStruct((n_rows, width), BF16),
        grid_spec=grid_spec,
        compiler_params=_params(("arbitrary",)),
        name="dispatch",
    )(run_tab, fill_tab, x, meta)


def _experts_kernel(tile_ref, expert_ref, rows_ref, xs_ref, wg_ref, wu_ref, wd_ref, ys_ref, acc_ref, *, sub):
    i = pl.program_id(0)
    f = pl.program_id(1)
    te, d = ys_ref.shape
    rows = rows_ref[i]

    @pl.when((rows == 0) & (f == 0))
    def _():
        ys_ref[...] = jnp.zeros(ys_ref.shape, ys_ref.dtype)

    def partial_out(r0):
        xb = xs_ref[r0:r0 + sub, 0:d]
        g = jnp.dot(xb, wg_ref[0], preferred_element_type=F32)
        u = jnp.dot(xb, wu_ref[0], preferred_element_type=F32)
        h = (_silu(g) * u).astype(BF16)
        return jnp.dot(h, wd_ref[0], preferred_element_type=F32)

    def first_chunk(n_rows):
        for r0 in range(0, n_rows, sub):
            acc_ref[r0:r0 + sub, :] = partial_out(r0)

    def last_chunk(n_rows):
        for r0 in range(0, n_rows, sub):
            gate = jnp.sum(xs_ref[r0:r0 + sub, d:d + GATE_LANES].astype(F32), axis=-1, keepdims=True)
            ys_ref[r0:r0 + sub, :] = ((acc_ref[r0:r0 + sub, :] + partial_out(r0)) * gate).astype(ys_ref.dtype)
        if n_rows < te:
            ys_ref[n_rows:te, :] = jnp.zeros((te - n_rows, d), ys_ref.dtype)

    for nb in range(1, te // sub + 1):
        fits = (rows > (nb - 1) * sub) & (rows <= nb * sub)
        pl.when(fits & (f == 0))(functools.partial(first_chunk, nb * sub))
        pl.when(fits & (f == 1))(functools.partial(last_chunk, nb * sub))


def _experts(xs, we_gate, we_up, we_down, tile_idx, tile_expert, tile_rows):
    n_rows, width = xs.shape
    d = width - GATE_LANES
    te = TM_EXPERT
    ff = we_gate.shape[-1]
    fc = ff // FF_CHUNKS
    assert n_rows % te == 0 and ff % FF_CHUNKS == 0 and fc % LANES == 0 and FF_CHUNKS == 2
    n_tiles = n_rows // te
    last = FF_CHUNKS - 1
    grid_spec = pltpu.PrefetchScalarGridSpec(
        num_scalar_prefetch=3,
        grid=(n_tiles, FF_CHUNKS),
        in_specs=[
            pl.BlockSpec((te, width), lambda i, f, tile, ex, rows: (tile[i], 0)),
            pl.BlockSpec((1, d, fc), lambda i, f, tile, ex, rows: (ex[i], 0, jnp.where(rows[i] > 0, f, last))),
            pl.BlockSpec((1, d, fc), lambda i, f, tile, ex, rows: (ex[i], 0, jnp.where(rows[i] > 0, f, last))),
            pl.BlockSpec((1, fc, d), lambda i, f, tile, ex, rows: (ex[i], jnp.where(rows[i] > 0, f, last), 0)),
        ],
        out_specs=pl.BlockSpec((te, d), lambda i, f, tile, ex, rows: (i, 0)),
        scratch_shapes=[pltpu.VMEM((te, d), F32)],
    )
    return pl.pallas_call(
        functools.partial(_experts_kernel, sub=SUB_EXPERT),
        out_shape=jax.ShapeDtypeStruct((n_rows, d), BF16),
        grid_spec=grid_spec,
        compiler_params=_params(("arbitrary", "arbitrary")),
        name="experts",
    )(tile_idx, tile_expert, tile_rows, xs, we_gate, we_up, we_down)


def _combine_kernel(tab_ref, x_ref, meta_ref, lng_ref, lnb_ref, ys_hbm, o_ref, loc_ref, sem, *, tm, alpha):
    i = pl.program_id(0)
    n_steps = pl.num_programs(0)
    n_tiles = x_ref.shape[0] // tm
    slot = lax.rem(i, 2)

    def copies(step, s, start):
        for k in range(n_tiles):
            _run_copies(tab_ref, n_tiles * step + k, loc_ref.at[s, k], ys_hbm, sem.at[s], to_global=False,
                        start=start)

    @pl.when(i == 0)
    def _():
        loc_ref[...] = jnp.zeros(loc_ref.shape, loc_ref.dtype)
        copies(0, 0, True)

    @pl.when(i + 1 < n_steps)
    def _():
        copies(i + 1, 1 - slot, True)

    copies(i, slot, False)

    lane = lax.broadcasted_iota(jnp.int32, (tm, SLOTS), 1).astype(F32)
    for k in range(n_tiles):
        rows = pl.ds(k * tm, tm)
        meta = meta_ref[rows, :]
        s0 = meta[:, _M_S0:_M_S0 + 1]
        s1 = meta[:, _M_S1:_M_S1 + 1]
        pick = jnp.where((lane == s0) | (lane == s1), 1.0, 0.0).astype(BF16)
        moe = jnp.dot(pick, loc_ref[slot, k], preferred_element_type=F32)
        o_ref[rows, :] = _layer_norm(alpha * x_ref[rows, :] + moe, lng_ref[...], lnb_ref[...])


def _combine_ln(x, meta, run_tab, ys, ln_g, ln_b, *, alpha):
    n, d = x.shape
    tm, k = TM_MOE, MOE_TILES
    n_steps = n // (k * tm)
    row = pl.BlockSpec((k * tm, d), lambda i, *_: (i, 0))
    grid_spec = pltpu.PrefetchScalarGridSpec(
        num_scalar_prefetch=1,
        grid=(n_steps,),
        in_specs=[row, pl.BlockSpec((k * tm, LANES), lambda i, *_: (i, 0)),
                  pl.BlockSpec(ln_g.shape, lambda i, *_: (0, 0)), pl.BlockSpec(ln_b.shape, lambda i, *_: (0, 0)),
                  pl.BlockSpec(memory_space=pl.ANY)],
        out_specs=row,
        scratch_shapes=[pltpu.VMEM((2, k, SLOTS, d), BF16), pltpu.SemaphoreType.DMA((2,))],
    )
    return pl.pallas_call(
        functools.partial(_combine_kernel, tm=tm, alpha=alpha),
        out_shape=jax.ShapeDtypeStruct((n, d), F32),
        grid_spec=grid_spec,
        compiler_params=_params(("arbitrary",)),
        name="combine_ln",
    )(run_tab, x, meta, ln_g, ln_b, ys)


def _moe_ln(x, w_router, we_gate, we_up, we_down, ln_g, ln_b, *, alpha):
    n, d = x.shape
    tm, te = TM_MOE, TM_EXPERT
    assert n % (MOE_TILES * tm) == 0 and te % SUB_EXPERT == 0
    nt = n // tm
    wr = jnp.pad(w_router, ((0, 0), (0, LANES - N_EXPERTS)))
    wr_hi = wr.astype(BF16)
    wr_lo = (wr - wr_hi.astype(F32)).astype(BF16)
    meta, runs, totals = _router(x, wr_hi, wr_lo)

    counts = totals[0, :N_EXPERTS].astype(jnp.int32)
    padded = (counts + RUN_ALIGN - 1) // RUN_ALIGN * RUN_ALIGN
    tiles_e = (padded + te - 1) // te
    tile_end = jnp.cumsum(tiles_e)
    region = (tile_end - tiles_e) * te
    n_used = tile_end[-1]
    max_rows = TOP_K * n + N_EXPERTS * (te - 1)
    n_tiles = -(-max_rows // te)
    n_spare = n_tiles - (TOP_K * n) // te
    runs_i = runs.reshape(nt, SUBLANES, LANES)[:, :RUN_FIELDS, :N_EXPERTS].astype(jnp.int32)
    runs_i = runs_i.at[:, _R_GOFF, :].add(region)
    run_tab = runs_i.reshape(-1)
    fill_tab = jnp.concatenate([region + padded, tiles_e * te - padded,
                                (n_used * te)[None], (n_tiles - n_used)[None]]).astype(jnp.int32)
    tiles = jnp.arange(n_tiles, dtype=jnp.int32)
    j = jnp.minimum(tiles, n_used - 1)
    tile_expert = jnp.minimum(jnp.sum(j[:, None] >= tile_end[None, :], axis=-1), N_EXPERTS - 1).astype(jnp.int32)
    region_end = jnp.sum(jnp.where(tile_expert[:, None] == jnp.arange(N_EXPERTS), region + padded, 0), axis=-1)
    tile_rows = jnp.where(tiles < n_used, jnp.minimum(region_end - tiles * te, te), 0).astype(jnp.int32)

    xs = _dispatch(x, meta, run_tab, fill_tab, n_tiles * te, n_spare)
    ys = _experts(xs, we_gate, we_up, we_down, j, tile_expert, tile_rows)
    return _combine_ln(x, meta, run_tab, ys, ln_g, ln_b, alpha=alpha)


def kernel(x, w_in, conv_w, v_g, v_b, w_s, b_s, out_g, w_out, ln1_g, ln1_b, ln2_g, ln2_b,
           w_gate, w_up, w_down, w_router, we_gate, we_up, we_down):
    bsz, seq, d = x.shape
    depth = w_in.shape[0]
    alpha = float((2 * depth) ** 0.25)
    h = x.reshape(bsz * seq, d)
    row = lambda a: a.reshape(1, -1)
    mix_w = [w_in[0].astype(BF16), w_out[0].astype(BF16)]
    ffn_w = None
    for i in range(depth):
        j = i // 2
        dense = i % 2 == 0
        bias_t = jnp.repeat(b_s[i].T, GMLP_HEAD_DIM, axis=1)
        h, ffn_w = _mixer_ln(h, mix_w[0], conv_w[i], row(v_g[i]), row(v_b[i]), w_s[i], bias_t,
                             row(out_g[i]), mix_w[1], row(ln1_g[i]), row(ln1_b[i]),
                             cast=(w_gate[j], w_up[j], w_down[j]) if dense else (), seq=seq, alpha=alpha)
        last = i + 1 == depth
        if dense:
            ahead = () if last else (w_in[i + 1], w_out[i + 1], we_gate[j], we_up[j], we_down[j])
            h, ahead_w = _ffn_ln(h, *ffn_w, row(ln2_g[i]), row(ln2_b[i]), cast=ahead, alpha=alpha)
            if not last:
                mix_w, moe_w = ahead_w[:2], ahead_w[2:]
        else:
            h = _moe_ln(h, w_router[j], *moe_w, row(ln2_g[i]), row(ln2_b[i]), alpha=alpha)
            if not last:
                mix_w = [w_in[i + 1].astype(BF16), w_out[i + 1].astype(BF16)]
    return h.reshape(bsz, seq, d)
```

```python
import functools

import numpy as np
import jax
import jax.numpy as jnp
from jax import lax
from jax.experimental import pallas as pl
from jax.experimental.pallas import tpu as pltpu

F32 = jnp.float32
BF16 = jnp.bfloat16

CONV_WIDTH = 512
CONV_HEAD_DIM = 64
GMLP_WIDTH = 512
GMLP_HEADS = 4
GMLP_HEAD_DIM = 128
CHUNK = 128
N_EXPERTS = 8
TOP_K = 2
LN_EPS = 1e-5
RMS_EPS = 1e-6

LANES = 128
SUBLANES = 8
BF16_ROWS = 16
VMEM_LIMIT_BYTES = 56 * 1024 * 1024

TM_MIX = 1024
SUB_MIX = 512
TM_FFN = 1024
SUB_FFN = 512
TM_MOE = 512
MOE_TILES = 2
ROUTER_TILES = 1
TM_EXPERT = 1024
SUB_EXPERT = 512
FF_CHUNKS = 2

RUN_ALIGN = BF16_ROWS
ALIGN_SHIFT = RUN_ALIGN.bit_length() - 1
MAX_RUN = TM_MOE + RUN_ALIGN
RUN_BITS = (MAX_RUN // RUN_ALIGN).bit_length()
SLOTS = -(-(TOP_K * TM_MOE + N_EXPERTS * 2 * (RUN_ALIGN - 1)) // LANES) * LANES
GATE_LANES = LANES
TAIL_BITS = (TM_EXPERT // RUN_ALIGN - 1).bit_length()


def _layer_norm(r, g, b):
    mu = jnp.mean(r, axis=-1, keepdims=True)
    d = r - mu
    var = jnp.mean(d * d, axis=-1, keepdims=True)
    return d * lax.rsqrt(var + LN_EPS) * g + b


def _gelu(x):
    return 0.5 * x * (1.0 + lax.erf(x * (2.0 ** -0.5)))


def _silu(x):
    return x * (1.0 / (1.0 + jnp.exp(-x)))


def _params(semantics):
    return pltpu.CompilerParams(dimension_semantics=semantics, vmem_limit_bytes=VMEM_LIMIT_BYTES)


def _const_spec(shape):
    nd = len(shape)
    return pl.BlockSpec(shape, lambda *_: (0,) * nd)


def _cast_plan(weights, n_steps):
    views, in_specs, out_specs, out_shapes, shapes = [], [], [], [], []
    for w in weights:
        a, layer = w if isinstance(w, tuple) else (w, None)
        shape = a.shape if layer is None else a.shape[1:]
        rows, cols = int(np.prod(shape[:-1])), shape[-1]
        n_blocks = next(nb for nb in range(n_steps, 0, -1)
                        if n_steps % nb == 0 and rows % (nb * BF16_ROWS) == 0)
        r, first = n_steps // n_blocks, 0 if layer is None else layer * n_blocks
        views.append(a.reshape(-1, cols))
        in_specs.append(pl.BlockSpec((rows // n_blocks, cols), lambda i, r=r, first=first: (first + i // r, 0)))
        out_specs.append(pl.BlockSpec((rows // n_blocks, cols), lambda i, r=r: (i // r, 0)))
        out_shapes.append(jax.ShapeDtypeStruct((rows, cols), BF16))
        shapes.append(shape)
    return views, in_specs, out_specs, out_shapes, shapes


def _cast_along(src_refs, dst_refs):
    for src, dst in zip(src_refs, dst_refs):
        dst[...] = src[...].astype(dst.dtype)


def _mixer_kernel(*refs, n_cast, tm, sub, seq, alpha):
    (x_ref, win_ref, convw_ref, vg_ref, vb_ref, ws_ref, bias_ref, outg_ref, wout_ref,
     lng_ref, lnb_ref) = refs[:11]
    o_ref, ch_ref = refs[11 + n_cast], refs[-1]
    _cast_along(refs[11:11 + n_cast], refs[12 + n_cast:12 + 2 * n_cast])
    i = pl.program_id(0)
    cw_, gw_ = CONV_WIDTH, GMLP_WIDTH

    @pl.when(lax.rem(i * tm, seq) == 0)
    def _():
        ch_ref[0:SUBLANES, :] = jnp.zeros((SUBLANES, cw_), F32)

    trow = lax.broadcasted_iota(jnp.int32, (CHUNK, CHUNK), 0)
    tcol = lax.broadcasted_iota(jnp.int32, (CHUNK, CHUNK), 1)
    causal = tcol <= trow
    ws = [jnp.where(causal, ws_ref[h], 0.0).astype(BF16) for h in range(GMLP_HEADS)]
    lane = lax.broadcasted_iota(jnp.int32, (sub, LANES), 1)
    low_half = lane < CONV_HEAD_DIM
    nch = sub // CHUNK

    for r0 in range(0, tm, sub):
        x = x_ref[r0:r0 + sub, :]
        proj = jnp.dot(x.astype(BF16), win_ref[...], preferred_element_type=F32)
        b_gate = proj[:, 0:cw_]
        c_gate = proj[:, cw_:2 * cw_]
        hh = proj[:, 2 * cw_:3 * cw_]
        u = proj[:, 3 * cw_:3 * cw_ + gw_]
        v = proj[:, 3 * cw_ + gw_:3 * cw_ + 2 * gw_]

        c0 = SUBLANES + r0
        ch_ref[c0:c0 + sub, :] = c_gate * hh
        convw = convw_ref[...]
        conv = (convw[0:1, :] * ch_ref[c0 - 2:c0 - 2 + sub, :]
                + convw[1:2, :] * ch_ref[c0 - 1:c0 - 1 + sub, :]
                + convw[2:3, :] * ch_ref[c0:c0 + sub, :])
        y_conv = b_gate * conv

        ug = _gelu(u)
        vn = _layer_norm(_gelu(v), vg_ref[...], vb_ref[...]).astype(BF16)
        zs = []
        for h in range(GMLP_HEADS):
            lo, hi = h * GMLP_HEAD_DIM, (h + 1) * GMLP_HEAD_DIM
            rhs = jnp.concatenate([vn[c * CHUNK:(c + 1) * CHUNK, lo:hi] for c in range(nch)], axis=1)
            zs.append(jnp.dot(ws[h], rhs, preferred_element_type=F32))
        z = jnp.concatenate(
            [jnp.concatenate([zs[h][:, c * GMLP_HEAD_DIM:(c + 1) * GMLP_HEAD_DIM]
                              for h in range(GMLP_HEADS)], axis=1) for c in range(nch)], axis=0)
        bias = jnp.concatenate([bias_ref[...]] * nch, axis=0)
        y_sg = ug * (z + bias)

        parts = []
        for j in range(cw_ // LANES):
            yt = y_conv[:, j * LANES:(j + 1) * LANES]
            sq = yt * yt
            ms_lo = jnp.sum(jnp.where(low_half, sq, 0.0), axis=-1, keepdims=True) * (1.0 / CONV_HEAD_DIM)
            ms_hi = jnp.sum(jnp.where(low_half, 0.0, sq), axis=-1, keepdims=True) * (1.0 / CONV_HEAD_DIM)
            parts.append(yt * jnp.where(low_half, lax.rsqrt(ms_lo + RMS_EPS), lax.rsqrt(ms_hi + RMS_EPS)))
        for h in range(GMLP_HEADS):
            yt = y_sg[:, h * GMLP_HEAD_DIM:(h + 1) * GMLP_HEAD_DIM]
            ms = jnp.mean(yt * yt, axis=-1, keepdims=True)
            parts.append(yt * lax.rsqrt(ms + RMS_EPS))
        y = jnp.concatenate(parts, axis=1) * outg_ref[...]

        mix = jnp.dot(y.astype(BF16), wout_ref[...], preferred_element_type=F32)
        o_ref[r0:r0 + sub, :] = _layer_norm(alpha * x + mix, lng_ref[...], lnb_ref[...])

    ch_ref[0:SUBLANES, :] = ch_ref[tm:tm + SUBLANES, :]


def _mixer_ln(x, w_in, conv_w, v_g, v_b, w_s, bias_t, out_g, w_out, ln_g, ln_b, cast=(), *, seq, alpha):
    n, d = x.shape
    tm, sub = TM_MIX, SUB_MIX
    assert n % tm == 0 and seq % tm == 0 and tm % sub == 0 and sub % CHUNK == 0
    views, cast_in, cast_out, cast_shapes, shapes = _cast_plan(cast, n // tm)
    kern = functools.partial(_mixer_kernel, n_cast=len(cast), tm=tm, sub=sub, seq=seq, alpha=alpha)
    row = pl.BlockSpec((tm, d), lambda i: (i, 0))
    out = pl.pallas_call(
        kern,
        out_shape=[jax.ShapeDtypeStruct((n, d), F32)] + cast_shapes,
        grid=(n // tm,),
        in_specs=[row, _const_spec(w_in.shape), _const_spec(conv_w.shape), _const_spec(v_g.shape),
                  _const_spec(v_b.shape), _const_spec(w_s.shape), _const_spec(bias_t.shape),
                  _const_spec(out_g.shape), _const_spec(w_out.shape), _const_spec(ln_g.shape),
                  _const_spec(ln_b.shape)] + cast_in,
        out_specs=[row] + cast_out,
        scratch_shapes=[pltpu.VMEM((tm + SUBLANES, CONV_WIDTH), F32)],
        compiler_params=_params(("arbitrary",)),
        name="mixer_ln",
    )(x, w_in, conv_w, v_g, v_b, w_s, bias_t, out_g, w_out, ln_g, ln_b, *views)
    return out[0], [o.reshape(shape) for o, shape in zip(out[1:], shapes)]


def _ffn_kernel(*refs, n_cast, sub, alpha):
    x_ref, wg_ref, wu_ref, wd_ref, lng_ref, lnb_ref = refs[:6]
    o_ref = refs[6 + n_cast]
    _cast_along(refs[6:6 + n_cast], refs[7 + n_cast:7 + 2 * n_cast])
    for r0 in range(0, x_ref.shape[0], sub):
        x = x_ref[r0:r0 + sub, :]
        xb = x.astype(BF16)
        g = jnp.dot(xb, wg_ref[...], preferred_element_type=F32)
        u = jnp.dot(xb, wu_ref[...], preferred_element_type=F32)
        h = (_silu(g) * u).astype(BF16)
        ffn = jnp.dot(h, wd_ref[...], preferred_element_type=F32)
        o_ref[r0:r0 + sub, :] = _layer_norm(alpha * x + ffn, lng_ref[...], lnb_ref[...])


def _ffn_ln(x, wg, wu, wd, ln_g, ln_b, cast=(), *, alpha):
    n, d = x.shape
    tm = TM_FFN
    assert n % tm == 0 and tm % SUB_FFN == 0
    views, cast_in, cast_out, cast_shapes, shapes = _cast_plan(cast, n // tm)
    row = pl.BlockSpec((tm, d), lambda i: (i, 0))
    out = pl.pallas_call(
        functools.partial(_ffn_kernel, n_cast=len(cast), sub=SUB_FFN, alpha=alpha),
        out_shape=[jax.ShapeDtypeStruct((n, d), F32)] + cast_shapes,
        grid=(n // tm,),
        in_specs=[row, _const_spec(wg.shape), _const_spec(wu.shape), _const_spec(wd.shape),
                  _const_spec(ln_g.shape), _const_spec(ln_b.shape)] + cast_in,
        out_specs=[row] + cast_out,
        compiler_params=_params(("arbitrary",)),
        name="ffn_ln",
    )(x, wg, wu, wd, ln_g, ln_b, *views)
    return out[0], [o.reshape(shape) for o, shape in zip(out[1:], shapes)]


_M_E0, _M_E1, _M_S0, _M_S1, _M_G0, _M_G1 = range(6)
_R_LSTART, _R_LEN, _R_GOFF, _R_PARTIAL = range(4)
RUN_FIELDS = 4


def _router_kernel(x_ref, wrh_ref, wrl_ref, meta_ref, runs_ref, tot_ref, base_ref, *, tm):
    i = pl.program_id(0)

    @pl.when(i == 0)
    def _():
        base_ref[...] = jnp.zeros(base_ref.shape, F32)

    for k in range(x_ref.shape[0] // tm):
        _route_tile(x_ref[k * tm:(k + 1) * tm, :], wrh_ref, wrl_ref, meta_ref.at[k * tm:(k + 1) * tm],
                    runs_ref.at[k * SUBLANES:(k + 1) * SUBLANES], tot_ref, base_ref)


def _route_tile(x, wrh_ref, wrl_ref, meta_ref, runs_ref, tot_ref, base_ref):
    tm = x.shape[0]
    xh = x.astype(BF16)
    xl = (x - xh.astype(F32)).astype(BF16)
    wrh = wrh_ref[...]
    logits = (jnp.dot(xh, wrh, preferred_element_type=F32)
              + jnp.dot(xl, wrh, preferred_element_type=F32)
              + jnp.dot(xh, wrl_ref[...], preferred_element_type=F32))
    lane = lax.broadcasted_iota(jnp.int32, (tm, LANES), 1)
    lanef = lane.astype(F32)
    neg = jnp.float32(-jnp.inf)
    lg = jnp.where(lane < N_EXPERTS, logits, neg)
    m0 = jnp.max(lg, axis=-1, keepdims=True)
    e0 = jnp.min(jnp.where(lg == m0, lanef, float(LANES)), axis=-1, keepdims=True)
    lg1 = jnp.where(lanef == e0, neg, lg)
    m1 = jnp.max(lg1, axis=-1, keepdims=True)
    e1 = jnp.min(jnp.where(lg1 == m1, lanef, float(LANES)), axis=-1, keepdims=True)
    t = jnp.exp(m1 - m0)
    g0 = 1.0 / (1.0 + t)
    g1 = t / (1.0 + t)

    oh0 = lanef == e0
    oh1 = lanef == e1
    cnt = jnp.where(oh0 | oh1, 1.0, 0.0)
    r_i = lax.broadcasted_iota(jnp.int32, (tm, tm), 0)
    c_i = lax.broadcasted_iota(jnp.int32, (tm, tm), 1)
    tri = jnp.where(c_i < r_i, 1.0, 0.0).astype(BF16)
    rank = jnp.dot(tri, cnt.astype(BF16), preferred_element_type=F32)

    def align_down(v):
        return jnp.floor(v * (1.0 / RUN_ALIGN)) * RUN_ALIGN

    count = jnp.broadcast_to(jnp.sum(cnt, axis=0, keepdims=True), (SUBLANES, LANES))
    before = base_ref[...]
    phase = before - align_down(before)
    end = phase + count
    run_len = jnp.where(count > 0, align_down(end + (RUN_ALIGN - 1.0)), 0.0)
    partial = jnp.where((count > 0) & (end != align_down(end)), 1.0, 0.0)
    lane8 = lax.broadcasted_iota(jnp.int32, (SUBLANES, LANES), 1)
    incl = run_len
    for sh in (1, 2, 4):
        incl = incl + jnp.where(lane8 >= sh, pltpu.roll(incl, sh, axis=1), 0.0)
    lstart = incl - run_len
    base_ref[...] = before + count
    tot_ref[...] = before + count
    row8 = lax.broadcasted_iota(jnp.int32, (SUBLANES, LANES), 0)
    table = jnp.zeros((SUBLANES, LANES), F32)
    for r, val in ((_R_LSTART, lstart), (_R_LEN, run_len), (_R_GOFF, before - phase), (_R_PARTIAL, partial)):
        table = jnp.where(row8 == r, val, table)
    runs_ref[...] = table

    slot = rank + (lstart + phase)[0:1, :]
    s0 = jnp.sum(jnp.where(oh0, slot, 0.0), axis=-1, keepdims=True)
    s1 = jnp.sum(jnp.where(oh1, slot, 0.0), axis=-1, keepdims=True)
    meta = jnp.zeros((tm, LANES), F32)
    for k, val in ((_M_E0, e0), (_M_E1, e1), (_M_S0, s0), (_M_S1, s1), (_M_G0, g0), (_M_G1, g1)):
        meta = jnp.where(lane == k, val, meta)
    meta_ref[...] = meta


def _router(x, wr_hi, wr_lo):
    n, d = x.shape
    tm, k = TM_MOE, ROUTER_TILES
    assert n % (k * tm) == 0
    nt = n // tm
    small = (SUBLANES, LANES)
    return pl.pallas_call(
        functools.partial(_router_kernel, tm=tm),
        out_shape=(jax.ShapeDtypeStruct((n, LANES), F32), jax.ShapeDtypeStruct((nt * SUBLANES, LANES), F32),
                   jax.ShapeDtypeStruct(small, F32)),
        grid=(nt // k,),
        in_specs=[pl.BlockSpec((k * tm, d), lambda i: (i, 0)), _const_spec(wr_hi.shape), _const_spec(wr_lo.shape)],
        out_specs=(pl.BlockSpec((k * tm, LANES), lambda i: (i, 0)),
                   pl.BlockSpec((k * SUBLANES, LANES), lambda i: (i, 0)), _const_spec(small)),
        scratch_shapes=[pltpu.VMEM(small, F32)],
        compiler_params=_params(("arbitrary",)),
        name="router",
    )(x, wr_hi, wr_lo)


def _run_copies(tab_ref, tile, local_ref, glob_hbm, sem, *, to_global, start):
    base = tile * (RUN_FIELDS * N_EXPERTS)
    for e in range(N_EXPERTS):
        lstart = tab_ref[base + e]
        length = tab_ref[base + N_EXPERTS + e]
        gstart = tab_ref[base + 2 * N_EXPERTS + e]
        for b in range(RUN_BITS):
            size = RUN_ALIGN << b
            off = (length >> (b + ALIGN_SHIFT + 1)) << (b + ALIGN_SHIFT + 1)

            def piece(size=size, off=off, lstart=lstart, gstart=gstart):
                loc = local_ref.at[pl.ds(pl.multiple_of(lstart + off, RUN_ALIGN), size)]
                glo = glob_hbm.at[pl.ds(pl.multiple_of(gstart + off, RUN_ALIGN), size)]
                cp = pltpu.make_async_copy(loc, glo, sem) if to_global else pltpu.make_async_copy(glo, loc, sem)
                if start:
                    cp.start()
                else:
                    cp.wait()

            pl.when(((length >> (b + ALIGN_SHIFT)) & 1) == 1)(piece)


def _fill_copies(fill_ref, zero_ref, glob_hbm, sem, *, te, n_spare, start):
    def go(cp):
        if start:
            cp.start()
        else:
            cp.wait()

    for e in range(N_EXPERTS):
        tstart = fill_ref[e]
        length = fill_ref[N_EXPERTS + e]
        for b in range(TAIL_BITS):
            size = RUN_ALIGN << b
            off = (length >> (b + ALIGN_SHIFT + 1)) << (b + ALIGN_SHIFT + 1)

            def piece(size=size, off=off, tstart=tstart):
                go(pltpu.make_async_copy(zero_ref.at[pl.ds(0, size)],
                                         glob_hbm.at[pl.ds(pl.multiple_of(tstart + off, RUN_ALIGN), size)], sem))

            pl.when(((length >> (b + ALIGN_SHIFT)) & 1) == 1)(piece)
    spare0 = fill_ref[2 * N_EXPERTS]
    for j in range(n_spare):
        def tile_fill(j=j):
            go(pltpu.make_async_copy(zero_ref, glob_hbm.at[pl.ds(pl.multiple_of(spare0 + j * te, te), te)], sem))

        pl.when(j < fill_ref[2 * N_EXPERTS + 1])(tile_fill)


def _sort_tile(rows, x_ref, meta_ref, loc_ref, *, tm):
    d = x_ref.shape[1]
    meta = meta_ref[rows, :]
    meta_t = meta.T
    s0 = meta_t[_M_S0:_M_S0 + 1, :]
    s1 = meta_t[_M_S1:_M_S1 + 1, :]
    slot = lax.broadcasted_iota(jnp.int32, (SLOTS, tm), 0).astype(F32)
    p0 = slot == s0
    p1 = slot == s1
    onehot = jnp.where(p0 | p1, 1.0, 0.0).astype(BF16)
    loc_ref[:, 0:d] = jnp.dot(onehot, x_ref[rows, :].astype(BF16), preferred_element_type=F32).astype(BF16)

    lane = lax.broadcasted_iota(jnp.int32, (tm, GATE_LANES), 1)

    def pieces(g):
        h = g.astype(BF16).astype(F32)
        r1 = g - h
        m = r1.astype(BF16).astype(F32)
        l = (r1 - m).astype(BF16).astype(F32)
        return jnp.where(lane == 0, h, jnp.where(lane == 1, m, jnp.where(lane == 2, l, 0.0))).astype(BF16)

    gate = (jnp.dot(jnp.where(p0, 1.0, 0.0).astype(BF16), pieces(meta[:, _M_G0:_M_G0 + 1]),
                    preferred_element_type=F32)
            + jnp.dot(jnp.where(p1, 1.0, 0.0).astype(BF16), pieces(meta[:, _M_G1:_M_G1 + 1]),
                      preferred_element_type=F32))
    loc_ref[:, d:d + GATE_LANES] = gate.astype(BF16)


def _merge_carry(tile, tab_ref, loc_ref, carry_ref):
    base = tile * (RUN_FIELDS * N_EXPERTS)
    for e in range(N_EXPERTS):
        lstart = tab_ref[base + _R_LSTART * N_EXPERTS + e]
        length = tab_ref[base + _R_LEN * N_EXPERTS + e]
        partial = tab_ref[base + _R_PARTIAL * N_EXPERTS + e]

        def merge(e=e, lstart=lstart, length=length, partial=partial):
            first = pl.ds(pl.multiple_of(lstart, RUN_ALIGN), RUN_ALIGN)
            loc_ref[first, :] = loc_ref[first, :] + carry_ref[e]
            last = pl.ds(pl.multiple_of(lstart + length - RUN_ALIGN, RUN_ALIGN), RUN_ALIGN)

            @pl.when(partial == 1)
            def _():
                carry_ref[e] = loc_ref[last, :]

            @pl.when(partial == 0)
            def _():
                carry_ref[e] = jnp.zeros(carry_ref.shape[1:], BF16)

        pl.when(length > 0)(merge)


def _dispatch_kernel(tab_ref, fill_ref, x_ref, meta_ref, xs_hbm, *scratch, tm, te, n_spare):
    *loc_refs, zero_ref, carry_ref, sem, zsem = scratch
    i = pl.program_id(0)

    @pl.when(i == 0)
    def _():
        zero_ref[...] = jnp.zeros(zero_ref.shape, BF16)
        carry_ref[...] = jnp.zeros(carry_ref.shape, BF16)
        _fill_copies(fill_ref, zero_ref, xs_hbm, zsem, te=te, n_spare=n_spare, start=True)

    n_tiles = x_ref.shape[0] // tm
    for k in range(n_tiles):
        tile = n_tiles * i + k
        _sort_tile(pl.ds(k * tm, tm), x_ref, meta_ref, loc_refs[k], tm=tm)
        if k > 0:
            _run_copies(tab_ref, tile - 1, loc_refs[k - 1], xs_hbm, sem, to_global=True, start=False)
        _merge_carry(tile, tab_ref, loc_refs[k], carry_ref)
        _run_copies(tab_ref, tile, loc_refs[k], xs_hbm, sem, to_global=True, start=True)
    _run_copies(tab_ref, n_tiles * i + n_tiles - 1, loc_refs[-1], xs_hbm, sem, to_global=True, start=False)

    @pl.when(i == pl.num_programs(0) - 1)
    def _():
        _fill_copies(fill_ref, zero_ref, xs_hbm, zsem, te=te, n_spare=n_spare, start=False)


def _dispatch(x, meta, run_tab, fill_tab, n_rows, n_spare):
    n, d = x.shape
    tm, te, k = TM_MOE, TM_EXPERT, MOE_TILES
    width = d + GATE_LANES
    grid_spec = pltpu.PrefetchScalarGridSpec(
        num_scalar_prefetch=2,
        grid=(n // (k * tm),),
        in_specs=[pl.BlockSpec((k * tm, d), lambda i, *_: (i, 0)),
                  pl.BlockSpec((k * tm, LANES), lambda i, *_: (i, 0))],
        out_specs=pl.BlockSpec(memory_space=pl.ANY),
        scratch_shapes=[pltpu.VMEM((SLOTS, width), BF16)] * k + [pltpu.VMEM((te, width), BF16),
                        pltpu.VMEM((N_EXPERTS, RUN_ALIGN, width), BF16),
                        pltpu.SemaphoreType.DMA, pltpu.SemaphoreType.DMA],
    )
    return pl.pallas_call(
        functools.partial(_dispatch_kernel, tm=tm, te=te, n_spare=n_spare),
        out_shape=jax.ShapeDtypeStruct((n_rows, width), BF16),
        grid_spec=grid_spec,
        compiler_params=_params(("arbitrary",)),
        name="dispatch",
    )(run_tab, fill_tab, x, meta)


def _experts_kernel(tile_ref, expert_ref, rows_ref, xs_ref, wg_ref, wu_ref, wd_ref, ys_ref, acc_ref, *, sub):
    i = pl.program_id(0)
    f = pl.program_id(1)
    te, d = ys_ref.shape
    rows = rows_ref[i]

    @pl.when((rows == 0) & (f == 0))
    def _():
        ys_ref[...] = jnp.zeros(ys_ref.shape, ys_ref.dtype)

    def partial_out(r0):
        xb = xs_ref[r0:r0 + sub, 0:d]
        g = jnp.dot(xb, wg_ref[0], preferred_element_type=F32)
        u = jnp.dot(xb, wu_ref[0], preferred_element_type=F32)
        h = (_silu(g) * u).astype(BF16)
        return jnp.dot(h, wd_ref[0], preferred_element_type=F32)

    def first_chunk(n_rows):
        for r0 in range(0, n_rows, sub):
            acc_ref[r0:r0 + sub, :] = partial_out(r0)

    def last_chunk(n_rows):
        for r0 in range(0, n_rows, sub):
            gate = jnp.sum(xs_ref[r0:r0 + sub, d:d + GATE_LANES].astype(F32), axis=-1, keepdims=True)
            ys_ref[r0:r0 + sub, :] = ((acc_ref[r0:r0 + sub, :] + partial_out(r0)) * gate).astype(ys_ref.dtype)
        if n_rows < te:
            ys_ref[n_rows:te, :] = jnp.zeros((te - n_rows, d), ys_ref.dtype)

    for nb in range(1, te // sub + 1):
        fits = (rows > (nb - 1) * sub) & (rows <= nb * sub)
        pl.when(fits & (f == 0))(functools.partial(first_chunk, nb * sub))
        pl.when(fits & (f == 1))(functools.partial(last_chunk, nb * sub))


def _experts(xs, we_gate, we_up, we_down, tile_idx, tile_expert, tile_rows):
    n_rows, width = xs.shape
    d = width - GATE_LANES
    te = TM_EXPERT
    ff = we_gate.shape[-1]
    fc = ff // FF_CHUNKS
    assert n_rows % te == 0 and ff % FF_CHUNKS == 0 and fc % LANES == 0 and FF_CHUNKS == 2
    n_tiles = n_rows // te
    last = FF_CHUNKS - 1
    grid_spec = pltpu.PrefetchScalarGridSpec(
        num_scalar_prefetch=3,
        grid=(n_tiles, FF_CHUNKS),
        in_specs=[
            pl.BlockSpec((te, width), lambda i, f, tile, ex, rows: (tile[i], 0)),
            pl.BlockSpec((1, d, fc), lambda i, f, tile, ex, rows: (ex[i], 0, jnp.where(rows[i] > 0, f, last))),
            pl.BlockSpec((1, d, fc), lambda i, f, tile, ex, rows: (ex[i], 0, jnp.where(rows[i] > 0, f, last))),
            pl.BlockSpec((1, fc, d), lambda i, f, tile, ex, rows: (ex[i], jnp.where(rows[i] > 0, f, last), 0)),
        ],
        out_specs=pl.BlockSpec((te, d), lambda i, f, tile, ex, rows: (i, 0)),
        scratch_shapes=[pltpu.VMEM((te, d), F32)],
    )
    return pl.pallas_call(
        functools.partial(_experts_kernel, sub=SUB_EXPERT),
        out_shape=jax.ShapeDtypeStruct((n_rows, d), BF16),
        grid_spec=grid_spec,
        compiler_params=_params(("arbitrary", "arbitrary")),
        name="experts",
    )(tile_idx, tile_expert, tile_rows, xs, we_gate, we_up, we_down)


def _combine_kernel(tab_ref, x_ref, meta_ref, lng_ref, lnb_ref, ys_hbm, o_ref, loc_ref, sem, *, tm, alpha):
    i = pl.program_id(0)
    n_steps = pl.num_programs(0)
    n_tiles = x_ref.shape[0] // tm
    slot = lax.rem(i, 2)

    def copies(step, s, start):
        for k in range(n_tiles):
            _run_copies(tab_ref, n_tiles * step + k, loc_ref.at[s, k], ys_hbm, sem.at[s], to_global=False,
                        start=start)

    @pl.when(i == 0)
    def _():
        loc_ref[...] = jnp.zeros(loc_ref.shape, loc_ref.dtype)
        copies(0, 0, True)

    @pl.when(i + 1 < n_steps)
    def _():
        copies(i + 1, 1 - slot, True)

    copies(i, slot, False)

    lane = lax.broadcasted_iota(jnp.int32, (tm, SLOTS), 1).astype(F32)
    for k in range(n_tiles):
        rows = pl.ds(k * tm, tm)
        meta = meta_ref[rows, :]
        s0 = meta[:, _M_S0:_M_S0 + 1]
        s1 = meta[:, _M_S1:_M_S1 + 1]
        pick = jnp.where((lane == s0) | (lane == s1), 1.0, 0.0).astype(BF16)
        moe = jnp.dot(pick, loc_ref[slot, k], preferred_element_type=F32)
        o_ref[rows, :] = _layer_norm(alpha * x_ref[rows, :] + moe, lng_ref[...], lnb_ref[...])


def _combine_ln(x, meta, run_tab, ys, ln_g, ln_b, *, alpha):
    n, d = x.shape
    tm, k = TM_MOE, MOE_TILES
    n_steps = n // (k * tm)
    row = pl.BlockSpec((k * tm, d), lambda i, *_: (i, 0))
    grid_spec = pltpu.PrefetchScalarGridSpec(
        num_scalar_prefetch=1,
        grid=(n_steps,),
        in_specs=[row, pl.BlockSpec((k * tm, LANES), lambda i, *_: (i, 0)),
                  pl.BlockSpec(ln_g.shape, lambda i, *_: (0, 0)), pl.BlockSpec(ln_b.shape, lambda i, *_: (0, 0)),
                  pl.BlockSpec(memory_space=pl.ANY)],
        out_specs=row,
        scratch_shapes=[pltpu.VMEM((2, k, SLOTS, d), BF16), pltpu.SemaphoreType.DMA((2,))],
    )
    return pl.pallas_call(
        functools.partial(_combine_kernel, tm=tm, alpha=alpha),
        out_shape=jax.ShapeDtypeStruct((n, d), F32),
        grid_spec=grid_spec,
        compiler_params=_params(("arbitrary",)),
        name="combine_ln",
    )(run_tab, x, meta, ln_g, ln_b, ys)


def _moe_ln(x, w_router, we_gate, we_up, we_down, ln_g, ln_b, *, alpha):
    n, d = x.shape
    tm, te = TM_MOE, TM_EXPERT
    assert n % (MOE_TILES * tm) == 0 and te % SUB_EXPERT == 0
    nt = n // tm
    wr = jnp.pad(w_router, ((0, 0), (0, LANES - N_EXPERTS)))
    wr_hi = wr.astype(BF16)
    wr_lo = (wr - wr_hi.astype(F32)).astype(BF16)
    meta, runs, totals = _router(x, wr_hi, wr_lo)

    counts = totals[0, :N_EXPERTS].astype(jnp.int32)
    padded = (counts + RUN_ALIGN - 1) // RUN_ALIGN * RUN_ALIGN
    tiles_e = (padded + te - 1) // te
    tile_end = jnp.cumsum(tiles_e)
    region = (tile_end - tiles_e) * te
    n_used = tile_end[-1]
    max_rows = TOP_K * n + N_EXPERTS * (te - 1)
    n_tiles = -(-max_rows // te)
    n_spare = n_tiles - (TOP_K * n) // te
    runs_i = runs.reshape(nt, SUBLANES, LANES)[:, :RUN_FIELDS, :N_EXPERTS].astype(jnp.int32)
    runs_i = runs_i.at[:, _R_GOFF, :].add(region)
    run_tab = runs_i.reshape(-1)
    fill_tab = jnp.concatenate([region + padded, tiles_e * te - padded,
                                (n_used * te)[None], (n_tiles - n_used)[None]]).astype(jnp.int32)
    tiles = jnp.arange(n_tiles, dtype=jnp.int32)
    j = jnp.minimum(tiles, n_used - 1)
    tile_expert = jnp.minimum(jnp.sum(j[:, None] >= tile_end[None, :], axis=-1), N_EXPERTS - 1).astype(jnp.int32)
    region_end = jnp.sum(jnp.where(tile_expert[:, None] == jnp.arange(N_EXPERTS), region + padded, 0), axis=-1)
    tile_rows = jnp.where(tiles < n_used, jnp.minimum(region_end - tiles * te, te), 0).astype(jnp.int32)

    xs = _dispatch(x, meta, run_tab, fill_tab, n_tiles * te, n_spare)
    ys = _experts(xs, we_gate, we_up, we_down, j, tile_expert, tile_rows)
    return _combine_ln(x, meta, run_tab, ys, ln_g, ln_b, alpha=alpha)


def kernel(x, w_in, conv_w, v_g, v_b, w_s, b_s, out_g, w_out, ln1_g, ln1_b, ln2_g, ln2_b,
           w_gate, w_up, w_down, w_router, we_gate, we_up, we_down):
    bsz, seq, d = x.shape
    depth = w_in.shape[0]
    alpha = float((2 * depth) ** 0.25)
    h = x.reshape(bsz * seq, d)
    row = lambda a: a.reshape(1, -1)
    mix_w = [w_in[0].astype(BF16), w_out[0].astype(BF16)]
    moe_w = [None, None, None]
    for i in range(depth):
        j = i // 2
        dense, last = i % 2 == 0, i + 1 == depth
        if dense:
            cast = [w_gate[j], w_up[j], w_down[j]] + ([] if last else [we_down[j]])
        else:
            cast = [we_up[j]]
        bias_t = jnp.repeat(b_s[i].T, GMLP_HEAD_DIM, axis=1)
        h, cast_w = _mixer_ln(h, mix_w[0], conv_w[i], row(v_g[i]), row(v_b[i]), w_s[i], bias_t,
                              row(out_g[i]), mix_w[1], row(ln1_g[i]), row(ln1_b[i]), cast=cast,
                              seq=seq, alpha=alpha)
        if dense:
            ahead = [] if last else [(w_in, i + 1), (w_out, i + 1), we_gate[j]]
            h, ahead_w = _ffn_ln(h, *cast_w[:3], row(ln2_g[i]), row(ln2_b[i]), cast=ahead, alpha=alpha)
            if not last:
                mix_w, moe_w = ahead_w[:2], [ahead_w[2], None, cast_w[3]]
        else:
            moe_w[1] = cast_w[0]
            h = _moe_ln(h, w_router[j], *moe_w, row(ln2_g[i]), row(ln2_b[i]), alpha=alpha)
            moe_w = [None, None, None]
            if not last:
                mix_w = [w_in[i + 1].astype(BF16), w_out[i + 1].astype(BF16)]
    return h.reshape(bsz, seq, d)
```

```python
import functools

import numpy as np
import jax
import jax.numpy as jnp
from jax import lax
from jax.experimental import pallas as pl
from jax.experimental.pallas import tpu as pltpu

F32 = jnp.float32
BF16 = jnp.bfloat16

CONV_WIDTH = 512
CONV_HEAD_DIM = 64
GMLP_WIDTH = 512
GMLP_HEADS = 4
GMLP_HEAD_DIM = 128
CHUNK = 128
N_EXPERTS = 8
TOP_K = 2
LN_EPS = 1e-5
RMS_EPS = 1e-6

LANES = 128
SUBLANES = 8
BF16_ROWS = 16
VMEM_LIMIT_BYTES = 56 * 1024 * 1024

TM_MIX = 1024
SUB_MIX = 512
TM_FFN = 1024
SUB_FFN = 512
TM_MOE = 512
MOE_TILES = 2
ROUTER_TILES = 1
TM_EXPERT = 1024
SUB_EXPERT = 512
FF_CHUNKS = 2

RUN_ALIGN = BF16_ROWS
ALIGN_SHIFT = RUN_ALIGN.bit_length() - 1
MAX_RUN = TM_MOE + RUN_ALIGN
RUN_BITS = (MAX_RUN // RUN_ALIGN).bit_length()
SLOTS = -(-(TOP_K * TM_MOE + N_EXPERTS * 2 * (RUN_ALIGN - 1)) // LANES) * LANES
GATE_LANES = LANES
TAIL_BITS = (TM_EXPERT // RUN_ALIGN - 1).bit_length()


def _layer_norm(r, g, b):
    mu = jnp.mean(r, axis=-1, keepdims=True)
    d = r - mu
    var = jnp.mean(d * d, axis=-1, keepdims=True)
    return d * lax.rsqrt(var + LN_EPS) * g + b


def _gelu(x):
    return 0.5 * x * (1.0 + lax.erf(x * (2.0 ** -0.5)))


def _silu(x):
    return x * (1.0 / (1.0 + jnp.exp(-x)))


def _params(semantics):
    return pltpu.CompilerParams(dimension_semantics=semantics, vmem_limit_bytes=VMEM_LIMIT_BYTES)


def _const_spec(shape):
    nd = len(shape)
    return pl.BlockSpec(shape, lambda *_: (0,) * nd)


def _cast_plan(weights, n_steps):
    views, in_specs, out_specs, out_shapes, shapes = [], [], [], [], []
    for w in weights:
        a, layer = w if isinstance(w, tuple) else (w, None)
        shape = a.shape if layer is None else a.shape[1:]
        rows, cols = int(np.prod(shape[:-1])), shape[-1]
        n_blocks = next(nb for nb in range(n_steps, 0, -1)
                        if n_steps % nb == 0 and rows % (nb * BF16_ROWS) == 0)
        r, first = n_steps // n_blocks, 0 if layer is None else layer * n_blocks
        views.append(a.reshape(-1, cols))
        in_specs.append(pl.BlockSpec((rows // n_blocks, cols), lambda i, r=r, first=first: (first + i // r, 0)))
        out_specs.append(pl.BlockSpec((rows // n_blocks, cols), lambda i, r=r: (i // r, 0)))
        out_shapes.append(jax.ShapeDtypeStruct((rows, cols), BF16))
        shapes.append(shape)
    return views, in_specs, out_specs, out_shapes, shapes


def _cast_along(src_refs, dst_refs):
    for src, dst in zip(src_refs, dst_refs):
        dst[...] = src[...].astype(dst.dtype)


def _mixer_kernel(*refs, n_cast, tm, sub, seq, alpha):
    (x_ref, win_ref, convw_ref, vg_ref, vb_ref, ws_ref, bias_ref, outg_ref, wout_ref,
     lng_ref, lnb_ref) = refs[:11]
    o_ref, ch_ref = refs[11 + n_cast], refs[-1]
    _cast_along(refs[11:11 + n_cast], refs[12 + n_cast:12 + 2 * n_cast])
    i = pl.program_id(0)
    cw_, gw_ = CONV_WIDTH, GMLP_WIDTH

    @pl.when(lax.rem(i * tm, seq) == 0)
    def _():
        ch_ref[0:SUBLANES, :] = jnp.zeros((SUBLANES, cw_), F32)

    trow = lax.broadcasted_iota(jnp.int32, (CHUNK, CHUNK), 0)
    tcol = lax.broadcasted_iota(jnp.int32, (CHUNK, CHUNK), 1)
    causal = tcol <= trow
    ws = [jnp.where(causal, ws_ref[h], 0.0).astype(BF16) for h in range(GMLP_HEADS)]
    lane = lax.broadcasted_iota(jnp.int32, (sub, LANES), 1)
    low_half = lane < CONV_HEAD_DIM
    nch = sub // CHUNK

    for r0 in range(0, tm, sub):
        x = x_ref[r0:r0 + sub, :]
        proj = jnp.dot(x.astype(BF16), win_ref[...], preferred_element_type=F32)
        b_gate = proj[:, 0:cw_]
        c_gate = proj[:, cw_:2 * cw_]
        hh = proj[:, 2 * cw_:3 * cw_]
        u = proj[:, 3 * cw_:3 * cw_ + gw_]
        v = proj[:, 3 * cw_ + gw_:3 * cw_ + 2 * gw_]

        c0 = SUBLANES + r0
        ch_ref[c0:c0 + sub, :] = c_gate * hh
        convw = convw_ref[...]
        conv = (convw[0:1, :] * ch_ref[c0 - 2:c0 - 2 + sub, :]
                + convw[1:2, :] * ch_ref[c0 - 1:c0 - 1 + sub, :]
                + convw[2:3, :] * ch_ref[c0:c0 + sub, :])
        y_conv = b_gate * conv

        ug = _gelu(u)
        vn = _layer_norm(_gelu(v), vg_ref[...], vb_ref[...]).astype(BF16)
        zs = []
        for h in range(GMLP_HEADS):
            lo, hi = h * GMLP_HEAD_DIM, (h + 1) * GMLP_HEAD_DIM
            rhs = jnp.concatenate([vn[c * CHUNK:(c + 1) * CHUNK, lo:hi] for c in range(nch)], axis=1)
            zs.append(jnp.dot(ws[h], rhs, preferred_element_type=F32))
        z = jnp.concatenate(
            [jnp.concatenate([zs[h][:, c * GMLP_HEAD_DIM:(c + 1) * GMLP_HEAD_DIM]
                              for h in range(GMLP_HEADS)], axis=1) for c in range(nch)], axis=0)
        bias = jnp.concatenate([bias_ref[...]] * nch, axis=0)
        y_sg = ug * (z + bias)

        parts = []
        for j in range(cw_ // LANES):
            yt = y_conv[:, j * LANES:(j + 1) * LANES]
            sq = yt * yt
            ms_lo = jnp.sum(jnp.where(low_half, sq, 0.0), axis=-1, keepdims=True) * (1.0 / CONV_HEAD_DIM)
            ms_hi = jnp.sum(jnp.where(low_half, 0.0, sq), axis=-1, keepdims=True) * (1.0 / CONV_HEAD_DIM)
            parts.append(yt * jnp.where(low_half, lax.rsqrt(ms_lo + RMS_EPS), lax.rsqrt(ms_hi + RMS_EPS)))
        for h in range(GMLP_HEADS):
            yt = y_sg[:, h * GMLP_HEAD_DIM:(h + 1) * GMLP_HEAD_DIM]
            ms = jnp.mean(yt * yt, axis=-1, keepdims=True)
            parts.append(yt * lax.rsqrt(ms + RMS_EPS))
        y = jnp.concatenate(parts, axis=1) * outg_ref[...]

        mix = jnp.dot(y.astype(BF16), wout_ref[...], preferred_element_type=F32)
        o_ref[r0:r0 + sub, :] = _layer_norm(alpha * x + mix, lng_ref[...], lnb_ref[...])

    ch_ref[0:SUBLANES, :] = ch_ref[tm:tm + SUBLANES, :]


def _mixer_ln(x, w_in, conv_w, v_g, v_b, w_s, bias_t, out_g, w_out, ln_g, ln_b, cast=(), *, seq, alpha):
    n, d = x.shape
    tm, sub = TM_MIX, SUB_MIX
    assert n % tm == 0 and seq % tm == 0 and tm % sub == 0 and sub % CHUNK == 0
    views, cast_in, cast_out, cast_shapes, shapes = _cast_plan(cast, n // tm)
    kern = functools.partial(_mixer_kernel, n_cast=len(cast), tm=tm, sub=sub, seq=seq, alpha=alpha)
    row = pl.BlockSpec((tm, d), lambda i: (i, 0))
    out = pl.pallas_call(
        kern,
        out_shape=[jax.ShapeDtypeStruct((n, d), F32)] + cast_shapes,
        grid=(n // tm,),
        in_specs=[row, _const_spec(w_in.shape), _const_spec(conv_w.shape), _const_spec(v_g.shape),
                  _const_spec(v_b.shape), _const_spec(w_s.shape), _const_spec(bias_t.shape),
                  _const_spec(out_g.shape), _const_spec(w_out.shape), _const_spec(ln_g.shape),
                  _const_spec(ln_b.shape)] + cast_in,
        out_specs=[row] + cast_out,
        scratch_shapes=[pltpu.VMEM((tm + SUBLANES, CONV_WIDTH), F32)],
        compiler_params=_params(("arbitrary",)),
        name="mixer_ln",
    )(x, w_in, conv_w, v_g, v_b, w_s, bias_t, out_g, w_out, ln_g, ln_b, *views)
    return out[0], [o.reshape(shape) for o, shape in zip(out[1:], shapes)]


def _ffn_kernel(*refs, n_cast, sub, alpha):
    x_ref, wg_ref, wu_ref, wd_ref, lng_ref, lnb_ref = refs[:6]
    o_ref = refs[6 + n_cast]
    _cast_along(refs[6:6 + n_cast], refs[7 + n_cast:7 + 2 * n_cast])
    for r0 in range(0, x_ref.shape[0], sub):
        x = x_ref[r0:r0 + sub, :]
        xb = x.astype(BF16)
        g = jnp.dot(xb, wg_ref[...], preferred_element_type=F32)
        u = jnp.dot(xb, wu_ref[...], preferred_element_type=F32)
        h = (_silu(g) * u).astype(BF16)
        ffn = jnp.dot(h, wd_ref[...], preferred_element_type=F32)
        o_ref[r0:r0 + sub, :] = _layer_norm(alpha * x + ffn, lng_ref[...], lnb_ref[...])


def _ffn_ln(x, wg, wu, wd, ln_g, ln_b, cast=(), *, alpha):
    n, d = x.shape
    tm = TM_FFN
    assert n % tm == 0 and tm % SUB_FFN == 0
    views, cast_in, cast_out, cast_shapes, shapes = _cast_plan(cast, n // tm)
    row = pl.BlockSpec((tm, d), lambda i: (i, 0))
    out = pl.pallas_call(
        functools.partial(_ffn_kernel, n_cast=len(cast), sub=SUB_FFN, alpha=alpha),
        out_shape=[jax.ShapeDtypeStruct((n, d), F32)] + cast_shapes,
        grid=(n // tm,),
        in_specs=[row, _const_spec(wg.shape), _const_spec(wu.shape), _const_spec(wd.shape),
                  _const_spec(ln_g.shape), _const_spec(ln_b.shape)] + cast_in,
        out_specs=[row] + cast_out,
        compiler_params=_params(("arbitrary",)),
        name="ffn_ln",
    )(x, wg, wu, wd, ln_g, ln_b, *views)
    return out[0], [o.reshape(shape) for o, shape in zip(out[1:], shapes)]


_M_E0, _M_E1, _M_S0, _M_S1, _M_G0, _M_G1 = range(6)
_R_LSTART, _R_LEN, _R_GOFF, _R_PARTIAL = range(4)
RUN_FIELDS = 4
EXPERT_ROWS = max(N_EXPERTS, BF16_ROWS)


def _router_kernel(x_ref, wrh_ref, wrl_ref, meta_ref, runs_ref, tot_ref, base_ref, *, tm):
    i = pl.program_id(0)

    @pl.when(i == 0)
    def _():
        base_ref[...] = jnp.zeros(base_ref.shape, F32)

    for k in range(x_ref.shape[0] // tm):
        _route_tile(x_ref[k * tm:(k + 1) * tm, :], wrh_ref, wrl_ref, meta_ref.at[k * tm:(k + 1) * tm],
                    runs_ref.at[k * EXPERT_ROWS:(k + 1) * EXPERT_ROWS], tot_ref, base_ref)


def _route_tile(x, wrh_ref, wrl_ref, meta_ref, runs_ref, tot_ref, base_ref):
    tm = x.shape[0]
    xh = x.astype(BF16)
    xl = (x - xh.astype(F32)).astype(BF16)
    wrh = wrh_ref[...]
    logits = (jnp.dot(xh, wrh, preferred_element_type=F32)
              + jnp.dot(xl, wrh, preferred_element_type=F32)
              + jnp.dot(xh, wrl_ref[...], preferred_element_type=F32))

    er = EXPERT_ROWS
    lg = logits.T[0:er, :]
    ef = lax.broadcasted_iota(jnp.int32, (er, tm), 0).astype(F32)
    neg = jnp.float32(-jnp.inf)
    lg = jnp.where(ef < N_EXPERTS, lg, neg)
    m0 = jnp.max(lg, axis=0, keepdims=True)
    e0 = jnp.min(jnp.where(lg == m0, ef, float(er)), axis=0, keepdims=True)
    lg1 = jnp.where(ef == e0, neg, lg)
    m1 = jnp.max(lg1, axis=0, keepdims=True)
    e1 = jnp.min(jnp.where(lg1 == m1, ef, float(er)), axis=0, keepdims=True)
    t = jnp.exp(m1 - m0)
    g0 = 1.0 / (1.0 + t)
    g1 = t / (1.0 + t)

    oh0 = ef == e0
    oh1 = ef == e1
    cnt = jnp.where(oh0 | oh1, 1.0, 0.0)
    s_i = lax.broadcasted_iota(jnp.int32, (tm, tm), 0)
    t_i = lax.broadcasted_iota(jnp.int32, (tm, tm), 1)
    earlier = jnp.where(s_i < t_i, 1.0, 0.0).astype(BF16)
    rank = jnp.dot(cnt.astype(BF16), earlier, preferred_element_type=F32)

    def align_down(v):
        return jnp.floor(v * (1.0 / RUN_ALIGN)) * RUN_ALIGN

    count = jnp.broadcast_to(jnp.sum(cnt, axis=1, keepdims=True), (er, LANES))
    before = base_ref[...]
    phase = before - align_down(before)
    end = phase + count
    run_len = jnp.where(count > 0, align_down(end + (RUN_ALIGN - 1.0)), 0.0)
    partial = jnp.where((count > 0) & (end != align_down(end)), 1.0, 0.0)
    r_e = lax.broadcasted_iota(jnp.int32, (er, er), 0)
    c_e = lax.broadcasted_iota(jnp.int32, (er, er), 1)
    lower = jnp.where(c_e < r_e, 1.0, 0.0).astype(BF16)
    lstart = jnp.dot(lower, run_len.astype(BF16), preferred_element_type=F32)
    base_ref[...] = before + count
    tot_ref[...] = before + count
    field = lax.broadcasted_iota(jnp.int32, (er, LANES), 1)
    table = jnp.zeros((er, LANES), F32)
    for f, val in ((_R_LSTART, lstart), (_R_LEN, run_len), (_R_GOFF, before - phase), (_R_PARTIAL, partial)):
        table = jnp.where(field == f, val, table)
    runs_ref[...] = table

    slot = rank + (lstart + phase)[:, 0:1]
    s0 = jnp.sum(jnp.where(oh0, slot, 0.0), axis=0, keepdims=True)
    s1 = jnp.sum(jnp.where(oh1, slot, 0.0), axis=0, keepdims=True)
    row = lax.broadcasted_iota(jnp.int32, (SUBLANES, tm), 0)
    meta_t = jnp.zeros((SUBLANES, tm), F32)
    for k, val in ((_M_E0, e0), (_M_E1, e1), (_M_S0, s0), (_M_S1, s1), (_M_G0, g0), (_M_G1, g1)):
        meta_t = jnp.where(row == k, val, meta_t)
    meta_ref[...] = jnp.concatenate([meta_t, jnp.zeros((LANES - SUBLANES, tm), F32)], axis=0).T


def _router(x, wr_hi, wr_lo):
    n, d = x.shape
    tm, k = TM_MOE, ROUTER_TILES
    assert n % (k * tm) == 0
    nt = n // tm
    small = (EXPERT_ROWS, LANES)
    return pl.pallas_call(
        functools.partial(_router_kernel, tm=tm),
        out_shape=(jax.ShapeDtypeStruct((n, LANES), F32), jax.ShapeDtypeStruct((nt * EXPERT_ROWS, LANES), F32),
                   jax.ShapeDtypeStruct(small, F32)),
        grid=(nt // k,),
        in_specs=[pl.BlockSpec((k * tm, d), lambda i: (i, 0)), _const_spec(wr_hi.shape), _const_spec(wr_lo.shape)],
        out_specs=(pl.BlockSpec((k * tm, LANES), lambda i: (i, 0)),
                   pl.BlockSpec((k * EXPERT_ROWS, LANES), lambda i: (i, 0)), _const_spec(small)),
        scratch_shapes=[pltpu.VMEM(small, F32)],
        compiler_params=_params(("arbitrary",)),
        name="router",
    )(x, wr_hi, wr_lo)


def _run_copies(tab_ref, tile, local_ref, glob_hbm, sem, *, to_global, start):
    base = tile * (RUN_FIELDS * N_EXPERTS)
    for e in range(N_EXPERTS):
        lstart = tab_ref[base + e]
        length = tab_ref[base + N_EXPERTS + e]
        gstart = tab_ref[base + 2 * N_EXPERTS + e]
        for b in range(RUN_BITS):
            size = RUN_ALIGN << b
            off = (length >> (b + ALIGN_SHIFT + 1)) << (b + ALIGN_SHIFT + 1)

            def piece(size=size, off=off, lstart=lstart, gstart=gstart):
                loc = local_ref.at[pl.ds(pl.multiple_of(lstart + off, RUN_ALIGN), size)]
                glo = glob_hbm.at[pl.ds(pl.multiple_of(gstart + off, RUN_ALIGN), size)]
                cp = pltpu.make_async_copy(loc, glo, sem) if to_global else pltpu.make_async_copy(glo, loc, sem)
                if start:
                    cp.start()
                else:
                    cp.wait()

            pl.when(((length >> (b + ALIGN_SHIFT)) & 1) == 1)(piece)


def _fill_copies(fill_ref, zero_ref, glob_hbm, sem, *, te, n_spare, start):
    def go(cp):
        if start:
            cp.start()
        else:
            cp.wait()

    for e in range(N_EXPERTS):
        tstart = fill_ref[e]
        length = fill_ref[N_EXPERTS + e]
        for b in range(TAIL_BITS):
            size = RUN_ALIGN << b
            off = (length >> (b + ALIGN_SHIFT + 1)) << (b + ALIGN_SHIFT + 1)

            def piece(size=size, off=off, tstart=tstart):
                go(pltpu.make_async_copy(zero_ref.at[pl.ds(0, size)],
                                         glob_hbm.at[pl.ds(pl.multiple_of(tstart + off, RUN_ALIGN), size)], sem))

            pl.when(((length >> (b + ALIGN_SHIFT)) & 1) == 1)(piece)
    spare0 = fill_ref[2 * N_EXPERTS]
    for j in range(n_spare):
        def tile_fill(j=j):
            go(pltpu.make_async_copy(zero_ref, glob_hbm.at[pl.ds(pl.multiple_of(spare0 + j * te, te), te)], sem))

        pl.when(j < fill_ref[2 * N_EXPERTS + 1])(tile_fill)


def _sort_tile(rows, x_ref, meta_ref, loc_ref, *, tm):
    d = x_ref.shape[1]
    meta = meta_ref[rows, :]
    meta_t = meta.T
    s0 = meta_t[_M_S0:_M_S0 + 1, :]
    s1 = meta_t[_M_S1:_M_S1 + 1, :]
    slot = lax.broadcasted_iota(jnp.int32, (SLOTS, tm), 0).astype(F32)
    p0 = slot == s0
    p1 = slot == s1
    onehot = jnp.where(p0 | p1, 1.0, 0.0).astype(BF16)
    loc_ref[:, 0:d] = jnp.dot(onehot, x_ref[rows, :].astype(BF16), preferred_element_type=F32).astype(BF16)

    lane = lax.broadcasted_iota(jnp.int32, (tm, GATE_LANES), 1)

    def pieces(g):
        h = g.astype(BF16).astype(F32)
        r1 = g - h
        m = r1.astype(BF16).astype(F32)
        l = (r1 - m).astype(BF16).astype(F32)
        return jnp.where(lane == 0, h, jnp.where(lane == 1, m, jnp.where(lane == 2, l, 0.0))).astype(BF16)

    gate = (jnp.dot(jnp.where(p0, 1.0, 0.0).astype(BF16), pieces(meta[:, _M_G0:_M_G0 + 1]),
                    preferred_element_type=F32)
            + jnp.dot(jnp.where(p1, 1.0, 0.0).astype(BF16), pieces(meta[:, _M_G1:_M_G1 + 1]),
                      preferred_element_type=F32))
    loc_ref[:, d:d + GATE_LANES] = gate.astype(BF16)


def _merge_carry(tile, tab_ref, loc_ref, carry_ref):
    base = tile * (RUN_FIELDS * N_EXPERTS)
    for e in range(N_EXPERTS):
        lstart = tab_ref[base + _R_LSTART * N_EXPERTS + e]
        length = tab_ref[base + _R_LEN * N_EXPERTS + e]
        partial = tab_ref[base + _R_PARTIAL * N_EXPERTS + e]

        def merge(e=e, lstart=lstart, length=length, partial=partial):
            first = pl.ds(pl.multiple_of(lstart, RUN_ALIGN), RUN_ALIGN)
            loc_ref[first, :] = loc_ref[first, :] + carry_ref[e]
            last = pl.ds(pl.multiple_of(lstart + length - RUN_ALIGN, RUN_ALIGN), RUN_ALIGN)

            @pl.when(partial == 1)
            def _():
                carry_ref[e] = loc_ref[last, :]

            @pl.when(partial == 0)
            def _():
                carry_ref[e] = jnp.zeros(carry_ref.shape[1:], BF16)

        pl.when(length > 0)(merge)


def _dispatch_kernel(tab_ref, fill_ref, x_ref, meta_ref, xs_hbm, *scratch, tm, te, n_spare):
    *loc_refs, zero_ref, carry_ref, sem, zsem = scratch
    i = pl.program_id(0)

    @pl.when(i == 0)
    def _():
        zero_ref[...] = jnp.zeros(zero_ref.shape, BF16)
        carry_ref[...] = jnp.zeros(carry_ref.shape, BF16)
        _fill_copies(fill_ref, zero_ref, xs_hbm, zsem, te=te, n_spare=n_spare, start=True)

    n_tiles = x_ref.shape[0] // tm
    for k in range(n_tiles):
        tile = n_tiles * i + k
        _sort_tile(pl.ds(k * tm, tm), x_ref, meta_ref, loc_refs[k], tm=tm)
        if k > 0:
            _run_copies(tab_ref, tile - 1, loc_refs[k - 1], xs_hbm, sem, to_global=True, start=False)
        _merge_carry(tile, tab_ref, loc_refs[k], carry_ref)
        _run_copies(tab_ref, tile, loc_refs[k], xs_hbm, sem, to_global=True, start=True)
    _run_copies(tab_ref, n_tiles * i + n_tiles - 1, loc_refs[-1], xs_hbm, sem, to_global=True, start=False)

    @pl.when(i == pl.num_programs(0) - 1)
    def _():
        _fill_copies(fill_ref, zero_ref, xs_hbm, zsem, te=te, n_spare=n_spare, start=False)


def _dispatch(x, meta, run_tab, fill_tab, n_rows, n_spare):
    n, d = x.shape
    tm, te, k = TM_MOE, TM_EXPERT, MOE_TILES
    width = d + GATE_LANES
    grid_spec = pltpu.PrefetchScalarGridSpec(
        num_scalar_prefetch=2,
        grid=(n // (k * tm),),
        in_specs=[pl.BlockSpec((k * tm, d), lambda i, *_: (i, 0)),
                  pl.BlockSpec((k * tm, LANES), lambda i, *_: (i, 0))],
        out_specs=pl.BlockSpec(memory_space=pl.ANY),
        scratch_shapes=[pltpu.VMEM((SLOTS, width), BF16)] * k + [pltpu.VMEM((te, width), BF16),
                        pltpu.VMEM((N_EXPERTS, RUN_ALIGN, width), BF16),
                        pltpu.SemaphoreType.DMA, pltpu.SemaphoreType.DMA],
    )
    return pl.pallas_call(
        functools.partial(_dispatch_kernel, tm=tm, te=te, n_spare=n_spare),
        out_shape=jax.ShapeDtypeStruct((n_rows, width), BF16),
        grid_spec=grid_spec,
        compiler_params=_params(("arbitrary",)),
        name="dispatch",
    )(run_tab, fill_tab, x, meta)


def _experts_kernel(tile_ref, expert_ref, rows_ref, xs_ref, wg_ref, wu_ref, wd_ref, ys_ref, acc_ref, *, sub):
    i = pl.program_id(0)
    f = pl.program_id(1)
    te, d = ys_ref.shape
    rows = rows_ref[i]

    @pl.when((rows == 0) & (f == 0))
    def _():
        ys_ref[...] = jnp.zeros(ys_ref.shape, ys_ref.dtype)

    def partial_out(r0):
        xb = xs_ref[r0:r0 + sub, 0:d]
        g = jnp.dot(xb, wg_ref[0], preferred_element_type=F32)
        u = jnp.dot(xb, wu_ref[0], preferred_element_type=F32)
        h = (_silu(g) * u).astype(BF16)
        return jnp.dot(h, wd_ref[0], preferred_element_type=F32)

    def first_chunk(n_rows):
        for r0 in range(0, n_rows, sub):
            acc_ref[r0:r0 + sub, :] = partial_out(r0)

    def last_chunk(n_rows):
        for r0 in range(0, n_rows, sub):
            gate = jnp.sum(xs_ref[r0:r0 + sub, d:d + GATE_LANES].astype(F32), axis=-1, keepdims=True)
            ys_ref[r0:r0 + sub, :] = ((acc_ref[r0:r0 + sub, :] + partial_out(r0)) * gate).astype(ys_ref.dtype)
        if n_rows < te:
            ys_ref[n_rows:te, :] = jnp.zeros((te - n_rows, d), ys_ref.dtype)

    for nb in range(1, te // sub + 1):
        fits = (rows > (nb - 1) * sub) & (rows <= nb * sub)
        pl.when(fits & (f == 0))(functools.partial(first_chunk, nb * sub))
        pl.when(fits & (f == 1))(functools.partial(last_chunk, nb * sub))


def _experts(xs, we_gate, we_up, we_down, tile_idx, tile_expert, tile_rows):
    n_rows, width = xs.shape
    d = width - GATE_LANES
    te = TM_EXPERT
    ff = we_gate.shape[-1]
    fc = ff // FF_CHUNKS
    assert n_rows % te == 0 and ff % FF_CHUNKS == 0 and fc % LANES == 0 and FF_CHUNKS == 2
    n_tiles = n_rows // te
    last = FF_CHUNKS - 1
    grid_spec = pltpu.PrefetchScalarGridSpec(
        num_scalar_prefetch=3,
        grid=(n_tiles, FF_CHUNKS),
        in_specs=[
            pl.BlockSpec((te, width), lambda i, f, tile, ex, rows: (tile[i], 0)),
            pl.BlockSpec((1, d, fc), lambda i, f, tile, ex, rows: (ex[i], 0, jnp.where(rows[i] > 0, f, last))),
            pl.BlockSpec((1, d, fc), lambda i, f, tile, ex, rows: (ex[i], 0, jnp.where(rows[i] > 0, f, last))),
            pl.BlockSpec((1, fc, d), lambda i, f, tile, ex, rows: (ex[i], jnp.where(rows[i] > 0, f, last), 0)),
        ],
        out_specs=pl.BlockSpec((te, d), lambda i, f, tile, ex, rows: (i, 0)),
        scratch_shapes=[pltpu.VMEM((te, d), F32)],
    )
    return pl.pallas_call(
        functools.partial(_experts_kernel, sub=SUB_EXPERT),
        out_shape=jax.ShapeDtypeStruct((n_rows, d), BF16),
        grid_spec=grid_spec,
        compiler_params=_params(("arbitrary", "arbitrary")),
        name="experts",
    )(tile_idx, tile_expert, tile_rows, xs, we_gate, we_up, we_down)


def _combine_kernel(tab_ref, x_ref, meta_ref, lng_ref, lnb_ref, ys_hbm, o_ref, loc_ref, sem, *, tm, alpha):
    i = pl.program_id(0)
    n_steps = pl.num_programs(0)
    n_tiles = x_ref.shape[0] // tm
    slot = lax.rem(i, 2)

    def copies(step, s, start):
        for k in range(n_tiles):
            _run_copies(tab_ref, n_tiles * step + k, loc_ref.at[s, k], ys_hbm, sem.at[s], to_global=False,
                        start=start)

    @pl.when(i == 0)
    def _():
        loc_ref[...] = jnp.zeros(loc_ref.shape, loc_ref.dtype)
        copies(0, 0, True)

    @pl.when(i + 1 < n_steps)
    def _():
        copies(i + 1, 1 - slot, True)

    copies(i, slot, False)

    lane = lax.broadcasted_iota(jnp.int32, (tm, SLOTS), 1).astype(F32)
    for k in range(n_tiles):
        rows = pl.ds(k * tm, tm)
        meta = meta_ref[rows, :]
        s0 = meta[:, _M_S0:_M_S0 + 1]
        s1 = meta[:, _M_S1:_M_S1 + 1]
        pick = jnp.where((lane == s0) | (lane == s1), 1.0, 0.0).astype(BF16)
        moe = jnp.dot(pick, loc_ref[slot, k], preferred_element_type=F32)
        o_ref[rows, :] = _layer_norm(alpha * x_ref[rows, :] + moe, lng_ref[...], lnb_ref[...])


def _combine_ln(x, meta, run_tab, ys, ln_g, ln_b, *, alpha):
    n, d = x.shape
    tm, k = TM_MOE, MOE_TILES
    n_steps = n // (k * tm)
    row = pl.BlockSpec((k * tm, d), lambda i, *_: (i, 0))
    grid_spec = pltpu.PrefetchScalarGridSpec(
        num_scalar_prefetch=1,
        grid=(n_steps,),
        in_specs=[row, pl.BlockSpec((k * tm, LANES), lambda i, *_: (i, 0)),
                  pl.BlockSpec(ln_g.shape, lambda i, *_: (0, 0)), pl.BlockSpec(ln_b.shape, lambda i, *_: (0, 0)),
                  pl.BlockSpec(memory_space=pl.ANY)],
        out_specs=row,
        scratch_shapes=[pltpu.VMEM((2, k, SLOTS, d), BF16), pltpu.SemaphoreType.DMA((2,))],
    )
    return pl.pallas_call(
        functools.partial(_combine_kernel, tm=tm, alpha=alpha),
        out_shape=jax.ShapeDtypeStruct((n, d), F32),
        grid_spec=grid_spec,
        compiler_params=_params(("arbitrary",)),
        name="combine_ln",
    )(run_tab, x, meta, ln_g, ln_b, ys)


def _moe_ln(x, w_router, we_gate, we_up, we_down, ln_g, ln_b, *, alpha):
    n, d = x.shape
    tm, te = TM_MOE, TM_EXPERT
    assert n % (MOE_TILES * tm) == 0 and te % SUB_EXPERT == 0
    nt = n // tm
    wr = jnp.pad(w_router, ((0, 0), (0, LANES - N_EXPERTS)))
    wr_hi = wr.astype(BF16)
    wr_lo = (wr - wr_hi.astype(F32)).astype(BF16)
    meta, runs, totals = _router(x, wr_hi, wr_lo)

    counts = totals[:N_EXPERTS, 0].astype(jnp.int32)
    padded = (counts + RUN_ALIGN - 1) // RUN_ALIGN * RUN_ALIGN
    tiles_e = (padded + te - 1) // te
    tile_end = jnp.cumsum(tiles_e)
    region = (tile_end - tiles_e) * te
    n_used = tile_end[-1]
    max_rows = TOP_K * n + N_EXPERTS * (te - 1)
    n_tiles = -(-max_rows // te)
    n_spare = n_tiles - (TOP_K * n) // te
    runs_i = runs.reshape(nt, EXPERT_ROWS, LANES)[:, :N_EXPERTS, :RUN_FIELDS].astype(jnp.int32)
    runs_i = jnp.swapaxes(runs_i, 1, 2)
    runs_i = runs_i.at[:, _R_GOFF, :].add(region)
    run_tab = runs_i.reshape(-1)
    fill_tab = jnp.concatenate([region + padded, tiles_e * te - padded,
                                (n_used * te)[None], (n_tiles - n_used)[None]]).astype(jnp.int32)
    tiles = jnp.arange(n_tiles, dtype=jnp.int32)
    j = jnp.minimum(tiles, n_used - 1)
    tile_expert = jnp.minimum(jnp.sum(j[:, None] >= tile_end[None, :], axis=-1), N_EXPERTS - 1).astype(jnp.int32)
    region_end = jnp.sum(jnp.where(tile_expert[:, None] == jnp.arange(N_EXPERTS), region + padded, 0), axis=-1)
    tile_rows = jnp.where(tiles < n_used, jnp.minimum(region_end - tiles * te, te), 0).astype(jnp.int32)

    xs = _dispatch(x, meta, run_tab, fill_tab, n_tiles * te, n_spare)
    ys = _experts(xs, we_gate, we_up, we_down, j, tile_expert, tile_rows)
    return _combine_ln(x, meta, run_tab, ys, ln_g, ln_b, alpha=alpha)


def kernel(x, w_in, conv_w, v_g, v_b, w_s, b_s, out_g, w_out, ln1_g, ln1_b, ln2_g, ln2_b,
           w_gate, w_up, w_down, w_router, we_gate, we_up, we_down):
    bsz, seq, d = x.shape
    depth = w_in.shape[0]
    alpha = float((2 * depth) ** 0.25)
    h = x.reshape(bsz * seq, d)
    row = lambda a: a.reshape(1, -1)
    mix_w = [w_in[0].astype(BF16), w_out[0].astype(BF16)]
    moe_w = [None, None, None]
    for i in range(depth):
        j = i // 2
        dense, last = i % 2 == 0, i + 1 == depth
        if dense:
            cast = [w_gate[j], w_up[j], w_down[j]] + ([] if last else [we_down[j]])
        else:
            cast = [we_up[j]]
        bias_t = jnp.repeat(b_s[i].T, GMLP_HEAD_DIM, axis=1)
        h, cast_w = _mixer_ln(h, mix_w[0], conv_w[i], row(v_g[i]), row(v_b[i]), w_s[i], bias_t,
                              row(out_g[i]), mix_w[1], row(ln1_g[i]), row(ln1_b[i]), cast=cast,
                              seq=seq, alpha=alpha)
        if dense:
            ahead = [] if last else [(w_in, i + 1), (w_out, i + 1), we_gate[j]]
            h, ahead_w = _ffn_ln(h, *cast_w[:3], row(ln2_g[i]), row(ln2_b[i]), cast=ahead, alpha=alpha)
            if not last:
                mix_w, moe_w = ahead_w[:2], [ahead_w[2], None, cast_w[3]]
        else:
            moe_w[1] = cast_w[0]
            h = _moe_ln(h, w_router[j], *moe_w, row(ln2_g[i]), row(ln2_b[i]), alpha=alpha)
            moe_w = [None, None, None]
            if not last:
                mix_w = [w_in[i + 1].astype(BF16), w_out[i + 1].astype(BF16)]
    return h.reshape(bsz, seq, d)
```

```python
import functools

import numpy as np
import jax
import jax.numpy as jnp
from jax import lax
from jax.experimental import pallas as pl
from jax.experimental.pallas import tpu as pltpu

F32 = jnp.float32
BF16 = jnp.bfloat16

CONV_WIDTH = 512
CONV_HEAD_DIM = 64
GMLP_WIDTH = 512
GMLP_HEADS = 4
GMLP_HEAD_DIM = 128
CHUNK = 128
N_EXPERTS = 8
TOP_K = 2
LN_EPS = 1e-5
RMS_EPS = 1e-6

LANES = 128
SUBLANES = 8
BF16_ROWS = 16
VMEM_LIMIT_BYTES = 56 * 1024 * 1024

TM_MIX = 1024
SUB_MIX = 512
TM_FFN = 1024
SUB_FFN = 512
TM_MOE = 512
MOE_TILES = 2
ROUTER_TILES = 1
TM_EXPERT = 1024
SUB_EXPERT = 512
FF_CHUNKS = 2

RUN_ALIGN = BF16_ROWS
ALIGN_SHIFT = RUN_ALIGN.bit_length() - 1
MAX_RUN = TM_MOE + RUN_ALIGN
RUN_BITS = (MAX_RUN // RUN_ALIGN).bit_length()
SLOTS = -(-(TOP_K * TM_MOE + N_EXPERTS * 2 * (RUN_ALIGN - 1)) // LANES) * LANES
GATE_LANES = LANES
GATE_PIECES = 3
TAIL_BITS = (TM_EXPERT // RUN_ALIGN - 1).bit_length()


def _layer_norm(r, g, b):
    mu = jnp.mean(r, axis=-1, keepdims=True)
    d = r - mu
    var = jnp.mean(d * d, axis=-1, keepdims=True)
    return d * lax.rsqrt(var + LN_EPS) * g + b


def _gelu(x):
    return 0.5 * x * (1.0 + lax.erf(x * (2.0 ** -0.5)))


def _silu(x):
    return x * (1.0 / (1.0 + jnp.exp(-x)))


def _params(semantics):
    return pltpu.CompilerParams(dimension_semantics=semantics, vmem_limit_bytes=VMEM_LIMIT_BYTES)


def _const_spec(shape):
    nd = len(shape)
    return pl.BlockSpec(shape, lambda *_: (0,) * nd)


def _cast_plan(weights, n_steps):
    views, in_specs, out_specs, out_shapes, shapes = [], [], [], [], []
    for w in weights:
        a, layer = w if isinstance(w, tuple) else (w, None)
        shape = a.shape if layer is None else a.shape[1:]
        rows, cols = int(np.prod(shape[:-1])), shape[-1]
        n_blocks = next(nb for nb in range(n_steps, 0, -1)
                        if n_steps % nb == 0 and rows % (nb * BF16_ROWS) == 0)
        r, first = n_steps // n_blocks, 0 if layer is None else layer * n_blocks
        views.append(a.reshape(-1, cols))
        in_specs.append(pl.BlockSpec((rows // n_blocks, cols), lambda i, r=r, first=first: (first + i // r, 0)))
        out_specs.append(pl.BlockSpec((rows // n_blocks, cols), lambda i, r=r: (i // r, 0)))
        out_shapes.append(jax.ShapeDtypeStruct((rows, cols), BF16))
        shapes.append(shape)
    return views, in_specs, out_specs, out_shapes, shapes


def _cast_along(src_refs, dst_refs):
    for src, dst in zip(src_refs, dst_refs):
        dst[...] = src[...].astype(dst.dtype)


def _mixer_kernel(*refs, n_cast, tm, sub, seq, alpha):
    (x_ref, win_ref, convw_ref, vg_ref, vb_ref, ws_ref, bias_ref, outg_ref, wout_ref,
     lng_ref, lnb_ref) = refs[:11]
    o_ref, ch_ref = refs[11 + n_cast], refs[-1]
    _cast_along(refs[11:11 + n_cast], refs[12 + n_cast:12 + 2 * n_cast])
    i = pl.program_id(0)
    cw_, gw_ = CONV_WIDTH, GMLP_WIDTH

    @pl.when(lax.rem(i * tm, seq) == 0)
    def _():
        ch_ref[0:SUBLANES, :] = jnp.zeros((SUBLANES, cw_), F32)

    trow = lax.broadcasted_iota(jnp.int32, (CHUNK, CHUNK), 0)
    tcol = lax.broadcasted_iota(jnp.int32, (CHUNK, CHUNK), 1)
    causal = tcol <= trow
    ws = [jnp.where(causal, ws_ref[h], 0.0).astype(BF16) for h in range(GMLP_HEADS)]
    lane = lax.broadcasted_iota(jnp.int32, (sub, LANES), 1)
    low_half = lane < CONV_HEAD_DIM
    nch = sub // CHUNK

    for r0 in range(0, tm, sub):
        x = x_ref[r0:r0 + sub, :]
        proj = jnp.dot(x.astype(BF16), win_ref[...], preferred_element_type=F32)
        b_gate = proj[:, 0:cw_]
        c_gate = proj[:, cw_:2 * cw_]
        hh = proj[:, 2 * cw_:3 * cw_]
        u = proj[:, 3 * cw_:3 * cw_ + gw_]
        v = proj[:, 3 * cw_ + gw_:3 * cw_ + 2 * gw_]

        c0 = SUBLANES + r0
        ch_ref[c0:c0 + sub, :] = c_gate * hh
        convw = convw_ref[...]
        conv = (convw[0:1, :] * ch_ref[c0 - 2:c0 - 2 + sub, :]
                + convw[1:2, :] * ch_ref[c0 - 1:c0 - 1 + sub, :]
                + convw[2:3, :] * ch_ref[c0:c0 + sub, :])
        y_conv = b_gate * conv

        ug = _gelu(u)
        vn = _layer_norm(_gelu(v), vg_ref[...], vb_ref[...]).astype(BF16)
        zs = []
        for h in range(GMLP_HEADS):
            lo, hi = h * GMLP_HEAD_DIM, (h + 1) * GMLP_HEAD_DIM
            rhs = jnp.concatenate([vn[c * CHUNK:(c + 1) * CHUNK, lo:hi] for c in range(nch)], axis=1)
            zs.append(jnp.dot(ws[h], rhs, preferred_element_type=F32))
        z = jnp.concatenate(
            [jnp.concatenate([zs[h][:, c * GMLP_HEAD_DIM:(c + 1) * GMLP_HEAD_DIM]
                              for h in range(GMLP_HEADS)], axis=1) for c in range(nch)], axis=0)
        bias = jnp.concatenate([bias_ref[...]] * nch, axis=0)
        y_sg = ug * (z + bias)

        parts = []
        for j in range(cw_ // LANES):
            yt = y_conv[:, j * LANES:(j + 1) * LANES]
            sq = yt * yt
            ms_lo = jnp.sum(jnp.where(low_half, sq, 0.0), axis=-1, keepdims=True) * (1.0 / CONV_HEAD_DIM)
            ms_hi = jnp.sum(jnp.where(low_half, 0.0, sq), axis=-1, keepdims=True) * (1.0 / CONV_HEAD_DIM)
            parts.append(yt * jnp.where(low_half, lax.rsqrt(ms_lo + RMS_EPS), lax.rsqrt(ms_hi + RMS_EPS)))
        for h in range(GMLP_HEADS):
            yt = y_sg[:, h * GMLP_HEAD_DIM:(h + 1) * GMLP_HEAD_DIM]
            ms = jnp.mean(yt * yt, axis=-1, keepdims=True)
            parts.append(yt * lax.rsqrt(ms + RMS_EPS))
        y = jnp.concatenate(parts, axis=1) * outg_ref[...]

        mix = jnp.dot(y.astype(BF16), wout_ref[...], preferred_element_type=F32)
        o_ref[r0:r0 + sub, :] = _layer_norm(alpha * x + mix, lng_ref[...], lnb_ref[...])

    ch_ref[0:SUBLANES, :] = ch_ref[tm:tm + SUBLANES, :]


def _mixer_ln(x, w_in, conv_w, v_g, v_b, w_s, bias_t, out_g, w_out, ln_g, ln_b, cast=(), *, seq, alpha):
    n, d = x.shape
    tm, sub = TM_MIX, SUB_MIX
    assert n % tm == 0 and seq % tm == 0 and tm % sub == 0 and sub % CHUNK == 0
    views, cast_in, cast_out, cast_shapes, shapes = _cast_plan(cast, n // tm)
    kern = functools.partial(_mixer_kernel, n_cast=len(cast), tm=tm, sub=sub, seq=seq, alpha=alpha)
    row = pl.BlockSpec((tm, d), lambda i: (i, 0))
    out = pl.pallas_call(
        kern,
        out_shape=[jax.ShapeDtypeStruct((n, d), F32)] + cast_shapes,
        grid=(n // tm,),
        in_specs=[row, _const_spec(w_in.shape), _const_spec(conv_w.shape), _const_spec(v_g.shape),
                  _const_spec(v_b.shape), _const_spec(w_s.shape), _const_spec(bias_t.shape),
                  _const_spec(out_g.shape), _const_spec(w_out.shape), _const_spec(ln_g.shape),
                  _const_spec(ln_b.shape)] + cast_in,
        out_specs=[row] + cast_out,
        scratch_shapes=[pltpu.VMEM((tm + SUBLANES, CONV_WIDTH), F32)],
        compiler_params=_params(("arbitrary",)),
        name="mixer_ln",
    )(x, w_in, conv_w, v_g, v_b, w_s, bias_t, out_g, w_out, ln_g, ln_b, *views)
    return out[0], [o.reshape(shape) for o, shape in zip(out[1:], shapes)]


def _ffn_kernel(*refs, n_cast, sub, alpha):
    x_ref, wg_ref, wu_ref, wd_ref, lng_ref, lnb_ref = refs[:6]
    o_ref = refs[6 + n_cast]
    _cast_along(refs[6:6 + n_cast], refs[7 + n_cast:7 + 2 * n_cast])
    for r0 in range(0, x_ref.shape[0], sub):
        x = x_ref[r0:r0 + sub, :]
        xb = x.astype(BF16)
        g = jnp.dot(xb, wg_ref[...], preferred_element_type=F32)
        u = jnp.dot(xb, wu_ref[...], preferred_element_type=F32)
        h = (_silu(g) * u).astype(BF16)
        ffn = jnp.dot(h, wd_ref[...], preferred_element_type=F32)
        o_ref[r0:r0 + sub, :] = _layer_norm(alpha * x + ffn, lng_ref[...], lnb_ref[...])


def _ffn_ln(x, wg, wu, wd, ln_g, ln_b, cast=(), *, alpha):
    n, d = x.shape
    tm = TM_FFN
    assert n % tm == 0 and tm % SUB_FFN == 0
    views, cast_in, cast_out, cast_shapes, shapes = _cast_plan(cast, n // tm)
    row = pl.BlockSpec((tm, d), lambda i: (i, 0))
    out = pl.pallas_call(
        functools.partial(_ffn_kernel, n_cast=len(cast), sub=SUB_FFN, alpha=alpha),
        out_shape=[jax.ShapeDtypeStruct((n, d), F32)] + cast_shapes,
        grid=(n // tm,),
        in_specs=[row, _const_spec(wg.shape), _const_spec(wu.shape), _const_spec(wd.shape),
                  _const_spec(ln_g.shape), _const_spec(ln_b.shape)] + cast_in,
        out_specs=[row] + cast_out,
        compiler_params=_params(("arbitrary",)),
        name="ffn_ln",
    )(x, wg, wu, wd, ln_g, ln_b, *views)
    return out[0], [o.reshape(shape) for o, shape in zip(out[1:], shapes)]


_M_E0, _M_E1, _M_S0, _M_S1, _M_G0, _M_G1 = range(6)
_R_LSTART, _R_LEN, _R_GOFF, _R_PARTIAL = range(4)
RUN_FIELDS = 4
EXPERT_ROWS = max(N_EXPERTS, BF16_ROWS)


def _router_kernel(x_ref, wrh_ref, wrl_ref, meta_ref, metat_ref, runs_ref, tot_ref, base_ref, *, tm):
    i = pl.program_id(0)

    @pl.when(i == 0)
    def _():
        base_ref[...] = jnp.zeros(base_ref.shape, F32)

    for k in range(x_ref.shape[0] // tm):
        _route_tile(x_ref[k * tm:(k + 1) * tm, :], wrh_ref, wrl_ref, meta_ref.at[k * tm:(k + 1) * tm],
                    metat_ref.at[k * SUBLANES:(k + 1) * SUBLANES],
                    runs_ref.at[k * EXPERT_ROWS:(k + 1) * EXPERT_ROWS], tot_ref, base_ref)


def _route_tile(x, wrh_ref, wrl_ref, meta_ref, metat_ref, runs_ref, tot_ref, base_ref):
    tm = x.shape[0]
    xh = x.astype(BF16)
    xl = (x - xh.astype(F32)).astype(BF16)
    wrh = wrh_ref[...]
    logits = (jnp.dot(xh, wrh, preferred_element_type=F32)
              + jnp.dot(xl, wrh, preferred_element_type=F32)
              + jnp.dot(xh, wrl_ref[...], preferred_element_type=F32))

    er = EXPERT_ROWS
    lg = logits.T[0:er, :]
    ef = lax.broadcasted_iota(jnp.int32, (er, tm), 0).astype(F32)
    neg = jnp.float32(-jnp.inf)
    lg = jnp.where(ef < N_EXPERTS, lg, neg)
    m0 = jnp.max(lg, axis=0, keepdims=True)
    e0 = jnp.min(jnp.where(lg == m0, ef, float(er)), axis=0, keepdims=True)
    lg1 = jnp.where(ef == e0, neg, lg)
    m1 = jnp.max(lg1, axis=0, keepdims=True)
    e1 = jnp.min(jnp.where(lg1 == m1, ef, float(er)), axis=0, keepdims=True)
    t = jnp.exp(m1 - m0)
    g0 = 1.0 / (1.0 + t)
    g1 = t / (1.0 + t)

    oh0 = ef == e0
    oh1 = ef == e1
    cnt = jnp.where(oh0 | oh1, 1.0, 0.0)
    s_i = lax.broadcasted_iota(jnp.int32, (tm, tm), 0)
    t_i = lax.broadcasted_iota(jnp.int32, (tm, tm), 1)
    earlier = jnp.where(s_i < t_i, 1.0, 0.0).astype(BF16)
    rank = jnp.dot(cnt.astype(BF16), earlier, preferred_element_type=F32)

    def align_down(v):
        return jnp.floor(v * (1.0 / RUN_ALIGN)) * RUN_ALIGN

    count = jnp.broadcast_to(jnp.sum(cnt, axis=1, keepdims=True), (er, LANES))
    before = base_ref[...]
    phase = before - align_down(before)
    end = phase + count
    run_len = jnp.where(count > 0, align_down(end + (RUN_ALIGN - 1.0)), 0.0)
    partial = jnp.where((count > 0) & (end != align_down(end)), 1.0, 0.0)
    r_e = lax.broadcasted_iota(jnp.int32, (er, er), 0)
    c_e = lax.broadcasted_iota(jnp.int32, (er, er), 1)
    lower = jnp.where(c_e < r_e, 1.0, 0.0).astype(BF16)
    lstart = jnp.dot(lower, run_len.astype(BF16), preferred_element_type=F32)
    base_ref[...] = before + count
    tot_ref[...] = before + count
    field = lax.broadcasted_iota(jnp.int32, (er, LANES), 1)
    table = jnp.zeros((er, LANES), F32)
    for f, val in ((_R_LSTART, lstart), (_R_LEN, run_len), (_R_GOFF, before - phase), (_R_PARTIAL, partial)):
        table = jnp.where(field == f, val, table)
    runs_ref[...] = table

    slot = rank + (lstart + phase)[:, 0:1]
    s0 = jnp.sum(jnp.where(oh0, slot, 0.0), axis=0, keepdims=True)
    s1 = jnp.sum(jnp.where(oh1, slot, 0.0), axis=0, keepdims=True)
    row = lax.broadcasted_iota(jnp.int32, (SUBLANES, tm), 0)
    meta_t = jnp.zeros((SUBLANES, tm), F32)
    for k, val in ((_M_E0, e0), (_M_E1, e1), (_M_S0, s0), (_M_S1, s1), (_M_G0, g0), (_M_G1, g1)):
        meta_t = jnp.where(row == k, val, meta_t)
    metat_ref[...] = meta_t
    meta_ref[...] = jnp.concatenate([meta_t, jnp.zeros((LANES - SUBLANES, tm), F32)], axis=0).T


def _router(x, wr_hi, wr_lo):
    n, d = x.shape
    tm, k = TM_MOE, ROUTER_TILES
    assert n % (k * tm) == 0
    nt = n // tm
    small = (EXPERT_ROWS, LANES)
    return pl.pallas_call(
        functools.partial(_router_kernel, tm=tm),
        out_shape=(jax.ShapeDtypeStruct((n, LANES), F32), jax.ShapeDtypeStruct((nt * SUBLANES, tm), F32),
                   jax.ShapeDtypeStruct((nt * EXPERT_ROWS, LANES), F32), jax.ShapeDtypeStruct(small, F32)),
        grid=(nt // k,),
        in_specs=[pl.BlockSpec((k * tm, d), lambda i: (i, 0)), _const_spec(wr_hi.shape), _const_spec(wr_lo.shape)],
        out_specs=(pl.BlockSpec((k * tm, LANES), lambda i: (i, 0)), pl.BlockSpec((k * SUBLANES, tm), lambda i: (i, 0)),
                   pl.BlockSpec((k * EXPERT_ROWS, LANES), lambda i: (i, 0)), _const_spec(small)),
        scratch_shapes=[pltpu.VMEM(small, F32)],
        compiler_params=_params(("arbitrary",)),
        name="router",
    )(x, wr_hi, wr_lo)


def _run_copies(tab_ref, tile, local_ref, glob_hbm, sem, *, to_global, start):
    base = tile * (RUN_FIELDS * N_EXPERTS)
    for e in range(N_EXPERTS):
        lstart = tab_ref[base + e]
        length = tab_ref[base + N_EXPERTS + e]
        gstart = tab_ref[base + 2 * N_EXPERTS + e]
        for b in range(RUN_BITS):
            size = RUN_ALIGN << b
            off = (length >> (b + ALIGN_SHIFT + 1)) << (b + ALIGN_SHIFT + 1)

            def piece(size=size, off=off, lstart=lstart, gstart=gstart):
                loc = local_ref.at[pl.ds(pl.multiple_of(lstart + off, RUN_ALIGN), size)]
                glo = glob_hbm.at[pl.ds(pl.multiple_of(gstart + off, RUN_ALIGN), size)]
                cp = pltpu.make_async_copy(loc, glo, sem) if to_global else pltpu.make_async_copy(glo, loc, sem)
                if start:
                    cp.start()
                else:
                    cp.wait()

            pl.when(((length >> (b + ALIGN_SHIFT)) & 1) == 1)(piece)


def _fill_copies(fill_ref, zero_ref, glob_hbm, sem, *, te, n_spare, start):
    def go(cp):
        if start:
            cp.start()
        else:
            cp.wait()

    for e in range(N_EXPERTS):
        tstart = fill_ref[e]
        length = fill_ref[N_EXPERTS + e]
        for b in range(TAIL_BITS):
            size = RUN_ALIGN << b
            off = (length >> (b + ALIGN_SHIFT + 1)) << (b + ALIGN_SHIFT + 1)

            def piece(size=size, off=off, tstart=tstart):
                go(pltpu.make_async_copy(zero_ref.at[pl.ds(0, size)],
                                         glob_hbm.at[pl.ds(pl.multiple_of(tstart + off, RUN_ALIGN), size)], sem))

            pl.when(((length >> (b + ALIGN_SHIFT)) & 1) == 1)(piece)
    spare0 = fill_ref[2 * N_EXPERTS]
    for j in range(n_spare):
        def tile_fill(j=j):
            go(pltpu.make_async_copy(zero_ref, glob_hbm.at[pl.ds(pl.multiple_of(spare0 + j * te, te), te)], sem))

        pl.when(j < fill_ref[2 * N_EXPERTS + 1])(tile_fill)


def _sort_tile(rows, x_ref, meta_ref, meta_t, loc_ref, *, tm):
    d = x_ref.shape[1]
    meta = meta_ref[rows, :]
    s0 = meta_t[_M_S0:_M_S0 + 1, :]
    s1 = meta_t[_M_S1:_M_S1 + 1, :]
    slot = lax.broadcasted_iota(jnp.int32, (SLOTS, tm), 0).astype(F32)
    signed = jnp.where(slot == s0, 1.0, jnp.where(slot == s1, -1.0, 0.0)).astype(BF16)
    loc_ref[:, 0:d] = jnp.dot(jnp.abs(signed), x_ref[rows, :].astype(BF16),
                              preferred_element_type=F32).astype(BF16)

    lane = lax.broadcasted_iota(jnp.int32, (tm, GATE_LANES), 1)
    info = jnp.where(lane == 2 * GATE_PIECES, 1.0, 0.0)
    for k, g in ((0, meta[:, _M_G0:_M_G0 + 1]), (1, meta[:, _M_G1:_M_G1 + 1])):
        rest = g
        for p in range(GATE_PIECES):
            piece = rest.astype(BF16).astype(F32)
            info = jnp.where(lane == k * GATE_PIECES + p, piece, info)
            rest = rest - piece
    loc_ref[:, d:d + GATE_LANES] = jnp.dot(signed, info.astype(BF16), preferred_element_type=F32).astype(BF16)


def _merge_carry(tile, tab_ref, loc_ref, carry_ref):
    base = tile * (RUN_FIELDS * N_EXPERTS)
    for e in range(N_EXPERTS):
        lstart = tab_ref[base + _R_LSTART * N_EXPERTS + e]
        length = tab_ref[base + _R_LEN * N_EXPERTS + e]
        partial = tab_ref[base + _R_PARTIAL * N_EXPERTS + e]

        def merge(e=e, lstart=lstart, length=length, partial=partial):
            first = pl.ds(pl.multiple_of(lstart, RUN_ALIGN), RUN_ALIGN)
            loc_ref[first, :] = loc_ref[first, :] + carry_ref[e]
            last = pl.ds(pl.multiple_of(lstart + length - RUN_ALIGN, RUN_ALIGN), RUN_ALIGN)

            @pl.when(partial == 1)
            def _():
                carry_ref[e] = loc_ref[last, :]

            @pl.when(partial == 0)
            def _():
                carry_ref[e] = jnp.zeros(carry_ref.shape[1:], BF16)

        pl.when(length > 0)(merge)


def _dispatch_kernel(tab_ref, fill_ref, x_ref, meta_ref, metat_ref, xs_hbm, *scratch, tm, te, n_spare):
    *loc_refs, zero_ref, carry_ref, sem, zsem = scratch
    i = pl.program_id(0)

    @pl.when(i == 0)
    def _():
        zero_ref[...] = jnp.zeros(zero_ref.shape, BF16)
        carry_ref[...] = jnp.zeros(carry_ref.shape, BF16)
        _fill_copies(fill_ref, zero_ref, xs_hbm, zsem, te=te, n_spare=n_spare, start=True)

    n_tiles = x_ref.shape[0] // tm
    for k in range(n_tiles):
        tile = n_tiles * i + k
        _sort_tile(pl.ds(k * tm, tm), x_ref, meta_ref, metat_ref[k * SUBLANES:(k + 1) * SUBLANES, :],
                   loc_refs[k], tm=tm)
        if k > 0:
            _run_copies(tab_ref, tile - 1, loc_refs[k - 1], xs_hbm, sem, to_global=True, start=False)
        _merge_carry(tile, tab_ref, loc_refs[k], carry_ref)
        _run_copies(tab_ref, tile, loc_refs[k], xs_hbm, sem, to_global=True, start=True)
    _run_copies(tab_ref, n_tiles * i + n_tiles - 1, loc_refs[-1], xs_hbm, sem, to_global=True, start=False)

    @pl.when(i == pl.num_programs(0) - 1)
    def _():
        _fill_copies(fill_ref, zero_ref, xs_hbm, zsem, te=te, n_spare=n_spare, start=False)


def _dispatch(x, meta, meta_t, run_tab, fill_tab, n_rows, n_spare):
    n, d = x.shape
    tm, te, k = TM_MOE, TM_EXPERT, MOE_TILES
    width = d + GATE_LANES
    grid_spec = pltpu.PrefetchScalarGridSpec(
        num_scalar_prefetch=2,
        grid=(n // (k * tm),),
        in_specs=[pl.BlockSpec((k * tm, d), lambda i, *_: (i, 0)),
                  pl.BlockSpec((k * tm, LANES), lambda i, *_: (i, 0)),
                  pl.BlockSpec((k * SUBLANES, tm), lambda i, *_: (i, 0))],
        out_specs=pl.BlockSpec(memory_space=pl.ANY),
        scratch_shapes=[pltpu.VMEM((SLOTS, width), BF16)] * k + [pltpu.VMEM((te, width), BF16),
                        pltpu.VMEM((N_EXPERTS, RUN_ALIGN, width), BF16),
                        pltpu.SemaphoreType.DMA, pltpu.SemaphoreType.DMA],
    )
    return pl.pallas_call(
        functools.partial(_dispatch_kernel, tm=tm, te=te, n_spare=n_spare),
        out_shape=jax.ShapeDtypeStruct((n_rows, width), BF16),
        grid_spec=grid_spec,
        compiler_params=_params(("arbitrary",)),
        name="dispatch",
    )(run_tab, fill_tab, x, meta, meta_t)


def _slot_gate(info):
    lane = lax.broadcasted_iota(jnp.int32, info.shape, 1)
    first = jnp.sum(jnp.where(lane < GATE_PIECES, info, 0.0), axis=-1, keepdims=True)
    second = jnp.sum(jnp.where((lane >= GATE_PIECES) & (lane < 2 * GATE_PIECES), info, 0.0), axis=-1, keepdims=True)
    sign = jnp.sum(jnp.where(lane == 2 * GATE_PIECES, info, 0.0), axis=-1, keepdims=True)
    return jnp.where(sign > 0, first, -second)


def _experts_kernel(tile_ref, expert_ref, rows_ref, xs_ref, wg_ref, wu_ref, wd_ref, ys_ref, acc_ref, *, sub):
    i = pl.program_id(0)
    f = pl.program_id(1)
    te, d = ys_ref.shape
    rows = rows_ref[i]

    @pl.when((rows == 0) & (f == 0))
    def _():
        ys_ref[...] = jnp.zeros(ys_ref.shape, ys_ref.dtype)

    def partial_out(r0):
        xb = xs_ref[r0:r0 + sub, 0:d]
        g = jnp.dot(xb, wg_ref[0], preferred_element_type=F32)
        u = jnp.dot(xb, wu_ref[0], preferred_element_type=F32)
        h = (_silu(g) * u).astype(BF16)
        return jnp.dot(h, wd_ref[0], preferred_element_type=F32)

    def first_chunk(n_rows):
        for r0 in range(0, n_rows, sub):
            acc_ref[r0:r0 + sub, :] = partial_out(r0)

    def last_chunk(n_rows):
        for r0 in range(0, n_rows, sub):
            gate = _slot_gate(xs_ref[r0:r0 + sub, d:d + GATE_LANES].astype(F32))
            ys_ref[r0:r0 + sub, :] = ((acc_ref[r0:r0 + sub, :] + partial_out(r0)) * gate).astype(ys_ref.dtype)
        if n_rows < te:
            ys_ref[n_rows:te, :] = jnp.zeros((te - n_rows, d), ys_ref.dtype)

    for nb in range(1, te // sub + 1):
        fits = (rows > (nb - 1) * sub) & (rows <= nb * sub)
        pl.when(fits & (f == 0))(functools.partial(first_chunk, nb * sub))
        pl.when(fits & (f == 1))(functools.partial(last_chunk, nb * sub))


def _experts(xs, we_gate, we_up, we_down, tile_idx, tile_expert, tile_rows):
    n_rows, width = xs.shape
    d = width - GATE_LANES
    te = TM_EXPERT
    ff = we_gate.shape[-1]
    fc = ff // FF_CHUNKS
    assert n_rows % te == 0 and ff % FF_CHUNKS == 0 and fc % LANES == 0 and FF_CHUNKS == 2
    n_tiles = n_rows // te
    last = FF_CHUNKS - 1
    grid_spec = pltpu.PrefetchScalarGridSpec(
        num_scalar_prefetch=3,
        grid=(n_tiles, FF_CHUNKS),
        in_specs=[
            pl.BlockSpec((te, width), lambda i, f, tile, ex, rows: (tile[i], 0)),
            pl.BlockSpec((1, d, fc), lambda i, f, tile, ex, rows: (ex[i], 0, jnp.where(rows[i] > 0, f, last))),
            pl.BlockSpec((1, d, fc), lambda i, f, tile, ex, rows: (ex[i], 0, jnp.where(rows[i] > 0, f, last))),
            pl.BlockSpec((1, fc, d), lambda i, f, tile, ex, rows: (ex[i], jnp.where(rows[i] > 0, f, last), 0)),
        ],
        out_specs=pl.BlockSpec((te, d), lambda i, f, tile, ex, rows: (i, 0)),
        scratch_shapes=[pltpu.VMEM((te, d), F32)],
    )
    return pl.pallas_call(
        functools.partial(_experts_kernel, sub=SUB_EXPERT),
        out_shape=jax.ShapeDtypeStruct((n_rows, d), BF16),
        grid_spec=grid_spec,
        compiler_params=_params(("arbitrary", "arbitrary")),
        name="experts",
    )(tile_idx, tile_expert, tile_rows, xs, we_gate, we_up, we_down)


def _combine_kernel(tab_ref, x_ref, meta_ref, lng_ref, lnb_ref, ys_hbm, o_ref, loc_ref, sem, *, tm, alpha):
    i = pl.program_id(0)
    n_steps = pl.num_programs(0)
    n_tiles = x_ref.shape[0] // tm
    slot = lax.rem(i, 2)

    def copies(step, s, start):
        for k in range(n_tiles):
            _run_copies(tab_ref, n_tiles * step + k, loc_ref.at[s, k], ys_hbm, sem.at[s], to_global=False,
                        start=start)

    @pl.when(i == 0)
    def _():
        loc_ref[...] = jnp.zeros(loc_ref.shape, loc_ref.dtype)
        copies(0, 0, True)

    @pl.when(i + 1 < n_steps)
    def _():
        copies(i + 1, 1 - slot, True)

    copies(i, slot, False)

    lane = lax.broadcasted_iota(jnp.int32, (tm, SLOTS), 1).astype(F32)
    for k in range(n_tiles):
        rows = pl.ds(k * tm, tm)
        meta = meta_ref[rows, :]
        s0 = meta[:, _M_S0:_M_S0 + 1]
        s1 = meta[:, _M_S1:_M_S1 + 1]
        pick = jnp.where((lane == s0) | (lane == s1), 1.0, 0.0).astype(BF16)
        moe = jnp.dot(pick, loc_ref[slot, k], preferred_element_type=F32)
        o_ref[rows, :] = _layer_norm(alpha * x_ref[rows, :] + moe, lng_ref[...], lnb_ref[...])


def _combine_ln(x, meta, run_tab, ys, ln_g, ln_b, *, alpha):
    n, d = x.shape
    tm, k = TM_MOE, MOE_TILES
    n_steps = n // (k * tm)
    row = pl.BlockSpec((k * tm, d), lambda i, *_: (i, 0))
    grid_spec = pltpu.PrefetchScalarGridSpec(
        num_scalar_prefetch=1,
        grid=(n_steps,),
        in_specs=[row, pl.BlockSpec((k * tm, LANES), lambda i, *_: (i, 0)),
                  pl.BlockSpec(ln_g.shape, lambda i, *_: (0, 0)), pl.BlockSpec(ln_b.shape, lambda i, *_: (0, 0)),
                  pl.BlockSpec(memory_space=pl.ANY)],
        out_specs=row,
        scratch_shapes=[pltpu.VMEM((2, k, SLOTS, d), BF16), pltpu.SemaphoreType.DMA((2,))],
    )
    return pl.pallas_call(
        functools.partial(_combine_kernel, tm=tm, alpha=alpha),
        out_shape=jax.ShapeDtypeStruct((n, d), F32),
        grid_spec=grid_spec,
        compiler_params=_params(("arbitrary",)),
        name="combine_ln",
    )(run_tab, x, meta, ln_g, ln_b, ys)


def _moe_ln(x, w_router, we_gate, we_up, we_down, ln_g, ln_b, *, alpha):
    n, d = x.shape
    tm, te = TM_MOE, TM_EXPERT
    assert n % (MOE_TILES * tm) == 0 and te % SUB_EXPERT == 0
    nt = n // tm
    wr = jnp.pad(w_router, ((0, 0), (0, LANES - N_EXPERTS)))
    wr_hi = wr.astype(BF16)
    wr_lo = (wr - wr_hi.astype(F32)).astype(BF16)
    meta, meta_t, runs, totals = _router(x, wr_hi, wr_lo)

    counts = totals[:N_EXPERTS, 0].astype(jnp.int32)
    padded = (counts + RUN_ALIGN - 1) // RUN_ALIGN * RUN_ALIGN
    tiles_e = (padded + te - 1) // te
    tile_end = jnp.cumsum(tiles_e)
    region = (tile_end - tiles_e) * te
    n_used = tile_end[-1]
    max_rows = TOP_K * n + N_EXPERTS * (te - 1)
    n_tiles = -(-max_rows // te)
    n_spare = n_tiles - (TOP_K * n) // te
    runs_i = runs.reshape(nt, EXPERT_ROWS, LANES)[:, :N_EXPERTS, :RUN_FIELDS].astype(jnp.int32)
    runs_i = jnp.swapaxes(runs_i, 1, 2)
    runs_i = runs_i.at[:, _R_GOFF, :].add(region)
    run_tab = runs_i.reshape(-1)
    fill_tab = jnp.concatenate([region + padded, tiles_e * te - padded,
                                (n_used * te)[None], (n_tiles - n_used)[None]]).astype(jnp.int32)
    tiles = jnp.arange(n_tiles, dtype=jnp.int32)
    j = jnp.minimum(tiles, n_used - 1)
    tile_expert = jnp.minimum(jnp.sum(j[:, None] >= tile_end[None, :], axis=-1), N_EXPERTS - 1).astype(jnp.int32)
    region_end = jnp.sum(jnp.where(tile_expert[:, None] == jnp.arange(N_EXPERTS), region + padded, 0), axis=-1)
    tile_rows = jnp.where(tiles < n_used, jnp.minimum(region_end - tiles * te, te), 0).astype(jnp.int32)

    xs = _dispatch(x, meta, meta_t, run_tab, fill_tab, n_tiles * te, n_spare)
    ys = _experts(xs, we_gate, we_up, we_down, j, tile_expert, tile_rows)
    return _combine_ln(x, meta, run_tab, ys, ln_g, ln_b, alpha=alpha)


def kernel(x, w_in, conv_w, v_g, v_b, w_s, b_s, out_g, w_out, ln1_g, ln1_b, ln2_g, ln2_b,
           w_gate, w_up, w_down, w_router, we_gate, we_up, we_down):
    bsz, seq, d = x.shape
    depth = w_in.shape[0]
    alpha = float((2 * depth) ** 0.25)
    h = x.reshape(bsz * seq, d)
    row = lambda a: a.reshape(1, -1)
    mix_w = [w_in[0].astype(BF16), w_out[0].astype(BF16)]
    moe_w = [None, None, None]
    for i in range(depth):
        j = i // 2
        dense, last = i % 2 == 0, i + 1 == depth
        if dense:
            cast = [w_gate[j], w_up[j], w_down[j]] + ([] if last else [we_down[j]])
        else:
            cast = [we_up[j]]
        bias_t = jnp.repeat(b_s[i].T, GMLP_HEAD_DIM, axis=1)
        h, cast_w = _mixer_ln(h, mix_w[0], conv_w[i], row(v_g[i]), row(v_b[i]), w_s[i], bias_t,
                              row(out_g[i]), mix_w[1], row(ln1_g[i]), row(ln1_b[i]), cast=cast,
                              seq=seq, alpha=alpha)
        if dense:
            ahead = [] if last else [(w_in, i + 1), (w_out, i + 1), we_gate[j]]
            h, ahead_w = _ffn_ln(h, *cast_w[:3], row(ln2_g[i]), row(ln2_b[i]), cast=ahead, alpha=alpha)
            if not last:
                mix_w, moe_w = ahead_w[:2], [ahead_w[2], None, cast_w[3]]
        else:
            moe_w[1] = cast_w[0]
            h = _moe_ln(h, w_router[j], *moe_w, row(ln2_g[i]), row(ln2_b[i]), alpha=alpha)
            moe_w = [None, None, None]
            if not last:
                mix_w = [w_in[i + 1].astype(BF16), w_out[i + 1].astype(BF16)]
    return h.reshape(bsz, seq, d)
```

```python
import functools

import numpy as np
import jax
import jax.numpy as jnp
from jax import lax
from jax.experimental import pallas as pl
from jax.experimental.pallas import tpu as pltpu

F32 = jnp.float32
BF16 = jnp.bfloat16

CONV_WIDTH = 512
CONV_HEAD_DIM = 64
GMLP_WIDTH = 512
GMLP_HEADS = 4
GMLP_HEAD_DIM = 128
CHUNK = 128
N_EXPERTS = 8
TOP_K = 2
LN_EPS = 1e-5
RMS_EPS = 1e-6

LANES = 128
SUBLANES = 8
BF16_ROWS = 16
VMEM_LIMIT_BYTES = 56 * 1024 * 1024

TM_MIX = 1024
SUB_MIX = 512
TM_FFN = 1024
SUB_FFN = 512
TM_MOE = 512
MOE_TILES = 2
ROUTER_TILES = 1
TM_EXPERT = 1024
SUB_EXPERT = 512
FF_CHUNKS = 2

RUN_ALIGN = BF16_ROWS
ALIGN_SHIFT = RUN_ALIGN.bit_length() - 1
MAX_RUN = TM_MOE + RUN_ALIGN
RUN_BITS = (MAX_RUN // RUN_ALIGN).bit_length()
SLOTS = -(-(TOP_K * TM_MOE + N_EXPERTS * 2 * (RUN_ALIGN - 1)) // LANES) * LANES
GATE_LANES = LANES
GATE_PIECES = 3
TAIL_BITS = (TM_EXPERT // RUN_ALIGN - 1).bit_length()


def _layer_norm(r, g, b):
    mu = jnp.mean(r, axis=-1, keepdims=True)
    d = r - mu
    var = jnp.mean(d * d, axis=-1, keepdims=True)
    return d * lax.rsqrt(var + LN_EPS) * g + b


def _gelu(x):
    return 0.5 * x * (1.0 + lax.erf(x * (2.0 ** -0.5)))


def _silu(x):
    return x * (1.0 / (1.0 + jnp.exp(-x)))


def _params(semantics):
    return pltpu.CompilerParams(dimension_semantics=semantics, vmem_limit_bytes=VMEM_LIMIT_BYTES)


def _const_spec(shape):
    nd = len(shape)
    return pl.BlockSpec(shape, lambda *_: (0,) * nd)


def _cast_plan(weights, n_steps):
    views, in_specs, out_specs, out_shapes, shapes = [], [], [], [], []
    for w in weights:
        a, layer = w if isinstance(w, tuple) else (w, None)
        shape = a.shape if layer is None else a.shape[1:]
        rows, cols = int(np.prod(shape[:-1])), shape[-1]
        n_blocks = next(nb for nb in range(n_steps, 0, -1)
                        if n_steps % nb == 0 and rows % (nb * BF16_ROWS) == 0)
        r, first = n_steps // n_blocks, 0 if layer is None else layer * n_blocks
        views.append(a.reshape(-1, cols))
        in_specs.append(pl.BlockSpec((rows // n_blocks, cols), lambda i, r=r, first=first: (first + i // r, 0)))
        out_specs.append(pl.BlockSpec((rows // n_blocks, cols), lambda i, r=r: (i // r, 0)))
        out_shapes.append(jax.ShapeDtypeStruct((rows, cols), BF16))
        shapes.append(shape)
    return views, in_specs, out_specs, out_shapes, shapes


def _cast_along(src_refs, dst_refs):
    for src, dst in zip(src_refs, dst_refs):
        dst[...] = src[...].astype(dst.dtype)


def _mixer_kernel(*refs, n_cast, tm, sub, seq, alpha):
    (x_ref, win_ref, convw_ref, vg_ref, vb_ref, ws_ref, bias_ref, outg_ref, wout_ref,
     lng_ref, lnb_ref) = refs[:11]
    o_ref, ch_ref = refs[11 + n_cast], refs[-1]
    _cast_along(refs[11:11 + n_cast], refs[12 + n_cast:12 + 2 * n_cast])
    i = pl.program_id(0)
    cw_, gw_ = CONV_WIDTH, GMLP_WIDTH

    @pl.when(lax.rem(i * tm, seq) == 0)
    def _():
        ch_ref[0:SUBLANES, :] = jnp.zeros((SUBLANES, cw_), F32)

    trow = lax.broadcasted_iota(jnp.int32, (CHUNK, CHUNK), 0)
    tcol = lax.broadcasted_iota(jnp.int32, (CHUNK, CHUNK), 1)
    causal = tcol <= trow
    ws = [jnp.where(causal, ws_ref[h], 0.0).astype(BF16) for h in range(GMLP_HEADS)]
    lane = lax.broadcasted_iota(jnp.int32, (sub, LANES), 1)
    low_half = lane < CONV_HEAD_DIM
    nch = sub // CHUNK

    for r0 in range(0, tm, sub):
        x = x_ref[r0:r0 + sub, :]
        proj = jnp.dot(x.astype(BF16), win_ref[...], preferred_element_type=F32)
        b_gate = proj[:, 0:cw_]
        c_gate = proj[:, cw_:2 * cw_]
        hh = proj[:, 2 * cw_:3 * cw_]
        u = proj[:, 3 * cw_:3 * cw_ + gw_]
        v = proj[:, 3 * cw_ + gw_:3 * cw_ + 2 * gw_]

        c0 = SUBLANES + r0
        ch_ref[c0:c0 + sub, :] = c_gate * hh
        convw = convw_ref[...]
        conv = (convw[0:1, :] * ch_ref[c0 - 2:c0 - 2 + sub, :]
                + convw[1:2, :] * ch_ref[c0 - 1:c0 - 1 + sub, :]
                + convw[2:3, :] * ch_ref[c0:c0 + sub, :])
        y_conv = b_gate * conv

        ug = _gelu(u)
        vn = _layer_norm(_gelu(v), vg_ref[...], vb_ref[...]).astype(BF16)
        zs = []
        for h in range(GMLP_HEADS):
            lo, hi = h * GMLP_HEAD_DIM, (h + 1) * GMLP_HEAD_DIM
            rhs = jnp.concatenate([vn[c * CHUNK:(c + 1) * CHUNK, lo:hi] for c in range(nch)], axis=1)
            zs.append(jnp.dot(ws[h], rhs, preferred_element_type=F32))
        z = jnp.concatenate(
            [jnp.concatenate([zs[h][:, c * GMLP_HEAD_DIM:(c + 1) * GMLP_HEAD_DIM]
                              for h in range(GMLP_HEADS)], axis=1) for c in range(nch)], axis=0)
        bias = jnp.concatenate([bias_ref[...]] * nch, axis=0)
        y_sg = ug * (z + bias)

        parts = []
        for j in range(cw_ // LANES):
            yt = y_conv[:, j * LANES:(j + 1) * LANES]
            sq = yt * yt
            ms_lo = jnp.sum(jnp.where(low_half, sq, 0.0), axis=-1, keepdims=True) * (1.0 / CONV_HEAD_DIM)
            ms_hi = jnp.sum(jnp.where(low_half, 0.0, sq), axis=-1, keepdims=True) * (1.0 / CONV_HEAD_DIM)
            parts.append(yt * jnp.where(low_half, lax.rsqrt(ms_lo + RMS_EPS), lax.rsqrt(ms_hi + RMS_EPS)))
        for h in range(GMLP_HEADS):
            yt = y_sg[:, h * GMLP_HEAD_DIM:(h + 1) * GMLP_HEAD_DIM]
            ms = jnp.mean(yt * yt, axis=-1, keepdims=True)
            parts.append(yt * lax.rsqrt(ms + RMS_EPS))
        y = jnp.concatenate(parts, axis=1) * outg_ref[...]

        mix = jnp.dot(y.astype(BF16), wout_ref[...], preferred_element_type=F32)
        o_ref[r0:r0 + sub, :] = _layer_norm(alpha * x + mix, lng_ref[...], lnb_ref[...])

    ch_ref[0:SUBLANES, :] = ch_ref[tm:tm + SUBLANES, :]


def _mixer_ln(x, w_in, conv_w, v_g, v_b, w_s, bias_t, out_g, w_out, ln_g, ln_b, cast=(), *, seq, alpha):
    n, d = x.shape
    tm, sub = TM_MIX, SUB_MIX
    assert n % tm == 0 and seq % tm == 0 and tm % sub == 0 and sub % CHUNK == 0
    views, cast_in, cast_out, cast_shapes, shapes = _cast_plan(cast, n // tm)
    kern = functools.partial(_mixer_kernel, n_cast=len(cast), tm=tm, sub=sub, seq=seq, alpha=alpha)
    row = pl.BlockSpec((tm, d), lambda i: (i, 0))
    out = pl.pallas_call(
        kern,
        out_shape=[jax.ShapeDtypeStruct((n, d), F32)] + cast_shapes,
        grid=(n // tm,),
        in_specs=[row, _const_spec(w_in.shape), _const_spec(conv_w.shape), _const_spec(v_g.shape),
                  _const_spec(v_b.shape), _const_spec(w_s.shape), _const_spec(bias_t.shape),
                  _const_spec(out_g.shape), _const_spec(w_out.shape), _const_spec(ln_g.shape),
                  _const_spec(ln_b.shape)] + cast_in,
        out_specs=[row] + cast_out,
        scratch_shapes=[pltpu.VMEM((tm + SUBLANES, CONV_WIDTH), F32)],
        compiler_params=_params(("arbitrary",)),
        name="mixer_ln",
    )(x, w_in, conv_w, v_g, v_b, w_s, bias_t, out_g, w_out, ln_g, ln_b, *views)
    return out[0], [o.reshape(shape) for o, shape in zip(out[1:], shapes)]


def _ffn_kernel(*refs, n_cast, sub, alpha):
    x_ref, wg_ref, wu_ref, wd_ref, lng_ref, lnb_ref = refs[:6]
    o_ref = refs[6 + n_cast]
    _cast_along(refs[6:6 + n_cast], refs[7 + n_cast:7 + 2 * n_cast])
    for r0 in range(0, x_ref.shape[0], sub):
        x = x_ref[r0:r0 + sub, :]
        xb = x.astype(BF16)
        g = jnp.dot(xb, wg_ref[...], preferred_element_type=F32)
        u = jnp.dot(xb, wu_ref[...], preferred_element_type=F32)
        h = (_silu(g) * u).astype(BF16)
        ffn = jnp.dot(h, wd_ref[...], preferred_element_type=F32)
        o_ref[r0:r0 + sub, :] = _layer_norm(alpha * x + ffn, lng_ref[...], lnb_ref[...])


def _ffn_ln(x, wg, wu, wd, ln_g, ln_b, cast=(), *, alpha):
    n, d = x.shape
    tm = TM_FFN
    assert n % tm == 0 and tm % SUB_FFN == 0
    views, cast_in, cast_out, cast_shapes, shapes = _cast_plan(cast, n // tm)
    row = pl.BlockSpec((tm, d), lambda i: (i, 0))
    out = pl.pallas_call(
        functools.partial(_ffn_kernel, n_cast=len(cast), sub=SUB_FFN, alpha=alpha),
        out_shape=[jax.ShapeDtypeStruct((n, d), F32)] + cast_shapes,
        grid=(n // tm,),
        in_specs=[row, _const_spec(wg.shape), _const_spec(wu.shape), _const_spec(wd.shape),
                  _const_spec(ln_g.shape), _const_spec(ln_b.shape)] + cast_in,
        out_specs=[row] + cast_out,
        compiler_params=_params(("arbitrary",)),
        name="ffn_ln",
    )(x, wg, wu, wd, ln_g, ln_b, *views)
    return out[0], [o.reshape(shape) for o, shape in zip(out[1:], shapes)]


_M_E0, _M_E1, _M_S0, _M_S1, _M_G0, _M_G1 = range(6)
_R_LSTART, _R_LEN, _R_GOFF, _R_PARTIAL = range(4)
RUN_FIELDS = 4
EXPERT_ROWS = max(N_EXPERTS, BF16_ROWS)


def _router_kernel(x_ref, wrh_ref, wrl_ref, meta_ref, metat_ref, runs_ref, tot_ref, base_ref, *, tm):
    i = pl.program_id(0)

    @pl.when(i == 0)
    def _():
        base_ref[...] = jnp.zeros(base_ref.shape, F32)

    for k in range(x_ref.shape[0] // tm):
        _route_tile(x_ref[k * tm:(k + 1) * tm, :], wrh_ref, wrl_ref, meta_ref.at[k * tm:(k + 1) * tm],
                    metat_ref.at[k * SUBLANES:(k + 1) * SUBLANES],
                    runs_ref.at[k * EXPERT_ROWS:(k + 1) * EXPERT_ROWS], tot_ref, base_ref)


def _route_tile(x, wrh_ref, wrl_ref, meta_ref, metat_ref, runs_ref, tot_ref, base_ref):
    tm = x.shape[0]
    xh = x.astype(BF16)
    xl = (x - xh.astype(F32)).astype(BF16)
    wrh = wrh_ref[...]
    logits = (jnp.dot(xh, wrh, preferred_element_type=F32)
              + jnp.dot(xl, wrh, preferred_element_type=F32)
              + jnp.dot(xh, wrl_ref[...], preferred_element_type=F32))

    er = EXPERT_ROWS
    lg = logits.T[0:er, :]
    ef = lax.broadcasted_iota(jnp.int32, (er, tm), 0).astype(F32)
    neg = jnp.float32(-jnp.inf)
    lg = jnp.where(ef < N_EXPERTS, lg, neg)
    m0 = jnp.max(lg, axis=0, keepdims=True)
    e0 = jnp.min(jnp.where(lg == m0, ef, float(er)), axis=0, keepdims=True)
    lg1 = jnp.where(ef == e0, neg, lg)
    m1 = jnp.max(lg1, axis=0, keepdims=True)
    e1 = jnp.min(jnp.where(lg1 == m1, ef, float(er)), axis=0, keepdims=True)
    t = jnp.exp(m1 - m0)
    g0 = 1.0 / (1.0 + t)
    g1 = t / (1.0 + t)

    oh0 = ef == e0
    oh1 = ef == e1
    cnt = jnp.where(oh0 | oh1, 1.0, 0.0)
    s_i = lax.broadcasted_iota(jnp.int32, (tm, tm), 0)
    t_i = lax.broadcasted_iota(jnp.int32, (tm, tm), 1)
    earlier = jnp.where(s_i < t_i, 1.0, 0.0).astype(BF16)
    rank = jnp.dot(cnt.astype(BF16), earlier, preferred_element_type=F32)

    def align_down(v):
        return jnp.floor(v * (1.0 / RUN_ALIGN)) * RUN_ALIGN

    count = jnp.broadcast_to(jnp.sum(cnt, axis=1, keepdims=True), (er, LANES))
    before = base_ref[...]
    phase = before - align_down(before)
    end = phase + count
    run_len = jnp.where(count > 0, align_down(end + (RUN_ALIGN - 1.0)), 0.0)
    partial = jnp.where((count > 0) & (end != align_down(end)), 1.0, 0.0)
    r_e = lax.broadcasted_iota(jnp.int32, (er, er), 0)
    c_e = lax.broadcasted_iota(jnp.int32, (er, er), 1)
    lower = jnp.where(c_e < r_e, 1.0, 0.0).astype(BF16)
    lstart = jnp.dot(lower, run_len.astype(BF16), preferred_element_type=F32)
    base_ref[...] = before + count
    tot_ref[...] = before + count
    field = lax.broadcasted_iota(jnp.int32, (er, LANES), 1)
    table = jnp.zeros((er, LANES), F32)
    for f, val in ((_R_LSTART, lstart), (_R_LEN, run_len), (_R_GOFF, before - phase), (_R_PARTIAL, partial)):
        table = jnp.where(field == f, val, table)
    runs_ref[...] = table

    slot = rank + (lstart + phase)[:, 0:1]
    s0 = jnp.sum(jnp.where(oh0, slot, 0.0), axis=0, keepdims=True)
    s1 = jnp.sum(jnp.where(oh1, slot, 0.0), axis=0, keepdims=True)
    row = lax.broadcasted_iota(jnp.int32, (SUBLANES, tm), 0)
    meta_t = jnp.zeros((SUBLANES, tm), F32)
    for k, val in ((_M_E0, e0), (_M_E1, e1), (_M_S0, s0), (_M_S1, s1), (_M_G0, g0), (_M_G1, g1)):
        meta_t = jnp.where(row == k, val, meta_t)
    metat_ref[...] = meta_t
    meta_ref[...] = jnp.concatenate([meta_t, jnp.zeros((LANES - SUBLANES, tm), F32)], axis=0).T


def _router(x, wr_hi, wr_lo):
    n, d = x.shape
    tm, k = TM_MOE, ROUTER_TILES
    assert n % (k * tm) == 0
    nt = n // tm
    small = (EXPERT_ROWS, LANES)
    return pl.pallas_call(
        functools.partial(_router_kernel, tm=tm),
        out_shape=(jax.ShapeDtypeStruct((n, LANES), F32), jax.ShapeDtypeStruct((nt * SUBLANES, tm), F32),
                   jax.ShapeDtypeStruct((nt * EXPERT_ROWS, LANES), F32), jax.ShapeDtypeStruct(small, F32)),
        grid=(nt // k,),
        in_specs=[pl.BlockSpec((k * tm, d), lambda i: (i, 0)), _const_spec(wr_hi.shape), _const_spec(wr_lo.shape)],
        out_specs=(pl.BlockSpec((k * tm, LANES), lambda i: (i, 0)), pl.BlockSpec((k * SUBLANES, tm), lambda i: (i, 0)),
                   pl.BlockSpec((k * EXPERT_ROWS, LANES), lambda i: (i, 0)), _const_spec(small)),
        scratch_shapes=[pltpu.VMEM(small, F32)],
        compiler_params=_params(("arbitrary",)),
        name="router",
    )(x, wr_hi, wr_lo)


def _run_copies(tab_ref, tile, local_ref, glob_hbm, sem, *, to_global, start):
    base = tile * (RUN_FIELDS * N_EXPERTS)
    for e in range(N_EXPERTS):
        lstart = tab_ref[base + e]
        length = tab_ref[base + N_EXPERTS + e]
        gstart = tab_ref[base + 2 * N_EXPERTS + e]
        for b in range(RUN_BITS):
            size = RUN_ALIGN << b
            off = (length >> (b + ALIGN_SHIFT + 1)) << (b + ALIGN_SHIFT + 1)

            def piece(size=size, off=off, lstart=lstart, gstart=gstart):
                loc = local_ref.at[pl.ds(pl.multiple_of(lstart + off, RUN_ALIGN), size)]
                glo = glob_hbm.at[pl.ds(pl.multiple_of(gstart + off, RUN_ALIGN), size)]
                cp = pltpu.make_async_copy(loc, glo, sem) if to_global else pltpu.make_async_copy(glo, loc, sem)
                if start:
                    cp.start()
                else:
                    cp.wait()

            pl.when(((length >> (b + ALIGN_SHIFT)) & 1) == 1)(piece)


def _fill_copies(fill_ref, zero_ref, glob_hbm, sem, *, te, n_spare, start):
    def go(cp):
        if start:
            cp.start()
        else:
            cp.wait()

    for e in range(N_EXPERTS):
        tstart = fill_ref[e]
        length = fill_ref[N_EXPERTS + e]
        for b in range(TAIL_BITS):
            size = RUN_ALIGN << b
            off = (length >> (b + ALIGN_SHIFT + 1)) << (b + ALIGN_SHIFT + 1)

            def piece(size=size, off=off, tstart=tstart):
                go(pltpu.make_async_copy(zero_ref.at[pl.ds(0, size)],
                                         glob_hbm.at[pl.ds(pl.multiple_of(tstart + off, RUN_ALIGN), size)], sem))

            pl.when(((length >> (b + ALIGN_SHIFT)) & 1) == 1)(piece)
    spare0 = fill_ref[2 * N_EXPERTS]
    for j in range(n_spare):
        def tile_fill(j=j):
            go(pltpu.make_async_copy(zero_ref, glob_hbm.at[pl.ds(pl.multiple_of(spare0 + j * te, te), te)], sem))

        pl.when(j < fill_ref[2 * N_EXPERTS + 1])(tile_fill)


def _sort_tile(rows, x_ref, meta_ref, meta_t, loc_ref, sel_ref, *, tm):
    d = x_ref.shape[1]
    meta = meta_ref[rows, :]
    s0 = meta_t[_M_S0:_M_S0 + 1, :]
    s1 = meta_t[_M_S1:_M_S1 + 1, :]
    slot = lax.broadcasted_iota(jnp.int32, (SLOTS, tm), 0).astype(F32)
    signed = jnp.where(slot == s0, 1.0, jnp.where(slot == s1, -1.0, 0.0)).astype(BF16)
    onehot = jnp.abs(signed)
    sel_ref[...] = onehot
    loc_ref[:, 0:d] = jnp.dot(onehot, x_ref[rows, :].astype(BF16), preferred_element_type=F32).astype(BF16)

    lane = lax.broadcasted_iota(jnp.int32, (tm, GATE_LANES), 1)
    info = jnp.where(lane == 2 * GATE_PIECES, 1.0, 0.0)
    for k, g in ((0, meta[:, _M_G0:_M_G0 + 1]), (1, meta[:, _M_G1:_M_G1 + 1])):
        rest = g
        for p in range(GATE_PIECES):
            piece = rest.astype(BF16).astype(F32)
            info = jnp.where(lane == k * GATE_PIECES + p, piece, info)
            rest = rest - piece
    loc_ref[:, d:d + GATE_LANES] = jnp.dot(signed, info.astype(BF16), preferred_element_type=F32).astype(BF16)


def _merge_carry(tile, tab_ref, loc_ref, carry_ref):
    base = tile * (RUN_FIELDS * N_EXPERTS)
    for e in range(N_EXPERTS):
        lstart = tab_ref[base + _R_LSTART * N_EXPERTS + e]
        length = tab_ref[base + _R_LEN * N_EXPERTS + e]
        partial = tab_ref[base + _R_PARTIAL * N_EXPERTS + e]

        def merge(e=e, lstart=lstart, length=length, partial=partial):
            first = pl.ds(pl.multiple_of(lstart, RUN_ALIGN), RUN_ALIGN)
            loc_ref[first, :] = loc_ref[first, :] + carry_ref[e]
            last = pl.ds(pl.multiple_of(lstart + length - RUN_ALIGN, RUN_ALIGN), RUN_ALIGN)

            @pl.when(partial == 1)
            def _():
                carry_ref[e] = loc_ref[last, :]

            @pl.when(partial == 0)
            def _():
                carry_ref[e] = jnp.zeros(carry_ref.shape[1:], BF16)

        pl.when(length > 0)(merge)


def _dispatch_kernel(tab_ref, fill_ref, x_ref, meta_ref, metat_ref, xs_hbm, sel_ref, *scratch, tm, te, n_spare):
    *loc_refs, zero_ref, carry_ref, sem, zsem = scratch
    i = pl.program_id(0)

    @pl.when(i == 0)
    def _():
        zero_ref[...] = jnp.zeros(zero_ref.shape, BF16)
        carry_ref[...] = jnp.zeros(carry_ref.shape, BF16)
        _fill_copies(fill_ref, zero_ref, xs_hbm, zsem, te=te, n_spare=n_spare, start=True)

    n_tiles = x_ref.shape[0] // tm
    for k in range(n_tiles):
        tile = n_tiles * i + k
        _sort_tile(pl.ds(k * tm, tm), x_ref, meta_ref, metat_ref[k * SUBLANES:(k + 1) * SUBLANES, :],
                   loc_refs[k], sel_ref.at[k * SLOTS:(k + 1) * SLOTS], tm=tm)
        if k > 0:
            _run_copies(tab_ref, tile - 1, loc_refs[k - 1], xs_hbm, sem, to_global=True, start=False)
        _merge_carry(tile, tab_ref, loc_refs[k], carry_ref)
        _run_copies(tab_ref, tile, loc_refs[k], xs_hbm, sem, to_global=True, start=True)
    _run_copies(tab_ref, n_tiles * i + n_tiles - 1, loc_refs[-1], xs_hbm, sem, to_global=True, start=False)

    @pl.when(i == pl.num_programs(0) - 1)
    def _():
        _fill_copies(fill_ref, zero_ref, xs_hbm, zsem, te=te, n_spare=n_spare, start=False)


def _dispatch(x, meta, meta_t, run_tab, fill_tab, n_rows, n_spare):
    n, d = x.shape
    tm, te, k = TM_MOE, TM_EXPERT, MOE_TILES
    width = d + GATE_LANES
    grid_spec = pltpu.PrefetchScalarGridSpec(
        num_scalar_prefetch=2,
        grid=(n // (k * tm),),
        in_specs=[pl.BlockSpec((k * tm, d), lambda i, *_: (i, 0)),
                  pl.BlockSpec((k * tm, LANES), lambda i, *_: (i, 0)),
                  pl.BlockSpec((k * SUBLANES, tm), lambda i, *_: (i, 0))],
        out_specs=[pl.BlockSpec(memory_space=pl.ANY), pl.BlockSpec((k * SLOTS, tm), lambda i, *_: (i, 0))],
        scratch_shapes=[pltpu.VMEM((SLOTS, width), BF16)] * k + [pltpu.VMEM((te, width), BF16),
                        pltpu.VMEM((N_EXPERTS, RUN_ALIGN, width), BF16),
                        pltpu.SemaphoreType.DMA, pltpu.SemaphoreType.DMA],
    )
    return pl.pallas_call(
        functools.partial(_dispatch_kernel, tm=tm, te=te, n_spare=n_spare),
        out_shape=[jax.ShapeDtypeStruct((n_rows, width), BF16), jax.ShapeDtypeStruct((n // tm * SLOTS, tm), BF16)],
        grid_spec=grid_spec,
        compiler_params=_params(("arbitrary",)),
        name="dispatch",
    )(run_tab, fill_tab, x, meta, meta_t)


def _slot_gate(info):
    lane = lax.broadcasted_iota(jnp.int32, info.shape, 1)
    first = jnp.sum(jnp.where(lane < GATE_PIECES, info, 0.0), axis=-1, keepdims=True)
    second = jnp.sum(jnp.where((lane >= GATE_PIECES) & (lane < 2 * GATE_PIECES), info, 0.0), axis=-1, keepdims=True)
    sign = jnp.sum(jnp.where(lane == 2 * GATE_PIECES, info, 0.0), axis=-1, keepdims=True)
    return jnp.where(sign > 0, first, -second)


def _experts_kernel(tile_ref, expert_ref, rows_ref, xs_ref, wg_ref, wu_ref, wd_ref, ys_ref, acc_ref, *, sub):
    i = pl.program_id(0)
    f = pl.program_id(1)
    te, d = ys_ref.shape
    rows = rows_ref[i]

    @pl.when((rows == 0) & (f == 0))
    def _():
        ys_ref[...] = jnp.zeros(ys_ref.shape, ys_ref.dtype)

    def partial_out(r0):
        xb = xs_ref[r0:r0 + sub, 0:d]
        g = jnp.dot(xb, wg_ref[0], preferred_element_type=F32)
        u = jnp.dot(xb, wu_ref[0], preferred_element_type=F32)
        h = (_silu(g) * u).astype(BF16)
        return jnp.dot(h, wd_ref[0], preferred_element_type=F32)

    def first_chunk(n_rows):
        for r0 in range(0, n_rows, sub):
            acc_ref[r0:r0 + sub, :] = partial_out(r0)

    def last_chunk(n_rows):
        for r0 in range(0, n_rows, sub):
            gate = _slot_gate(xs_ref[r0:r0 + sub, d:d + GATE_LANES].astype(F32))
            ys_ref[r0:r0 + sub, :] = ((acc_ref[r0:r0 + sub, :] + partial_out(r0)) * gate).astype(ys_ref.dtype)
        if n_rows < te:
            ys_ref[n_rows:te, :] = jnp.zeros((te - n_rows, d), ys_ref.dtype)

    for nb in range(1, te // sub + 1):
        fits = (rows > (nb - 1) * sub) & (rows <= nb * sub)
        pl.when(fits & (f == 0))(functools.partial(first_chunk, nb * sub))
        pl.when(fits & (f == 1))(functools.partial(last_chunk, nb * sub))


def _experts(xs, we_gate, we_up, we_down, tile_idx, tile_expert, tile_rows):
    n_rows, width = xs.shape
    d = width - GATE_LANES
    te = TM_EXPERT
    ff = we_gate.shape[-1]
    fc = ff // FF_CHUNKS
    assert n_rows % te == 0 and ff % FF_CHUNKS == 0 and fc % LANES == 0 and FF_CHUNKS == 2
    n_tiles = n_rows // te
    last = FF_CHUNKS - 1
    grid_spec = pltpu.PrefetchScalarGridSpec(
        num_scalar_prefetch=3,
        grid=(n_tiles, FF_CHUNKS),
        in_specs=[
            pl.BlockSpec((te, width), lambda i, f, tile, ex, rows: (tile[i], 0)),
            pl.BlockSpec((1, d, fc), lambda i, f, tile, ex, rows: (ex[i], 0, jnp.where(rows[i] > 0, f, last))),
            pl.BlockSpec((1, d, fc), lambda i, f, tile, ex, rows: (ex[i], 0, jnp.where(rows[i] > 0, f, last))),
            pl.BlockSpec((1, fc, d), lambda i, f, tile, ex, rows: (ex[i], jnp.where(rows[i] > 0, f, last), 0)),
        ],
        out_specs=pl.BlockSpec((te, d), lambda i, f, tile, ex, rows: (i, 0)),
        scratch_shapes=[pltpu.VMEM((te, d), F32)],
    )
    return pl.pallas_call(
        functools.partial(_experts_kernel, sub=SUB_EXPERT),
        out_shape=jax.ShapeDtypeStruct((n_rows, d), BF16),
        grid_spec=grid_spec,
        compiler_params=_params(("arbitrary", "arbitrary")),
        name="experts",
    )(tile_idx, tile_expert, tile_rows, xs, we_gate, we_up, we_down)


def _combine_kernel(tab_ref, x_ref, sel_ref, lng_ref, lnb_ref, ys_hbm, o_ref, loc_ref, sem, *, tm, alpha):
    i = pl.program_id(0)
    n_steps = pl.num_programs(0)
    n_tiles = x_ref.shape[0] // tm
    slot = lax.rem(i, 2)

    def copies(step, s, start):
        for k in range(n_tiles):
            _run_copies(tab_ref, n_tiles * step + k, loc_ref.at[s, k], ys_hbm, sem.at[s], to_global=False,
                        start=start)

    @pl.when(i == 0)
    def _():
        loc_ref[...] = jnp.zeros(loc_ref.shape, loc_ref.dtype)
        copies(0, 0, True)

    @pl.when(i + 1 < n_steps)
    def _():
        copies(i + 1, 1 - slot, True)

    copies(i, slot, False)

    for k in range(n_tiles):
        rows = pl.ds(k * tm, tm)
        moe = lax.dot_general(sel_ref[k * SLOTS:(k + 1) * SLOTS, :], loc_ref[slot, k],
                              (((0,), (0,)), ((), ())), preferred_element_type=F32)
        o_ref[rows, :] = _layer_norm(alpha * x_ref[rows, :] + moe, lng_ref[...], lnb_ref[...])


def _combine_ln(x, sel, run_tab, ys, ln_g, ln_b, *, alpha):
    n, d = x.shape
    tm, k = TM_MOE, MOE_TILES
    n_steps = n // (k * tm)
    row = pl.BlockSpec((k * tm, d), lambda i, *_: (i, 0))
    grid_spec = pltpu.PrefetchScalarGridSpec(
        num_scalar_prefetch=1,
        grid=(n_steps,),
        in_specs=[row, pl.BlockSpec((k * SLOTS, tm), lambda i, *_: (i, 0)),
                  pl.BlockSpec(ln_g.shape, lambda i, *_: (0, 0)), pl.BlockSpec(ln_b.shape, lambda i, *_: (0, 0)),
                  pl.BlockSpec(memory_space=pl.ANY)],
        out_specs=row,
        scratch_shapes=[pltpu.VMEM((2, k, SLOTS, d), BF16), pltpu.SemaphoreType.DMA((2,))],
    )
    return pl.pallas_call(
        functools.partial(_combine_kernel, tm=tm, alpha=alpha),
        out_shape=jax.ShapeDtypeStruct((n, d), F32),
        grid_spec=grid_spec,
        compiler_params=_params(("arbitrary",)),
        name="combine_ln",
    )(run_tab, x, sel, ln_g, ln_b, ys)


def _moe_ln(x, w_router, we_gate, we_up, we_down, ln_g, ln_b, *, alpha):
    n, d = x.shape
    tm, te = TM_MOE, TM_EXPERT
    assert n % (MOE_TILES * tm) == 0 and te % SUB_EXPERT == 0
    nt = n // tm
    wr = jnp.pad(w_router, ((0, 0), (0, LANES - N_EXPERTS)))
    wr_hi = wr.astype(BF16)
    wr_lo = (wr - wr_hi.astype(F32)).astype(BF16)
    meta, meta_t, runs, totals = _router(x, wr_hi, wr_lo)

    counts = totals[:N_EXPERTS, 0].astype(jnp.int32)
    padded = (counts + RUN_ALIGN - 1) // RUN_ALIGN * RUN_ALIGN
    tiles_e = (padded + te - 1) // te
    tile_end = jnp.cumsum(tiles_e)
    region = (tile_end - tiles_e) * te
    n_used = tile_end[-1]
    max_rows = TOP_K * n + N_EXPERTS * (te - 1)
    n_tiles = -(-max_rows // te)
    n_spare = n_tiles - (TOP_K * n) // te
    runs_i = runs.reshape(nt, EXPERT_ROWS, LANES)[:, :N_EXPERTS, :RUN_FIELDS].astype(jnp.int32)
    runs_i = jnp.swapaxes(runs_i, 1, 2)
    runs_i = runs_i.at[:, _R_GOFF, :].add(region)
    run_tab = runs_i.reshape(-1)
    fill_tab = jnp.concatenate([region + padded, tiles_e * te - padded,
                                (n_used * te)[None], (n_tiles - n_used)[None]]).astype(jnp.int32)
    tiles = jnp.arange(n_tiles, dtype=jnp.int32)
    j = jnp.minimum(tiles, n_used - 1)
    tile_expert = jnp.minimum(jnp.sum(j[:, None] >= tile_end[None, :], axis=-1), N_EXPERTS - 1).astype(jnp.int32)
    region_end = jnp.sum(jnp.where(tile_expert[:, None] == jnp.arange(N_EXPERTS), region + padded, 0), axis=-1)
    tile_rows = jnp.where(tiles < n_used, jnp.minimum(region_end - tiles * te, te), 0).astype(jnp.int32)

    xs, sel = _dispatch(x, meta, meta_t, run_tab, fill_tab, n_tiles * te, n_spare)
    ys = _experts(xs, we_gate, we_up, we_down, j, tile_expert, tile_rows)
    return _combine_ln(x, sel, run_tab, ys, ln_g, ln_b, alpha=alpha)


def kernel(x, w_in, conv_w, v_g, v_b, w_s, b_s, out_g, w_out, ln1_g, ln1_b, ln2_g, ln2_b,
           w_gate, w_up, w_down, w_router, we_gate, we_up, we_down):
    bsz, seq, d = x.shape
    depth = w_in.shape[0]
    alpha = float((2 * depth) ** 0.25)
    h = x.reshape(bsz * seq, d)
    row = lambda a: a.reshape(1, -1)
    mix_w = [w_in[0].astype(BF16), w_out[0].astype(BF16)]
    moe_w = [None, None, None]
    for i in range(depth):
        j = i // 2
        dense, last = i % 2 == 0, i + 1 == depth
        if dense:
            cast = [w_gate[j], w_up[j], w_down[j]] + ([] if last else [we_down[j]])
        else:
            cast = [we_up[j]]
        bias_t = jnp.repeat(b_s[i].T, GMLP_HEAD_DIM, axis=1)
        h, cast_w = _mixer_ln(h, mix_w[0], conv_w[i], row(v_g[i]), row(v_b[i]), w_s[i], bias_t,
                              row(out_g[i]), mix_w[1], row(ln1_g[i]), row(ln1_b[i]), cast=cast,
                              seq=seq, alpha=alpha)
        if dense:
            ahead = [] if last else [(w_in, i + 1), (w_out, i + 1), we_gate[j]]
            h, ahead_w = _ffn_ln(h, *cast_w[:3], row(ln2_g[i]), row(ln2_b[i]), cast=ahead, alpha=alpha)
            if not last:
                mix_w, moe_w = ahead_w[:2], [ahead_w[2], None, cast_w[3]]
        else:
            moe_w[1] = cast_w[0]
            h = _moe_ln(h, w_router[j], *moe_w, row(ln2_g[i]), row(ln2_b[i]), alpha=alpha)
            moe_w = [None, None, None]
            if not last:
                mix_w = [w_in[i + 1].astype(BF16), w_out[i + 1].astype(BF16)]
    return h.reshape(bsz, seq, d)
```

```python
import functools

import numpy as np
import jax
import jax.numpy as jnp
from jax import lax
from jax.experimental import pallas as pl
from jax.experimental.pallas import tpu as pltpu

F32 = jnp.float32
BF16 = jnp.bfloat16

CONV_WIDTH = 512
CONV_HEAD_DIM = 64
GMLP_WIDTH = 512
GMLP_HEADS = 4
GMLP_HEAD_DIM = 128
CHUNK = 128
N_EXPERTS = 8
TOP_K = 2
LN_EPS = 1e-5
RMS_EPS = 1e-6

LANES = 128
SUBLANES = 8
BF16_ROWS = 16
VMEM_LIMIT_BYTES = 56 * 1024 * 1024

TM_MIX = 1024
SUB_MIX = 512
TM_FFN = 1024
SUB_FFN = 512
TM_MOE = 512
MOE_TILES = 2
ROUTER_TILES = 1
TM_EXPERT = 1024
SUB_EXPERT = 512
STEP_EXPERT = 256
FF_CHUNKS = 2

RUN_ALIGN = BF16_ROWS
ALIGN_SHIFT = RUN_ALIGN.bit_length() - 1
MAX_RUN = TM_MOE + RUN_ALIGN
RUN_BITS = (MAX_RUN // RUN_ALIGN).bit_length()
SLOTS = -(-(TOP_K * TM_MOE + N_EXPERTS * 2 * (RUN_ALIGN - 1)) // LANES) * LANES
GATE_LANES = LANES
GATE_PIECES = 3
TAIL_BITS = (TM_EXPERT // RUN_ALIGN - 1).bit_length()


def _layer_norm(r, g, b):
    mu = jnp.mean(r, axis=-1, keepdims=True)
    d = r - mu
    var = jnp.mean(d * d, axis=-1, keepdims=True)
    return d * lax.rsqrt(var + LN_EPS) * g + b


def _gelu(x):
    return 0.5 * x * (1.0 + lax.erf(x * (2.0 ** -0.5)))


def _silu(x):
    return x * (1.0 / (1.0 + jnp.exp(-x)))


def _params(semantics):
    return pltpu.CompilerParams(dimension_semantics=semantics, vmem_limit_bytes=VMEM_LIMIT_BYTES)


def _const_spec(shape):
    nd = len(shape)
    return pl.BlockSpec(shape, lambda *_: (0,) * nd)


def _cast_plan(weights, n_steps):
    views, in_specs, out_specs, out_shapes, shapes = [], [], [], [], []
    for w in weights:
        a, layer = w if isinstance(w, tuple) else (w, None)
        shape = a.shape if layer is None else a.shape[1:]
        rows, cols = int(np.prod(shape[:-1])), shape[-1]
        n_blocks = next(nb for nb in range(n_steps, 0, -1)
                        if n_steps % nb == 0 and rows % (nb * BF16_ROWS) == 0)
        r, first = n_steps // n_blocks, 0 if layer is None else layer * n_blocks
        views.append(a.reshape(-1, cols))
        in_specs.append(pl.BlockSpec((rows // n_blocks, cols), lambda i, r=r, first=first: (first + i // r, 0)))
        out_specs.append(pl.BlockSpec((rows // n_blocks, cols), lambda i, r=r: (i // r, 0)))
        out_shapes.append(jax.ShapeDtypeStruct((rows, cols), BF16))
        shapes.append(shape)
    return views, in_specs, out_specs, out_shapes, shapes


def _cast_along(src_refs, dst_refs):
    for src, dst in zip(src_refs, dst_refs):
        dst[...] = src[...].astype(dst.dtype)


def _mixer_kernel(*refs, n_cast, tm, sub, seq, alpha):
    (x_ref, win_ref, convw_ref, vg_ref, vb_ref, ws_ref, bias_ref, outg_ref, wout_ref,
     lng_ref, lnb_ref) = refs[:11]
    o_ref, ch_ref = refs[11 + n_cast], refs[-1]
    _cast_along(refs[11:11 + n_cast], refs[12 + n_cast:12 + 2 * n_cast])
    i = pl.program_id(0)
    cw_, gw_ = CONV_WIDTH, GMLP_WIDTH

    @pl.when(lax.rem(i * tm, seq) == 0)
    def _():
        ch_ref[0:SUBLANES, :] = jnp.zeros((SUBLANES, cw_), F32)

    trow = lax.broadcasted_iota(jnp.int32, (CHUNK, CHUNK), 0)
    tcol = lax.broadcasted_iota(jnp.int32, (CHUNK, CHUNK), 1)
    causal = tcol <= trow
    ws = [jnp.where(causal, ws_ref[h], 0.0).astype(BF16) for h in range(GMLP_HEADS)]
    lane = lax.broadcasted_iota(jnp.int32, (sub, LANES), 1)
    low_half = lane < CONV_HEAD_DIM
    nch = sub // CHUNK

    for r0 in range(0, tm, sub):
        x = x_ref[r0:r0 + sub, :]
        proj = jnp.dot(x.astype(BF16), win_ref[...], preferred_element_type=F32)
        b_gate = proj[:, 0:cw_]
        c_gate = proj[:, cw_:2 * cw_]
        hh = proj[:, 2 * cw_:3 * cw_]
        u = proj[:, 3 * cw_:3 * cw_ + gw_]
        v = proj[:, 3 * cw_ + gw_:3 * cw_ + 2 * gw_]

        c0 = SUBLANES + r0
        ch_ref[c0:c0 + sub, :] = c_gate * hh
        convw = convw_ref[...]
        conv = (convw[0:1, :] * ch_ref[c0 - 2:c0 - 2 + sub, :]
                + convw[1:2, :] * ch_ref[c0 - 1:c0 - 1 + sub, :]
                + convw[2:3, :] * ch_ref[c0:c0 + sub, :])
        y_conv = b_gate * conv

        ug = _gelu(u)
        vn = _layer_norm(_gelu(v), vg_ref[...], vb_ref[...]).astype(BF16)
        zs = []
        for h in range(GMLP_HEADS):
            lo, hi = h * GMLP_HEAD_DIM, (h + 1) * GMLP_HEAD_DIM
            rhs = jnp.concatenate([vn[c * CHUNK:(c + 1) * CHUNK, lo:hi] for c in range(nch)], axis=1)
            zs.append(jnp.dot(ws[h], rhs, preferred_element_type=F32))
        z = jnp.concatenate(
            [jnp.concatenate([zs[h][:, c * GMLP_HEAD_DIM:(c + 1) * GMLP_HEAD_DIM]
                              for h in range(GMLP_HEADS)], axis=1) for c in range(nch)], axis=0)
        bias = jnp.concatenate([bias_ref[...]] * nch, axis=0)
        y_sg = ug * (z + bias)

        parts = []
        for j in range(cw_ // LANES):
            yt = y_conv[:, j * LANES:(j + 1) * LANES]
            sq = yt * yt
            ms_lo = jnp.sum(jnp.where(low_half, sq, 0.0), axis=-1, keepdims=True) * (1.0 / CONV_HEAD_DIM)
            ms_hi = jnp.sum(jnp.where(low_half, 0.0, sq), axis=-1, keepdims=True) * (1.0 / CONV_HEAD_DIM)
            parts.append(yt * jnp.where(low_half, lax.rsqrt(ms_lo + RMS_EPS), lax.rsqrt(ms_hi + RMS_EPS)))
        for h in range(GMLP_HEADS):
            yt = y_sg[:, h * GMLP_HEAD_DIM:(h + 1) * GMLP_HEAD_DIM]
            ms = jnp.mean(yt * yt, axis=-1, keepdims=True)
            parts.append(yt * lax.rsqrt(ms + RMS_EPS))
        y = jnp.concatenate(parts, axis=1) * outg_ref[...]

        mix = jnp.dot(y.astype(BF16), wout_ref[...], preferred_element_type=F32)
        o_ref[r0:r0 + sub, :] = _layer_norm(alpha * x + mix, lng_ref[...], lnb_ref[...])

    ch_ref[0:SUBLANES, :] = ch_ref[tm:tm + SUBLANES, :]


def _mixer_ln(x, w_in, conv_w, v_g, v_b, w_s, bias_t, out_g, w_out, ln_g, ln_b, cast=(), *, seq, alpha):
    n, d = x.shape
    tm, sub = TM_MIX, SUB_MIX
    assert n % tm == 0 and seq % tm == 0 and tm % sub == 0 and sub % CHUNK == 0
    views, cast_in, cast_out, cast_shapes, shapes = _cast_plan(cast, n // tm)
    kern = functools.partial(_mixer_kernel, n_cast=len(cast), tm=tm, sub=sub, seq=seq, alpha=alpha)
    row = pl.BlockSpec((tm, d), lambda i: (i, 0))
    out = pl.pallas_call(
        kern,
        out_shape=[jax.ShapeDtypeStruct((n, d), F32)] + cast_shapes,
        grid=(n // tm,),
        in_specs=[row, _const_spec(w_in.shape), _const_spec(conv_w.shape), _const_spec(v_g.shape),
                  _const_spec(v_b.shape), _const_spec(w_s.shape), _const_spec(bias_t.shape),
                  _const_spec(out_g.shape), _const_spec(w_out.shape), _const_spec(ln_g.shape),
                  _const_spec(ln_b.shape)] + cast_in,
        out_specs=[row] + cast_out,
        scratch_shapes=[pltpu.VMEM((tm + SUBLANES, CONV_WIDTH), F32)],
        compiler_params=_params(("arbitrary",)),
        name="mixer_ln",
    )(x, w_in, conv_w, v_g, v_b, w_s, bias_t, out_g, w_out, ln_g, ln_b, *views)
    return out[0], [o.reshape(shape) for o, shape in zip(out[1:], shapes)]


def _ffn_kernel(*refs, n_cast, sub, alpha):
    x_ref, wg_ref, wu_ref, wd_ref, lng_ref, lnb_ref = refs[:6]
    o_ref = refs[6 + n_cast]
    _cast_along(refs[6:6 + n_cast], refs[7 + n_cast:7 + 2 * n_cast])
    for r0 in range(0, x_ref.shape[0], sub):
        x = x_ref[r0:r0 + sub, :]
        xb = x.astype(BF16)
        g = jnp.dot(xb, wg_ref[...], preferred_element_type=F32)
        u = jnp.dot(xb, wu_ref[...], preferred_element_type=F32)
        h = (_silu(g) * u).astype(BF16)
        ffn = jnp.dot(h, wd_ref[...], preferred_element_type=F32)
        o_ref[r0:r0 + sub, :] = _layer_norm(alpha * x + ffn, lng_ref[...], lnb_ref[...])


def _ffn_ln(x, wg, wu, wd, ln_g, ln_b, cast=(), *, alpha):
    n, d = x.shape
    tm = TM_FFN
    assert n % tm == 0 and tm % SUB_FFN == 0
    views, cast_in, cast_out, cast_shapes, shapes = _cast_plan(cast, n // tm)
    row = pl.BlockSpec((tm, d), lambda i: (i, 0))
    out = pl.pallas_call(
        functools.partial(_ffn_kernel, n_cast=len(cast), sub=SUB_FFN, alpha=alpha),
        out_shape=[jax.ShapeDtypeStruct((n, d), F32)] + cast_shapes,
        grid=(n // tm,),
        in_specs=[row, _const_spec(wg.shape), _const_spec(wu.shape), _const_spec(wd.shape),
                  _const_spec(ln_g.shape), _const_spec(ln_b.shape)] + cast_in,
        out_specs=[row] + cast_out,
        compiler_params=_params(("arbitrary",)),
        name="ffn_ln",
    )(x, wg, wu, wd, ln_g, ln_b, *views)
    return out[0], [o.reshape(shape) for o, shape in zip(out[1:], shapes)]


_M_E0, _M_E1, _M_S0, _M_S1, _M_G0, _M_G1 = range(6)
_R_LSTART, _R_LEN, _R_GOFF, _R_PARTIAL = range(4)
RUN_FIELDS = 4
EXPERT_ROWS = max(N_EXPERTS, BF16_ROWS)


def _router_kernel(x_ref, wrh_ref, wrl_ref, meta_ref, metat_ref, runs_ref, tot_ref, base_ref, *, tm):
    i = pl.program_id(0)

    @pl.when(i == 0)
    def _():
        base_ref[...] = jnp.zeros(base_ref.shape, F32)

    for k in range(x_ref.shape[0] // tm):
        _route_tile(x_ref[k * tm:(k + 1) * tm, :], wrh_ref, wrl_ref, meta_ref.at[k * tm:(k + 1) * tm],
                    metat_ref.at[k * SUBLANES:(k + 1) * SUBLANES],
                    runs_ref.at[k * EXPERT_ROWS:(k + 1) * EXPERT_ROWS], tot_ref, base_ref)


def _route_tile(x, wrh_ref, wrl_ref, meta_ref, metat_ref, runs_ref, tot_ref, base_ref):
    tm = x.shape[0]
    xh = x.astype(BF16)
    xl = (x - xh.astype(F32)).astype(BF16)
    wrh = wrh_ref[...]
    logits = (jnp.dot(xh, wrh, preferred_element_type=F32)
              + jnp.dot(xl, wrh, preferred_element_type=F32)
              + jnp.dot(xh, wrl_ref[...], preferred_element_type=F32))

    er = EXPERT_ROWS
    lg = logits.T[0:er, :]
    ef = lax.broadcasted_iota(jnp.int32, (er, tm), 0).astype(F32)
    neg = jnp.float32(-jnp.inf)
    lg = jnp.where(ef < N_EXPERTS, lg, neg)
    m0 = jnp.max(lg, axis=0, keepdims=True)
    e0 = jnp.min(jnp.where(lg == m0, ef, float(er)), axis=0, keepdims=True)
    lg1 = jnp.where(ef == e0, neg, lg)
    m1 = jnp.max(lg1, axis=0, keepdims=True)
    e1 = jnp.min(jnp.where(lg1 == m1, ef, float(er)), axis=0, keepdims=True)
    t = jnp.exp(m1 - m0)
    g0 = 1.0 / (1.0 + t)
    g1 = t / (1.0 + t)

    oh0 = ef == e0
    oh1 = ef == e1
    cnt = jnp.where(oh0 | oh1, 1.0, 0.0)
    s_i = lax.broadcasted_iota(jnp.int32, (tm, tm), 0)
    t_i = lax.broadcasted_iota(jnp.int32, (tm, tm), 1)
    earlier = jnp.where(s_i < t_i, 1.0, 0.0).astype(BF16)
    rank = jnp.dot(cnt.astype(BF16), earlier, preferred_element_type=F32)

    def align_down(v):
        return jnp.floor(v * (1.0 / RUN_ALIGN)) * RUN_ALIGN

    count = jnp.broadcast_to(jnp.sum(cnt, axis=1, keepdims=True), (er, LANES))
    before = base_ref[...]
    phase = before - align_down(before)
    end = phase + count
    run_len = jnp.where(count > 0, align_down(end + (RUN_ALIGN - 1.0)), 0.0)
    partial = jnp.where((count > 0) & (end != align_down(end)), 1.0, 0.0)
    r_e = lax.broadcasted_iota(jnp.int32, (er, er), 0)
    c_e = lax.broadcasted_iota(jnp.int32, (er, er), 1)
    lower = jnp.where(c_e < r_e, 1.0, 0.0).astype(BF16)
    lstart = jnp.dot(lower, run_len.astype(BF16), preferred_element_type=F32)
    base_ref[...] = before + count
    tot_ref[...] = before + count
    field = lax.broadcasted_iota(jnp.int32, (er, LANES), 1)
    table = jnp.zeros((er, LANES), F32)
    for f, val in ((_R_LSTART, lstart), (_R_LEN, run_len), (_R_GOFF, before - phase), (_R_PARTIAL, partial)):
        table = jnp.where(field == f, val, table)
    runs_ref[...] = table

    slot = rank + (lstart + phase)[:, 0:1]
    s0 = jnp.sum(jnp.where(oh0, slot, 0.0), axis=0, keepdims=True)
    s1 = jnp.sum(jnp.where(oh1, slot, 0.0), axis=0, keepdims=True)
    row = lax.broadcasted_iota(jnp.int32, (SUBLANES, tm), 0)
    meta_t = jnp.zeros((SUBLANES, tm), F32)
    for k, val in ((_M_E0, e0), (_M_E1, e1), (_M_S0, s0), (_M_S1, s1), (_M_G0, g0), (_M_G1, g1)):
        meta_t = jnp.where(row == k, val, meta_t)
    metat_ref[...] = meta_t
    meta_ref[...] = jnp.concatenate([meta_t, jnp.zeros((LANES - SUBLANES, tm), F32)], axis=0).T


def _router(x, wr_hi, wr_lo):
    n, d = x.shape
    tm, k = TM_MOE, ROUTER_TILES
    assert n % (k * tm) == 0
    nt = n // tm
    small = (EXPERT_ROWS, LANES)
    return pl.pallas_call(
        functools.partial(_router_kernel, tm=tm),
        out_shape=(jax.ShapeDtypeStruct((n, LANES), F32), jax.ShapeDtypeStruct((nt * SUBLANES, tm), F32),
                   jax.ShapeDtypeStruct((nt * EXPERT_ROWS, LANES), F32), jax.ShapeDtypeStruct(small, F32)),
        grid=(nt // k,),
        in_specs=[pl.BlockSpec((k * tm, d), lambda i: (i, 0)), _const_spec(wr_hi.shape), _const_spec(wr_lo.shape)],
        out_specs=(pl.BlockSpec((k * tm, LANES), lambda i: (i, 0)), pl.BlockSpec((k * SUBLANES, tm), lambda i: (i, 0)),
                   pl.BlockSpec((k * EXPERT_ROWS, LANES), lambda i: (i, 0)), _const_spec(small)),
        scratch_shapes=[pltpu.VMEM(small, F32)],
        compiler_params=_params(("arbitrary",)),
        name="router",
    )(x, wr_hi, wr_lo)


def _run_copies(tab_ref, tile, local_ref, glob_hbm, sem, *, to_global, start):
    base = tile * (RUN_FIELDS * N_EXPERTS)
    for e in range(N_EXPERTS):
        lstart = tab_ref[base + e]
        length = tab_ref[base + N_EXPERTS + e]
        gstart = tab_ref[base + 2 * N_EXPERTS + e]
        for b in range(RUN_BITS):
            size = RUN_ALIGN << b
            off = (length >> (b + ALIGN_SHIFT + 1)) << (b + ALIGN_SHIFT + 1)

            def piece(size=size, off=off, lstart=lstart, gstart=gstart):
                loc = local_ref.at[pl.ds(pl.multiple_of(lstart + off, RUN_ALIGN), size)]
                glo = glob_hbm.at[pl.ds(pl.multiple_of(gstart + off, RUN_ALIGN), size)]
                cp = pltpu.make_async_copy(loc, glo, sem) if to_global else pltpu.make_async_copy(glo, loc, sem)
                if start:
                    cp.start()
                else:
                    cp.wait()

            pl.when(((length >> (b + ALIGN_SHIFT)) & 1) == 1)(piece)


def _fill_copies(fill_ref, zero_ref, glob_hbm, sem, *, te, n_spare, start):
    def go(cp):
        if start:
            cp.start()
        else:
            cp.wait()

    for e in range(N_EXPERTS):
        tstart = fill_ref[e]
        length = fill_ref[N_EXPERTS + e]
        for b in range(TAIL_BITS):
            size = RUN_ALIGN << b
            off = (length >> (b + ALIGN_SHIFT + 1)) << (b + ALIGN_SHIFT + 1)

            def piece(size=size, off=off, tstart=tstart):
                go(pltpu.make_async_copy(zero_ref.at[pl.ds(0, size)],
                                         glob_hbm.at[pl.ds(pl.multiple_of(tstart + off, RUN_ALIGN), size)], sem))

            pl.when(((length >> (b + ALIGN_SHIFT)) & 1) == 1)(piece)
    spare0 = fill_ref[2 * N_EXPERTS]
    for j in range(n_spare):
        def tile_fill(j=j):
            go(pltpu.make_async_copy(zero_ref, glob_hbm.at[pl.ds(pl.multiple_of(spare0 + j * te, te), te)], sem))

        pl.when(j < fill_ref[2 * N_EXPERTS + 1])(tile_fill)


def _sort_tile(rows, x_ref, meta_ref, meta_t, loc_ref, *, tm):
    d = x_ref.shape[1]
    meta = meta_ref[rows, :]
    s0 = meta_t[_M_S0:_M_S0 + 1, :]
    s1 = meta_t[_M_S1:_M_S1 + 1, :]
    slot = lax.broadcasted_iota(jnp.int32, (SLOTS, tm), 0).astype(F32)
    signed = jnp.where(slot == s0, 1.0, jnp.where(slot == s1, -1.0, 0.0)).astype(BF16)
    loc_ref[:, 0:d] = jnp.dot(jnp.abs(signed), x_ref[rows, :].astype(BF16),
                              preferred_element_type=F32).astype(BF16)

    lane = lax.broadcasted_iota(jnp.int32, (tm, GATE_LANES), 1)
    info = jnp.where(lane == 2 * GATE_PIECES, 1.0, 0.0)
    for k, g in ((0, meta[:, _M_G0:_M_G0 + 1]), (1, meta[:, _M_G1:_M_G1 + 1])):
        rest = g
        for p in range(GATE_PIECES):
            piece = rest.astype(BF16).astype(F32)
            info = jnp.where(lane == k * GATE_PIECES + p, piece, info)
            rest = rest - piece
    loc_ref[:, d:d + GATE_LANES] = jnp.dot(signed, info.astype(BF16), preferred_element_type=F32).astype(BF16)


def _merge_carry(tile, tab_ref, loc_ref, carry_ref):
    base = tile * (RUN_FIELDS * N_EXPERTS)
    for e in range(N_EXPERTS):
        lstart = tab_ref[base + _R_LSTART * N_EXPERTS + e]
        length = tab_ref[base + _R_LEN * N_EXPERTS + e]
        partial = tab_ref[base + _R_PARTIAL * N_EXPERTS + e]

        def merge(e=e, lstart=lstart, length=length, partial=partial):
            first = pl.ds(pl.multiple_of(lstart, RUN_ALIGN), RUN_ALIGN)
            loc_ref[first, :] = loc_ref[first, :] + carry_ref[e]
            last = pl.ds(pl.multiple_of(lstart + length - RUN_ALIGN, RUN_ALIGN), RUN_ALIGN)

            @pl.when(partial == 1)
            def _():
                carry_ref[e] = loc_ref[last, :]

            @pl.when(partial == 0)
            def _():
                carry_ref[e] = jnp.zeros(carry_ref.shape[1:], BF16)

        pl.when(length > 0)(merge)


def _dispatch_kernel(tab_ref, fill_ref, x_ref, meta_ref, metat_ref, xs_hbm, *scratch, tm, te, n_spare):
    *loc_refs, zero_ref, carry_ref, sem, zsem = scratch
    i = pl.program_id(0)

    @pl.when(i == 0)
    def _():
        zero_ref[...] = jnp.zeros(zero_ref.shape, BF16)
        carry_ref[...] = jnp.zeros(carry_ref.shape, BF16)
        _fill_copies(fill_ref, zero_ref, xs_hbm, zsem, te=te, n_spare=n_spare, start=True)

    n_tiles = x_ref.shape[0] // tm
    for k in range(n_tiles):
        tile = n_tiles * i + k
        _sort_tile(pl.ds(k * tm, tm), x_ref, meta_ref, metat_ref[k * SUBLANES:(k + 1) * SUBLANES, :],
                   loc_refs[k], tm=tm)
        if k > 0:
            _run_copies(tab_ref, tile - 1, loc_refs[k - 1], xs_hbm, sem, to_global=True, start=False)
        _merge_carry(tile, tab_ref, loc_refs[k], carry_ref)
        _run_copies(tab_ref, tile, loc_refs[k], xs_hbm, sem, to_global=True, start=True)
    _run_copies(tab_ref, n_tiles * i + n_tiles - 1, loc_refs[-1], xs_hbm, sem, to_global=True, start=False)

    @pl.when(i == pl.num_programs(0) - 1)
    def _():
        _fill_copies(fill_ref, zero_ref, xs_hbm, zsem, te=te, n_spare=n_spare, start=False)


def _dispatch(x, meta, meta_t, run_tab, fill_tab, n_rows, n_spare):
    n, d = x.shape
    tm, te, k = TM_MOE, TM_EXPERT, MOE_TILES
    width = d + GATE_LANES
    grid_spec = pltpu.PrefetchScalarGridSpec(
        num_scalar_prefetch=2,
        grid=(n // (k * tm),),
        in_specs=[pl.BlockSpec((k * tm, d), lambda i, *_: (i, 0)),
                  pl.BlockSpec((k * tm, LANES), lambda i, *_: (i, 0)),
                  pl.BlockSpec((k * SUBLANES, tm), lambda i, *_: (i, 0))],
        out_specs=pl.BlockSpec(memory_space=pl.ANY),
        scratch_shapes=[pltpu.VMEM((SLOTS, width), BF16)] * k + [pltpu.VMEM((te, width), BF16),
                        pltpu.VMEM((N_EXPERTS, RUN_ALIGN, width), BF16),
                        pltpu.SemaphoreType.DMA, pltpu.SemaphoreType.DMA],
    )
    return pl.pallas_call(
        functools.partial(_dispatch_kernel, tm=tm, te=te, n_spare=n_spare),
        out_shape=jax.ShapeDtypeStruct((n_rows, width), BF16),
        grid_spec=grid_spec,
        compiler_params=_params(("arbitrary",)),
        name="dispatch",
    )(run_tab, fill_tab, x, meta, meta_t)


def _slot_gate(info):
    lane = lax.broadcasted_iota(jnp.int32, info.shape, 1)
    first = jnp.sum(jnp.where(lane < GATE_PIECES, info, 0.0), axis=-1, keepdims=True)
    second = jnp.sum(jnp.where((lane >= GATE_PIECES) & (lane < 2 * GATE_PIECES), info, 0.0), axis=-1, keepdims=True)
    sign = jnp.sum(jnp.where(lane == 2 * GATE_PIECES, info, 0.0), axis=-1, keepdims=True)
    return jnp.where(sign > 0, first, -second)


def _experts_kernel(tile_ref, expert_ref, rows_ref, xs_ref, wg_ref, wu_ref, wd_ref, ys_ref, acc_ref, *,
                    sub, step):
    i = pl.program_id(0)
    f = pl.program_id(1)
    te, d = ys_ref.shape
    rows = rows_ref[i]

    @pl.when((rows == 0) & (f == 0))
    def _():
        ys_ref[...] = jnp.zeros(ys_ref.shape, ys_ref.dtype)

    def partial_out(r0, m):
        xb = xs_ref[r0:r0 + m, 0:d]
        g = jnp.dot(xb, wg_ref[0], preferred_element_type=F32)
        u = jnp.dot(xb, wu_ref[0], preferred_element_type=F32)
        h = (_silu(g) * u).astype(BF16)
        return jnp.dot(h, wd_ref[0], preferred_element_type=F32)

    def blocks(n_rows):
        return [(r0, min(sub, n_rows - r0)) for r0 in range(0, n_rows, sub)]

    def first_chunk(n_rows):
        for r0, m in blocks(n_rows):
            acc_ref[r0:r0 + m, :] = partial_out(r0, m)

    def last_chunk(n_rows):
        for r0, m in blocks(n_rows):
            gate = _slot_gate(xs_ref[r0:r0 + m, d:d + GATE_LANES].astype(F32))
            ys_ref[r0:r0 + m, :] = ((acc_ref[r0:r0 + m, :] + partial_out(r0, m)) * gate).astype(ys_ref.dtype)
        if n_rows < te:
            ys_ref[n_rows:te, :] = jnp.zeros((te - n_rows, d), ys_ref.dtype)

    for n_rows in range(step, te + step, step):
        fits = (rows > n_rows - step) & (rows <= n_rows)
        pl.when(fits & (f == 0))(functools.partial(first_chunk, n_rows))
        pl.when(fits & (f == 1))(functools.partial(last_chunk, n_rows))


def _experts(xs, we_gate, we_up, we_down, tile_idx, tile_expert, tile_rows):
    n_rows, width = xs.shape
    d = width - GATE_LANES
    te = TM_EXPERT
    ff = we_gate.shape[-1]
    fc = ff // FF_CHUNKS
    assert n_rows % te == 0 and ff % FF_CHUNKS == 0 and fc % LANES == 0 and FF_CHUNKS == 2
    n_tiles = n_rows // te
    last = FF_CHUNKS - 1
    grid_spec = pltpu.PrefetchScalarGridSpec(
        num_scalar_prefetch=3,
        grid=(n_tiles, FF_CHUNKS),
        in_specs=[
            pl.BlockSpec((te, width), lambda i, f, tile, ex, rows: (tile[i], 0)),
            pl.BlockSpec((1, d, fc), lambda i, f, tile, ex, rows: (ex[i], 0, jnp.where(rows[i] > 0, f, last))),
            pl.BlockSpec((1, d, fc), lambda i, f, tile, ex, rows: (ex[i], 0, jnp.where(rows[i] > 0, f, last))),
            pl.BlockSpec((1, fc, d), lambda i, f, tile, ex, rows: (ex[i], jnp.where(rows[i] > 0, f, last), 0)),
        ],
        out_specs=pl.BlockSpec((te, d), lambda i, f, tile, ex, rows: (i, 0)),
        scratch_shapes=[pltpu.VMEM((te, d), F32)],
    )
    return pl.pallas_call(
        functools.partial(_experts_kernel, sub=SUB_EXPERT, step=STEP_EXPERT),
        out_shape=jax.ShapeDtypeStruct((n_rows, d), BF16),
        grid_spec=grid_spec,
        compiler_params=_params(("arbitrary", "arbitrary")),
        name="experts",
    )(tile_idx, tile_expert, tile_rows, xs, we_gate, we_up, we_down)


def _combine_kernel(tab_ref, x_ref, meta_ref, lng_ref, lnb_ref, ys_hbm, o_ref, loc_ref, sem, *, tm, alpha):
    i = pl.program_id(0)
    n_steps = pl.num_programs(0)
    n_tiles = x_ref.shape[0] // tm
    slot = lax.rem(i, 2)

    def copies(step, s, start):
        for k in range(n_tiles):
            _run_copies(tab_ref, n_tiles * step + k, loc_ref.at[s, k], ys_hbm, sem.at[s], to_global=False,
                        start=start)

    @pl.when(i == 0)
    def _():
        loc_ref[...] = jnp.zeros(loc_ref.shape, loc_ref.dtype)
        copies(0, 0, True)

    @pl.when(i + 1 < n_steps)
    def _():
        copies(i + 1, 1 - slot, True)

    copies(i, slot, False)

    lane = lax.broadcasted_iota(jnp.int32, (tm, SLOTS), 1).astype(F32)
    for k in range(n_tiles):
        rows = pl.ds(k * tm, tm)
        meta = meta_ref[rows, :]
        s0 = meta[:, _M_S0:_M_S0 + 1]
        s1 = meta[:, _M_S1:_M_S1 + 1]
        pick = jnp.where((lane == s0) | (lane == s1), 1.0, 0.0).astype(BF16)
        moe = jnp.dot(pick, loc_ref[slot, k], preferred_element_type=F32)
        o_ref[rows, :] = _layer_norm(alpha * x_ref[rows, :] + moe, lng_ref[...], lnb_ref[...])


def _combine_ln(x, meta, run_tab, ys, ln_g, ln_b, *, alpha):
    n, d = x.shape
    tm, k = TM_MOE, MOE_TILES
    n_steps = n // (k * tm)
    row = pl.BlockSpec((k * tm, d), lambda i, *_: (i, 0))
    grid_spec = pltpu.PrefetchScalarGridSpec(
        num_scalar_prefetch=1,
        grid=(n_steps,),
        in_specs=[row, pl.BlockSpec((k * tm, LANES), lambda i, *_: (i, 0)),
                  pl.BlockSpec(ln_g.shape, lambda i, *_: (0, 0)), pl.BlockSpec(ln_b.shape, lambda i, *_: (0, 0)),
                  pl.BlockSpec(memory_space=pl.ANY)],
        out_specs=row,
        scratch_shapes=[pltpu.VMEM((2, k, SLOTS, d), BF16), pltpu.SemaphoreType.DMA((2,))],
    )
    return pl.pallas_call(
        functools.partial(_combine_kernel, tm=tm, alpha=alpha),
        out_shape=jax.ShapeDtypeStruct((n, d), F32),
        grid_spec=grid_spec,
        compiler_params=_params(("arbitrary",)),
        name="combine_ln",
    )(run_tab, x, meta, ln_g, ln_b, ys)


def _moe_ln(x, w_router, we_gate, we_up, we_down, ln_g, ln_b, *, alpha):
    n, d = x.shape
    tm, te = TM_MOE, TM_EXPERT
    assert n % (MOE_TILES * tm) == 0 and te % SUB_EXPERT == 0 and SUB_EXPERT % STEP_EXPERT == 0
    nt = n // tm
    wr = jnp.pad(w_router, ((0, 0), (0, LANES - N_EXPERTS)))
    wr_hi = wr.astype(BF16)
    wr_lo = (wr - wr_hi.astype(F32)).astype(BF16)
    meta, meta_t, runs, totals = _router(x, wr_hi, wr_lo)

    counts = totals[:N_EXPERTS, 0].astype(jnp.int32)
    padded = (counts + RUN_ALIGN - 1) // RUN_ALIGN * RUN_ALIGN
    tiles_e = (padded + te - 1) // te
    tile_end = jnp.cumsum(tiles_e)
    region = (tile_end - tiles_e) * te
    n_used = tile_end[-1]
    max_rows = TOP_K * n + N_EXPERTS * (te - 1)
    n_tiles = -(-max_rows // te)
    n_spare = n_tiles - (TOP_K * n) // te
    runs_i = runs.reshape(nt, EXPERT_ROWS, LANES)[:, :N_EXPERTS, :RUN_FIELDS].astype(jnp.int32)
    runs_i = jnp.swapaxes(runs_i, 1, 2)
    runs_i = runs_i.at[:, _R_GOFF, :].add(region)
    run_tab = runs_i.reshape(-1)
    fill_tab = jnp.concatenate([region + padded, tiles_e * te - padded,
                                (n_used * te)[None], (n_tiles - n_used)[None]]).astype(jnp.int32)
    tiles = jnp.arange(n_tiles, dtype=jnp.int32)
    j = jnp.minimum(tiles, n_used - 1)
    tile_expert = jnp.minimum(jnp.sum(j[:, None] >= tile_end[None, :], axis=-1), N_EXPERTS - 1).astype(jnp.int32)
    region_end = jnp.sum(jnp.where(tile_expert[:, None] == jnp.arange(N_EXPERTS), region + padded, 0), axis=-1)
    tile_rows = jnp.where(tiles < n_used, jnp.minimum(region_end - tiles * te, te), 0).astype(jnp.int32)

    xs = _dispatch(x, meta, meta_t, run_tab, fill_tab, n_tiles * te, n_spare)
    ys = _experts(xs, we_gate, we_up, we_down, j, tile_expert, tile_rows)
    return _combine_ln(x, meta, run_tab, ys, ln_g, ln_b, alpha=alpha)


def kernel(x, w_in, conv_w, v_g, v_b, w_s, b_s, out_g, w_out, ln1_g, ln1_b, ln2_g, ln2_b,
           w_gate, w_up, w_down, w_router, we_gate, we_up, we_down):
    bsz, seq, d = x.shape
    depth = w_in.shape[0]
    alpha = float((2 * depth) ** 0.25)
    h = x.reshape(bsz * seq, d)
    row = lambda a: a.reshape(1, -1)
    mix_w = [w_in[0].astype(BF16), w_out[0].astype(BF16)]
    moe_w = [None, None, None]
    for i in range(depth):
        j = i // 2
        dense, last = i % 2 == 0, i + 1 == depth
        if dense:
            cast = [w_gate[j], w_up[j], w_down[j]] + ([] if last else [we_down[j]])
        else:
            cast = [we_up[j]]
        bias_t = jnp.repeat(b_s[i].T, GMLP_HEAD_DIM, axis=1)
        h, cast_w = _mixer_ln(h, mix_w[0], conv_w[i], row(v_g[i]), row(v_b[i]), w_s[i], bias_t,
                              row(out_g[i]), mix_w[1], row(ln1_g[i]), row(ln1_b[i]), cast=cast,
                              seq=seq, alpha=alpha)
        if dense:
            ahead = [] if last else [(w_in, i + 1), (w_out, i + 1), we_gate[j]]
            h, ahead_w = _ffn_ln(h, *cast_w[:3], row(ln2_g[i]), row(ln2_b[i]), cast=ahead, alpha=alpha)
            if not last:
                mix_w, moe_w = ahead_w[:2], [ahead_w[2], None, cast_w[3]]
        else:
            moe_w[1] = cast_w[0]
            h = _moe_ln(h, w_router[j], *moe_w, row(ln2_g[i]), row(ln2_b[i]), alpha=alpha)
            moe_w = [None, None, None]
            if not last:
                mix_w = [w_in[i + 1].astype(BF16), w_out[i + 1].astype(BF16)]
    return h.reshape(bsz, seq, d)
```

```python
import functools

import numpy as np
import jax
import jax.numpy as jnp
from jax import lax
from jax.experimental import pallas as pl
from jax.experimental.pallas import tpu as pltpu

F32 = jnp.float32
BF16 = jnp.bfloat16

CONV_WIDTH = 512
CONV_HEAD_DIM = 64
GMLP_WIDTH = 512
GMLP_HEADS = 4
GMLP_HEAD_DIM = 128
CHUNK = 128
N_EXPERTS = 8
TOP_K = 2
LN_EPS = 1e-5
RMS_EPS = 1e-6

LANES = 128
SUBLANES = 8
BF16_ROWS = 16
VMEM_LIMIT_BYTES = 56 * 1024 * 1024

TM_MIX = 1024
SUB_MIX = 512
TM_FFN = 1024
SUB_FFN = 256
TM_MOE = 512
MOE_TILES = 2
ROUTER_TILES = 1
TM_EXPERT = 1024
SUB_EXPERT = 512
FF_CHUNKS = 2

RUN_ALIGN = BF16_ROWS
ALIGN_SHIFT = RUN_ALIGN.bit_length() - 1
MAX_RUN = TM_MOE + RUN_ALIGN
RUN_BITS = (MAX_RUN // RUN_ALIGN).bit_length()
SLOTS = -(-(TOP_K * TM_MOE + N_EXPERTS * 2 * (RUN_ALIGN - 1)) // LANES) * LANES
GATE_LANES = LANES
GATE_PIECES = 3
TAIL_BITS = (TM_EXPERT // RUN_ALIGN - 1).bit_length()


def _layer_norm(r, g, b):
    mu = jnp.mean(r, axis=-1, keepdims=True)
    d = r - mu
    var = jnp.mean(d * d, axis=-1, keepdims=True)
    return d * lax.rsqrt(var + LN_EPS) * g + b


def _gelu(x):
    return 0.5 * x * (1.0 + lax.erf(x * (2.0 ** -0.5)))


def _silu(x):
    return x * (1.0 / (1.0 + jnp.exp(-x)))


def _params(semantics):
    return pltpu.CompilerParams(dimension_semantics=semantics, vmem_limit_bytes=VMEM_LIMIT_BYTES)


def _const_spec(shape):
    nd = len(shape)
    return pl.BlockSpec(shape, lambda *_: (0,) * nd)


def _cast_plan(weights, n_steps):
    views, in_specs, out_specs, out_shapes, shapes = [], [], [], [], []
    for w in weights:
        a, layer = w if isinstance(w, tuple) else (w, None)
        shape = a.shape if layer is None else a.shape[1:]
        rows, cols = int(np.prod(shape[:-1])), shape[-1]
        n_blocks = next(nb for nb in range(n_steps, 0, -1)
                        if n_steps % nb == 0 and rows % (nb * BF16_ROWS) == 0)
        r, first = n_steps // n_blocks, 0 if layer is None else layer * n_blocks
        views.append(a.reshape(-1, cols))
        in_specs.append(pl.BlockSpec((rows // n_blocks, cols), lambda i, r=r, first=first: (first + i // r, 0)))
        out_specs.append(pl.BlockSpec((rows // n_blocks, cols), lambda i, r=r: (i // r, 0)))
        out_shapes.append(jax.ShapeDtypeStruct((rows, cols), BF16))
        shapes.append(shape)
    return views, in_specs, out_specs, out_shapes, shapes


def _cast_along(src_refs, dst_refs):
    for src, dst in zip(src_refs, dst_refs):
        dst[...] = src[...].astype(dst.dtype)


def _mixer_kernel(*refs, n_cast, tm, sub, seq, alpha):
    (x_ref, win_ref, convw_ref, vg_ref, vb_ref, ws_ref, bias_ref, outg_ref, wout_ref,
     lng_ref, lnb_ref) = refs[:11]
    o_ref, ch_ref = refs[11 + n_cast], refs[-1]
    _cast_along(refs[11:11 + n_cast], refs[12 + n_cast:12 + 2 * n_cast])
    i = pl.program_id(0)
    cw_, gw_ = CONV_WIDTH, GMLP_WIDTH

    @pl.when(lax.rem(i * tm, seq) == 0)
    def _():
        ch_ref[0:SUBLANES, :] = jnp.zeros((SUBLANES, cw_), F32)

    trow = lax.broadcasted_iota(jnp.int32, (CHUNK, CHUNK), 0)
    tcol = lax.broadcasted_iota(jnp.int32, (CHUNK, CHUNK), 1)
    causal = tcol <= trow
    ws = [jnp.where(causal, ws_ref[h], 0.0).astype(BF16) for h in range(GMLP_HEADS)]
    lane = lax.broadcasted_iota(jnp.int32, (sub, LANES), 1)
    low_half = lane < CONV_HEAD_DIM
    nch = sub // CHUNK

    for r0 in range(0, tm, sub):
        x = x_ref[r0:r0 + sub, :]
        proj = jnp.dot(x.astype(BF16), win_ref[...], preferred_element_type=F32)
        b_gate = proj[:, 0:cw_]
        c_gate = proj[:, cw_:2 * cw_]
        hh = proj[:, 2 * cw_:3 * cw_]
        u = proj[:, 3 * cw_:3 * cw_ + gw_]
        v = proj[:, 3 * cw_ + gw_:3 * cw_ + 2 * gw_]

        c0 = SUBLANES + r0
        ch_ref[c0:c0 + sub, :] = c_gate * hh
        convw = convw_ref[...]
        conv = (convw[0:1, :] * ch_ref[c0 - 2:c0 - 2 + sub, :]
                + convw[1:2, :] * ch_ref[c0 - 1:c0 - 1 + sub, :]
                + convw[2:3, :] * ch_ref[c0:c0 + sub, :])
        y_conv = b_gate * conv

        ug = _gelu(u)
        vn = _layer_norm(_gelu(v), vg_ref[...], vb_ref[...]).astype(BF16)
        zs = []
        for h in range(GMLP_HEADS):
            lo, hi = h * GMLP_HEAD_DIM, (h + 1) * GMLP_HEAD_DIM
            rhs = jnp.concatenate([vn[c * CHUNK:(c + 1) * CHUNK, lo:hi] for c in range(nch)], axis=1)
            zs.append(jnp.dot(ws[h], rhs, preferred_element_type=F32))
        z = jnp.concatenate(
            [jnp.concatenate([zs[h][:, c * GMLP_HEAD_DIM:(c + 1) * GMLP_HEAD_DIM]
                              for h in range(GMLP_HEADS)], axis=1) for c in range(nch)], axis=0)
        bias = jnp.concatenate([bias_ref[...]] * nch, axis=0)
        y_sg = ug * (z + bias)

        parts = []
        for j in range(cw_ // LANES):
            yt = y_conv[:, j * LANES:(j + 1) * LANES]
            sq = yt * yt
            ms_lo = jnp.sum(jnp.where(low_half, sq, 0.0), axis=-1, keepdims=True) * (1.0 / CONV_HEAD_DIM)
            ms_hi = jnp.sum(jnp.where(low_half, 0.0, sq), axis=-1, keepdims=True) * (1.0 / CONV_HEAD_DIM)
            parts.append(yt * jnp.where(low_half, lax.rsqrt(ms_lo + RMS_EPS), lax.rsqrt(ms_hi + RMS_EPS)))
        for h in range(GMLP_HEADS):
            yt = y_sg[:, h * GMLP_HEAD_DIM:(h + 1) * GMLP_HEAD_DIM]
            ms = jnp.mean(yt * yt, axis=-1, keepdims=True)
            parts.append(yt * lax.rsqrt(ms + RMS_EPS))
        y = jnp.concatenate(parts, axis=1) * outg_ref[...]

        mix = jnp.dot(y.astype(BF16), wout_ref[...], preferred_element_type=F32)
        o_ref[r0:r0 + sub, :] = _layer_norm(alpha * x + mix, lng_ref[...], lnb_ref[...])

    ch_ref[0:SUBLANES, :] = ch_ref[tm:tm + SUBLANES, :]


def _mixer_ln(x, w_in, conv_w, v_g, v_b, w_s, bias_t, out_g, w_out, ln_g, ln_b, cast=(), *, seq, alpha):
    n, d = x.shape
    tm, sub = TM_MIX, SUB_MIX
    assert n % tm == 0 and seq % tm == 0 and tm % sub == 0 and sub % CHUNK == 0
    views, cast_in, cast_out, cast_shapes, shapes = _cast_plan(cast, n // tm)
    kern = functools.partial(_mixer_kernel, n_cast=len(cast), tm=tm, sub=sub, seq=seq, alpha=alpha)
    row = pl.BlockSpec((tm, d), lambda i: (i, 0))
    out = pl.pallas_call(
        kern,
        out_shape=[jax.ShapeDtypeStruct((n, d), F32)] + cast_shapes,
        grid=(n // tm,),
        in_specs=[row, _const_spec(w_in.shape), _const_spec(conv_w.shape), _const_spec(v_g.shape),
                  _const_spec(v_b.shape), _const_spec(w_s.shape), _const_spec(bias_t.shape),
                  _const_spec(out_g.shape), _const_spec(w_out.shape), _const_spec(ln_g.shape),
                  _const_spec(ln_b.shape)] + cast_in,
        out_specs=[row] + cast_out,
        scratch_shapes=[pltpu.VMEM((tm + SUBLANES, CONV_WIDTH), F32)],
        compiler_params=_params(("arbitrary",)),
        name="mixer_ln",
    )(x, w_in, conv_w, v_g, v_b, w_s, bias_t, out_g, w_out, ln_g, ln_b, *views)
    return out[0], [o.reshape(shape) for o, shape in zip(out[1:], shapes)]


def _ffn_kernel(*refs, n_cast, sub, alpha):
    x_ref, wg_ref, wu_ref, wd_ref, lng_ref, lnb_ref = refs[:6]
    o_ref = refs[6 + n_cast]
    _cast_along(refs[6:6 + n_cast], refs[7 + n_cast:7 + 2 * n_cast])
    for r0 in range(0, x_ref.shape[0], sub):
        x = x_ref[r0:r0 + sub, :]
        xb = x.astype(BF16)
        g = jnp.dot(xb, wg_ref[...], preferred_element_type=F32)
        u = jnp.dot(xb, wu_ref[...], preferred_element_type=F32)
        h = (_silu(g) * u).astype(BF16)
        ffn = jnp.dot(h, wd_ref[...], preferred_element_type=F32)
        o_ref[r0:r0 + sub, :] = _layer_norm(alpha * x + ffn, lng_ref[...], lnb_ref[...])


def _ffn_ln(x, wg, wu, wd, ln_g, ln_b, cast=(), *, alpha):
    n, d = x.shape
    tm = TM_FFN
    assert n % tm == 0 and tm % SUB_FFN == 0
    views, cast_in, cast_out, cast_shapes, shapes = _cast_plan(cast, n // tm)
    row = pl.BlockSpec((tm, d), lambda i: (i, 0))
    out = pl.pallas_call(
        functools.partial(_ffn_kernel, n_cast=len(cast), sub=SUB_FFN, alpha=alpha),
        out_shape=[jax.ShapeDtypeStruct((n, d), F32)] + cast_shapes,
        grid=(n // tm,),
        in_specs=[row, _const_spec(wg.shape), _const_spec(wu.shape), _const_spec(wd.shape),
                  _const_spec(ln_g.shape), _const_spec(ln_b.shape)] + cast_in,
        out_specs=[row] + cast_out,
        compiler_params=_params(("arbitrary",)),
        name="ffn_ln",
    )(x, wg, wu, wd, ln_g, ln_b, *views)
    return out[0], [o.reshape(shape) for o, shape in zip(out[1:], shapes)]


_M_E0, _M_E1, _M_S0, _M_S1, _M_G0, _M_G1 = range(6)
_R_LSTART, _R_LEN, _R_GOFF, _R_PARTIAL = range(4)
RUN_FIELDS = 4
EXPERT_ROWS = max(N_EXPERTS, BF16_ROWS)


def _router_kernel(x_ref, wrh_ref, wrl_ref, meta_ref, metat_ref, runs_ref, tot_ref, base_ref, *, tm):
    i = pl.program_id(0)

    @pl.when(i == 0)
    def _():
        base_ref[...] = jnp.zeros(base_ref.shape, F32)

    for k in range(x_ref.shape[0] // tm):
        _route_tile(x_ref[k * tm:(k + 1) * tm, :], wrh_ref, wrl_ref, meta_ref.at[k * tm:(k + 1) * tm],
                    metat_ref.at[k * SUBLANES:(k + 1) * SUBLANES],
                    runs_ref.at[k * EXPERT_ROWS:(k + 1) * EXPERT_ROWS], tot_ref, base_ref)


def _route_tile(x, wrh_ref, wrl_ref, meta_ref, metat_ref, runs_ref, tot_ref, base_ref):
    tm = x.shape[0]
    xh = x.astype(BF16)
    xl = (x - xh.astype(F32)).astype(BF16)
    wrh = wrh_ref[...]
    logits = (jnp.dot(xh, wrh, preferred_element_type=F32)
              + jnp.dot(xl, wrh, preferred_element_type=F32)
              + jnp.dot(xh, wrl_ref[...], preferred_element_type=F32))

    er = EXPERT_ROWS
    lg = logits.T[0:er, :]
    ef = lax.broadcasted_iota(jnp.int32, (er, tm), 0).astype(F32)
    neg = jnp.float32(-jnp.inf)
    lg = jnp.where(ef < N_EXPERTS, lg, neg)
    m0 = jnp.max(lg, axis=0, keepdims=True)
    e0 = jnp.min(jnp.where(lg == m0, ef, float(er)), axis=0, keepdims=True)
    lg1 = jnp.where(ef == e0, neg, lg)
    m1 = jnp.max(lg1, axis=0, keepdims=True)
    e1 = jnp.min(jnp.where(lg1 == m1, ef, float(er)), axis=0, keepdims=True)
    t = jnp.exp(m1 - m0)
    g0 = 1.0 / (1.0 + t)
    g1 = t / (1.0 + t)

    oh0 = ef == e0
    oh1 = ef == e1
    cnt = jnp.where(oh0 | oh1, 1.0, 0.0)
    s_i = lax.broadcasted_iota(jnp.int32, (tm, tm), 0)
    t_i = lax.broadcasted_iota(jnp.int32, (tm, tm), 1)
    earlier = jnp.where(s_i < t_i, 1.0, 0.0).astype(BF16)
    rank = jnp.dot(cnt.astype(BF16), earlier, preferred_element_type=F32)

    def align_down(v):
        return jnp.floor(v * (1.0 / RUN_ALIGN)) * RUN_ALIGN

    count = jnp.broadcast_to(jnp.sum(cnt, axis=1, keepdims=True), (er, LANES))
    before = base_ref[...]
    phase = before - align_down(before)
    end = phase + count
    run_len = jnp.where(count > 0, align_down(end + (RUN_ALIGN - 1.0)), 0.0)
    partial = jnp.where((count > 0) & (end != align_down(end)), 1.0, 0.0)
    r_e = lax.broadcasted_iota(jnp.int32, (er, er), 0)
    c_e = lax.broadcasted_iota(jnp.int32, (er, er), 1)
    lower = jnp.where(c_e < r_e, 1.0, 0.0).astype(BF16)
    lstart = jnp.dot(lower, run_len.astype(BF16), preferred_element_type=F32)
    base_ref[...] = before + count
    tot_ref[...] = before + count
    field = lax.broadcasted_iota(jnp.int32, (er, LANES), 1)
    table = jnp.zeros((er, LANES), F32)
    for f, val in ((_R_LSTART, lstart), (_R_LEN, run_len), (_R_GOFF, before - phase), (_R_PARTIAL, partial)):
        table = jnp.where(field == f, val, table)
    runs_ref[...] = table

    slot = rank + (lstart + phase)[:, 0:1]
    s0 = jnp.sum(jnp.where(oh0, slot, 0.0), axis=0, keepdims=True)
    s1 = jnp.sum(jnp.where(oh1, slot, 0.0), axis=0, keepdims=True)
    row = lax.broadcasted_iota(jnp.int32, (SUBLANES, tm), 0)
    meta_t = jnp.zeros((SUBLANES, tm), F32)
    for k, val in ((_M_E0, e0), (_M_E1, e1), (_M_S0, s0), (_M_S1, s1), (_M_G0, g0), (_M_G1, g1)):
        meta_t = jnp.where(row == k, val, meta_t)
    metat_ref[...] = meta_t
    meta_ref[...] = jnp.concatenate([meta_t, jnp.zeros((LANES - SUBLANES, tm), F32)], axis=0).T


def _router(x, wr_hi, wr_lo):
    n, d = x.shape
    tm, k = TM_MOE, ROUTER_TILES
    assert n % (k * tm) == 0
    nt = n // tm
    small = (EXPERT_ROWS, LANES)
    return pl.pallas_call(
        functools.partial(_router_kernel, tm=tm),
        out_shape=(jax.ShapeDtypeStruct((n, LANES), F32), jax.ShapeDtypeStruct((nt * SUBLANES, tm), F32),
                   jax.ShapeDtypeStruct((nt * EXPERT_ROWS, LANES), F32), jax.ShapeDtypeStruct(small, F32)),
        grid=(nt // k,),
        in_specs=[pl.BlockSpec((k * tm, d), lambda i: (i, 0)), _const_spec(wr_hi.shape), _const_spec(wr_lo.shape)],
        out_specs=(pl.BlockSpec((k * tm, LANES), lambda i: (i, 0)), pl.BlockSpec((k * SUBLANES, tm), lambda i: (i, 0)),
                   pl.BlockSpec((k * EXPERT_ROWS, LANES), lambda i: (i, 0)), _const_spec(small)),
        scratch_shapes=[pltpu.VMEM(small, F32)],
        compiler_params=_params(("arbitrary",)),
        name="router",
    )(x, wr_hi, wr_lo)


def _run_copies(tab_ref, tile, local_ref, glob_hbm, sem, *, to_global, start):
    base = tile * (RUN_FIELDS * N_EXPERTS)
    for e in range(N_EXPERTS):
        lstart = tab_ref[base + e]
        length = tab_ref[base + N_EXPERTS + e]
        gstart = tab_ref[base + 2 * N_EXPERTS + e]
        for b in range(RUN_BITS):
            size = RUN_ALIGN << b
            off = (length >> (b + ALIGN_SHIFT + 1)) << (b + ALIGN_SHIFT + 1)

            def piece(size=size, off=off, lstart=lstart, gstart=gstart):
                loc = local_ref.at[pl.ds(pl.multiple_of(lstart + off, RUN_ALIGN), size)]
                glo = glob_hbm.at[pl.ds(pl.multiple_of(gstart + off, RUN_ALIGN), size)]
                cp = pltpu.make_async_copy(loc, glo, sem) if to_global else pltpu.make_async_copy(glo, loc, sem)
                if start:
                    cp.start()
                else:
                    cp.wait()

            pl.when(((length >> (b + ALIGN_SHIFT)) & 1) == 1)(piece)


def _fill_copies(fill_ref, zero_ref, glob_hbm, sem, *, te, n_spare, start):
    def go(cp):
        if start:
            cp.start()
        else:
            cp.wait()

    for e in range(N_EXPERTS):
        tstart = fill_ref[e]
        length = fill_ref[N_EXPERTS + e]
        for b in range(TAIL_BITS):
            size = RUN_ALIGN << b
            off = (length >> (b + ALIGN_SHIFT + 1)) << (b + ALIGN_SHIFT + 1)

            def piece(size=size, off=off, tstart=tstart):
                go(pltpu.make_async_copy(zero_ref.at[pl.ds(0, size)],
                                         glob_hbm.at[pl.ds(pl.multiple_of(tstart + off, RUN_ALIGN), size)], sem))

            pl.when(((length >> (b + ALIGN_SHIFT)) & 1) == 1)(piece)
    spare0 = fill_ref[2 * N_EXPERTS]
    for j in range(n_spare):
        def tile_fill(j=j):
            go(pltpu.make_async_copy(zero_ref, glob_hbm.at[pl.ds(pl.multiple_of(spare0 + j * te, te), te)], sem))

        pl.when(j < fill_ref[2 * N_EXPERTS + 1])(tile_fill)


def _sort_tile(rows, x_ref, meta_ref, meta_t, loc_ref, *, tm):
    d = x_ref.shape[1]
    meta = meta_ref[rows, :]
    s0 = meta_t[_M_S0:_M_S0 + 1, :]
    s1 = meta_t[_M_S1:_M_S1 + 1, :]
    slot = lax.broadcasted_iota(jnp.int32, (SLOTS, tm), 0).astype(F32)
    signed = jnp.where(slot == s0, 1.0, jnp.where(slot == s1, -1.0, 0.0)).astype(BF16)
    loc_ref[:, 0:d] = jnp.dot(jnp.abs(signed), x_ref[rows, :].astype(BF16),
                              preferred_element_type=F32).astype(BF16)

    lane = lax.broadcasted_iota(jnp.int32, (tm, GATE_LANES), 1)
    info = jnp.zeros((tm, GATE_LANES), F32)
    for k, g in ((0, meta[:, _M_G0:_M_G0 + 1]), (1, meta[:, _M_G1:_M_G1 + 1])):
        rest = g
        for p in range(GATE_PIECES):
            piece = rest.astype(BF16).astype(F32)
            info = jnp.where(lane == k * GATE_PIECES + p, piece, info)
            rest = rest - piece
    loc_ref[:, d:d + GATE_LANES] = jnp.dot(signed, info.astype(BF16), preferred_element_type=F32).astype(BF16)


def _merge_carry(tile, tab_ref, loc_ref, carry_ref):
    base = tile * (RUN_FIELDS * N_EXPERTS)
    for e in range(N_EXPERTS):
        lstart = tab_ref[base + _R_LSTART * N_EXPERTS + e]
        length = tab_ref[base + _R_LEN * N_EXPERTS + e]
        partial = tab_ref[base + _R_PARTIAL * N_EXPERTS + e]

        def merge(e=e, lstart=lstart, length=length, partial=partial):
            first = pl.ds(pl.multiple_of(lstart, RUN_ALIGN), RUN_ALIGN)
            loc_ref[first, :] = loc_ref[first, :] + carry_ref[e]
            last = pl.ds(pl.multiple_of(lstart + length - RUN_ALIGN, RUN_ALIGN), RUN_ALIGN)

            @pl.when(partial == 1)
            def _():
                carry_ref[e] = loc_ref[last, :]

            @pl.when(partial == 0)
            def _():
                carry_ref[e] = jnp.zeros(carry_ref.shape[1:], BF16)

        pl.when(length > 0)(merge)


def _dispatch_kernel(tab_ref, fill_ref, x_ref, meta_ref, metat_ref, xs_hbm, *scratch, tm, te, n_spare):
    *loc_refs, zero_ref, carry_ref, sem, zsem = scratch
    i = pl.program_id(0)

    @pl.when(i == 0)
    def _():
        zero_ref[...] = jnp.zeros(zero_ref.shape, BF16)
        carry_ref[...] = jnp.zeros(carry_ref.shape, BF16)
        _fill_copies(fill_ref, zero_ref, xs_hbm, zsem, te=te, n_spare=n_spare, start=True)

    n_tiles = x_ref.shape[0] // tm
    for k in range(n_tiles):
        tile = n_tiles * i + k
        _sort_tile(pl.ds(k * tm, tm), x_ref, meta_ref, metat_ref[k * SUBLANES:(k + 1) * SUBLANES, :],
                   loc_refs[k], tm=tm)
        if k > 0:
            _run_copies(tab_ref, tile - 1, loc_refs[k - 1], xs_hbm, sem, to_global=True, start=False)
        _merge_carry(tile, tab_ref, loc_refs[k], carry_ref)
        _run_copies(tab_ref, tile, loc_refs[k], xs_hbm, sem, to_global=True, start=True)
    _run_copies(tab_ref, n_tiles * i + n_tiles - 1, loc_refs[-1], xs_hbm, sem, to_global=True, start=False)

    @pl.when(i == pl.num_programs(0) - 1)
    def _():
        _fill_copies(fill_ref, zero_ref, xs_hbm, zsem, te=te, n_spare=n_spare, start=False)


def _dispatch(x, meta, meta_t, run_tab, fill_tab, n_rows, n_spare):
    n, d = x.shape
    tm, te, k = TM_MOE, TM_EXPERT, MOE_TILES
    width = d + GATE_LANES
    grid_spec = pltpu.PrefetchScalarGridSpec(
        num_scalar_prefetch=2,
        grid=(n // (k * tm),),
        in_specs=[pl.BlockSpec((k * tm, d), lambda i, *_: (i, 0)),
                  pl.BlockSpec((k * tm, LANES), lambda i, *_: (i, 0)),
                  pl.BlockSpec((k * SUBLANES, tm), lambda i, *_: (i, 0))],
        out_specs=pl.BlockSpec(memory_space=pl.ANY),
        scratch_shapes=[pltpu.VMEM((SLOTS, width), BF16)] * k + [pltpu.VMEM((te, width), BF16),
                        pltpu.VMEM((N_EXPERTS, RUN_ALIGN, width), BF16),
                        pltpu.SemaphoreType.DMA, pltpu.SemaphoreType.DMA],
    )
    return pl.pallas_call(
        functools.partial(_dispatch_kernel, tm=tm, te=te, n_spare=n_spare),
        out_shape=jax.ShapeDtypeStruct((n_rows, width), BF16),
        grid_spec=grid_spec,
        compiler_params=_params(("arbitrary",)),
        name="dispatch",
    )(run_tab, fill_tab, x, meta, meta_t)


def _slot_gate(info):
    lane = lax.broadcasted_iota(jnp.int32, info.shape, 1)
    first = jnp.sum(jnp.where(lane < GATE_PIECES, info, 0.0), axis=-1, keepdims=True)
    second = jnp.sum(jnp.where((lane >= GATE_PIECES) & (lane < 2 * GATE_PIECES), info, 0.0), axis=-1, keepdims=True)
    return jnp.maximum(first, -second)


def _experts_kernel(tile_ref, expert_ref, rows_ref, xs_ref, wg_ref, wu_ref, wd_ref, ys_ref, acc_ref, *, sub):
    i = pl.program_id(0)
    f = pl.program_id(1)
    te, d = ys_ref.shape
    rows = rows_ref[i]

    @pl.when((rows == 0) & (f == 0))
    def _():
        ys_ref[...] = jnp.zeros(ys_ref.shape, ys_ref.dtype)

    def partial_out(r0):
        xb = xs_ref[r0:r0 + sub, 0:d]
        g = jnp.dot(xb, wg_ref[0], preferred_element_type=F32)
        u = jnp.dot(xb, wu_ref[0], preferred_element_type=F32)
        h = (_silu(g) * u).astype(BF16)
        return jnp.dot(h, wd_ref[0], preferred_element_type=F32)

    def first_chunk(n_rows):
        for r0 in range(0, n_rows, sub):
            acc_ref[r0:r0 + sub, :] = partial_out(r0)

    def last_chunk(n_rows):
        for r0 in range(0, n_rows, sub):
            gate = _slot_gate(xs_ref[r0:r0 + sub, d:d + GATE_LANES].astype(F32))
            ys_ref[r0:r0 + sub, :] = ((acc_ref[r0:r0 + sub, :] + partial_out(r0)) * gate).astype(ys_ref.dtype)
        if n_rows < te:
            ys_ref[n_rows:te, :] = jnp.zeros((te - n_rows, d), ys_ref.dtype)

    for nb in range(1, te // sub + 1):
        fits = (rows > (nb - 1) * sub) & (rows <= nb * sub)
        pl.when(fits & (f == 0))(functools.partial(first_chunk, nb * sub))
        pl.when(fits & (f == 1))(functools.partial(last_chunk, nb * sub))


def _experts(xs, we_gate, we_up, we_down, tile_idx, tile_expert, tile_rows):
    n_rows, width = xs.shape
    d = width - GATE_LANES
    te = TM_EXPERT
    ff = we_gate.shape[-1]
    fc = ff // FF_CHUNKS
    assert n_rows % te == 0 and ff % FF_CHUNKS == 0 and fc % LANES == 0 and FF_CHUNKS == 2
    n_tiles = n_rows // te
    last = FF_CHUNKS - 1
    grid_spec = pltpu.PrefetchScalarGridSpec(
        num_scalar_prefetch=3,
        grid=(n_tiles, FF_CHUNKS),
        in_specs=[
            pl.BlockSpec((te, width), lambda i, f, tile, ex, rows: (tile[i], 0)),
            pl.BlockSpec((1, d, fc), lambda i, f, tile, ex, rows: (ex[i], 0, jnp.where(rows[i] > 0, f, last))),
            pl.BlockSpec((1, d, fc), lambda i, f, tile, ex, rows: (ex[i], 0, jnp.where(rows[i] > 0, f, last))),
            pl.BlockSpec((1, fc, d), lambda i, f, tile, ex, rows: (ex[i], jnp.where(rows[i] > 0, f, last), 0)),
        ],
        out_specs=pl.BlockSpec((te, d), lambda i, f, tile, ex, rows: (i, 0)),
        scratch_shapes=[pltpu.VMEM((te, d), F32)],
    )
    return pl.pallas_call(
        functools.partial(_experts_kernel, sub=SUB_EXPERT),
        out_shape=jax.ShapeDtypeStruct((n_rows, d), BF16),
        grid_spec=grid_spec,
        compiler_params=_params(("arbitrary", "arbitrary")),
        name="experts",
    )(tile_idx, tile_expert, tile_rows, xs, we_gate, we_up, we_down)


def _combine_kernel(tab_ref, x_ref, meta_ref, lng_ref, lnb_ref, ys_hbm, o_ref, loc_ref, sem, *, tm, alpha):
    i = pl.program_id(0)
    n_steps = pl.num_programs(0)
    n_tiles = x_ref.shape[0] // tm
    slot = lax.rem(i, 2)

    def copies(step, s, start):
        for k in range(n_tiles):
            _run_copies(tab_ref, n_tiles * step + k, loc_ref.at[s, k], ys_hbm, sem.at[s], to_global=False,
                        start=start)

    @pl.when(i == 0)
    def _():
        loc_ref[...] = jnp.zeros(loc_ref.shape, loc_ref.dtype)
        copies(0, 0, True)

    @pl.when(i + 1 < n_steps)
    def _():
        copies(i + 1, 1 - slot, True)

    copies(i, slot, False)

    lane = lax.broadcasted_iota(jnp.int32, (tm, SLOTS), 1).astype(F32)
    for k in range(n_tiles):
        rows = pl.ds(k * tm, tm)
        meta = meta_ref[rows, :]
        s0 = meta[:, _M_S0:_M_S0 + 1]
        s1 = meta[:, _M_S1:_M_S1 + 1]
        pick = jnp.where((lane == s0) | (lane == s1), 1.0, 0.0).astype(BF16)
        moe = jnp.dot(pick, loc_ref[slot, k], preferred_element_type=F32)
        o_ref[rows, :] = _layer_norm(alpha * x_ref[rows, :] + moe, lng_ref[...], lnb_ref[...])


def _combine_ln(x, meta, run_tab, ys, ln_g, ln_b, *, alpha):
    n, d = x.shape
    tm, k = TM_MOE, MOE_TILES
    n_steps = n // (k * tm)
    row = pl.BlockSpec((k * tm, d), lambda i, *_: (i, 0))
    grid_spec = pltpu.PrefetchScalarGridSpec(
        num_scalar_prefetch=1,
        grid=(n_steps,),
        in_specs=[row, pl.BlockSpec((k * tm, LANES), lambda i, *_: (i, 0)),
                  pl.BlockSpec(ln_g.shape, lambda i, *_: (0, 0)), pl.BlockSpec(ln_b.shape, lambda i, *_: (0, 0)),
                  pl.BlockSpec(memory_space=pl.ANY)],
        out_specs=row,
        scratch_shapes=[pltpu.VMEM((2, k, SLOTS, d), BF16), pltpu.SemaphoreType.DMA((2,))],
    )
    return pl.pallas_call(
        functools.partial(_combine_kernel, tm=tm, alpha=alpha),
        out_shape=jax.ShapeDtypeStruct((n, d), F32),
        grid_spec=grid_spec,
        compiler_params=_params(("arbitrary",)),
        name="combine_ln",
    )(run_tab, x, meta, ln_g, ln_b, ys)


def _moe_ln(x, w_router, we_gate, we_up, we_down, ln_g, ln_b, *, alpha):
    n, d = x.shape
    tm, te = TM_MOE, TM_EXPERT
    assert n % (MOE_TILES * tm) == 0 and te % SUB_EXPERT == 0
    nt = n // tm
    wr = jnp.pad(w_router, ((0, 0), (0, LANES - N_EXPERTS)))
    wr_hi = wr.astype(BF16)
    wr_lo = (wr - wr_hi.astype(F32)).astype(BF16)
    meta, meta_t, runs, totals = _router(x, wr_hi, wr_lo)

    counts = totals[:N_EXPERTS, 0].astype(jnp.int32)
    padded = (counts + RUN_ALIGN - 1) // RUN_ALIGN * RUN_ALIGN
    tiles_e = (padded + te - 1) // te
    tile_end = jnp.cumsum(tiles_e)
    region = (tile_end - tiles_e) * te
    n_used = tile_end[-1]
    max_rows = TOP_K * n + N_EXPERTS * (te - 1)
    n_tiles = -(-max_rows // te)
    n_spare = n_tiles - (TOP_K * n) // te
    runs_i = runs.reshape(nt, EXPERT_ROWS, LANES)[:, :N_EXPERTS, :RUN_FIELDS].astype(jnp.int32)
    runs_i = jnp.swapaxes(runs_i, 1, 2)
    runs_i = runs_i.at[:, _R_GOFF, :].add(region)
    run_tab = runs_i.reshape(-1)
    fill_tab = jnp.concatenate([region + padded, tiles_e * te - padded,
                                (n_used * te)[None], (n_tiles - n_used)[None]]).astype(jnp.int32)
    tiles = jnp.arange(n_tiles, dtype=jnp.int32)
    j = jnp.minimum(tiles, n_used - 1)
    tile_expert = jnp.minimum(jnp.sum(j[:, None] >= tile_end[None, :], axis=-1), N_EXPERTS - 1).astype(jnp.int32)
    region_end = jnp.sum(jnp.where(tile_expert[:, None] == jnp.arange(N_EXPERTS), region + padded, 0), axis=-1)
    tile_rows = jnp.where(tiles < n_used, jnp.minimum(region_end - tiles * te, te), 0).astype(jnp.int32)

    xs = _dispatch(x, meta, meta_t, run_tab, fill_tab, n_tiles * te, n_spare)
    ys = _experts(xs, we_gate, we_up, we_down, j, tile_expert, tile_rows)
    return _combine_ln(x, meta, run_tab, ys, ln_g, ln_b, alpha=alpha)


def kernel(x, w_in, conv_w, v_g, v_b, w_s, b_s, out_g, w_out, ln1_g, ln1_b, ln2_g, ln2_b,
           w_gate, w_up, w_down, w_router, we_gate, we_up, we_down):
    bsz, seq, d = x.shape
    depth = w_in.shape[0]
    alpha = float((2 * depth) ** 0.25)
    h = x.reshape(bsz * seq, d)
    row = lambda a: a.reshape(1, -1)
    mix_w = [w_in[0].astype(BF16), w_out[0].astype(BF16)]
    moe_w = [None, None, None]
    for i in range(depth):
        j = i // 2
        dense, last = i % 2 == 0, i + 1 == depth
        if dense:
            cast = [w_gate[j], w_up[j], w_down[j]] + ([] if last else [we_down[j]])
        else:
            cast = [we_up[j]]
        bias_t = jnp.repeat(b_s[i].T, GMLP_HEAD_DIM, axis=1)
        h, cast_w = _mixer_ln(h, mix_w[0], conv_w[i], row(v_g[i]), row(v_b[i]), w_s[i], bias_t,
                              row(out_g[i]), mix_w[1], row(ln1_g[i]), row(ln1_b[i]), cast=cast,
                              seq=seq, alpha=alpha)
        if dense:
            ahead = [] if last else [(w_in, i + 1), (w_out, i + 1), we_gate[j]]
            h, ahead_w = _ffn_ln(h, *cast_w[:3], row(ln2_g[i]), row(ln2_b[i]), cast=ahead, alpha=alpha)
            if not last:
                mix_w, moe_w = ahead_w[:2], [ahead_w[2], None, cast_w[3]]
        else:
            moe_w[1] = cast_w[0]
            h = _moe_ln(h, w_router[j], *moe_w, row(ln2_g[i]), row(ln2_b[i]), alpha=alpha)
            moe_w = [None, None, None]
            if not last:
                mix_w = [w_in[i + 1].astype(BF16), w_out[i + 1].astype(BF16)]
    return h.reshape(bsz, seq, d)
```

```python
import functools

import numpy as np
import jax
import jax.numpy as jnp
from jax import lax
from jax.experimental import pallas as pl
from jax.experimental.pallas import tpu as pltpu

F32 = jnp.float32
BF16 = jnp.bfloat16

CONV_WIDTH = 512
CONV_HEAD_DIM = 64
GMLP_WIDTH = 512
GMLP_HEADS = 4
GMLP_HEAD_DIM = 128
CHUNK = 128
N_EXPERTS = 8
TOP_K = 2
LN_EPS = 1e-5
RMS_EPS = 1e-6

LANES = 128
SUBLANES = 8
BF16_ROWS = 16
VMEM_LIMIT_BYTES = 56 * 1024 * 1024

TM_MIX = 1024
SUB_MIX = 512
TM_FFN = 1024
SUB_FFN = 256
TM_MOE = 512
DISPATCH_TILES = 4
COMBINE_TILES = 2
ROUTER_TILES = 1
TM_EXPERT = 1024
SUB_EXPERT = 256
FF_CHUNKS = 2

RUN_ALIGN = BF16_ROWS
ALIGN_SHIFT = RUN_ALIGN.bit_length() - 1
MAX_RUN = TM_MOE + RUN_ALIGN
RUN_BITS = (MAX_RUN // RUN_ALIGN).bit_length()
SLOTS = -(-(TOP_K * TM_MOE + N_EXPERTS * 2 * (RUN_ALIGN - 1)) // LANES) * LANES
GATE_LANES = LANES
GATE_PIECES = 3
TAIL_BITS = (TM_EXPERT // RUN_ALIGN - 1).bit_length()


def _layer_norm(r, g, b):
    mu = jnp.mean(r, axis=-1, keepdims=True)
    d = r - mu
    var = jnp.mean(d * d, axis=-1, keepdims=True)
    return d * lax.rsqrt(var + LN_EPS) * g + b


def _gelu(x):
    return 0.5 * x * (1.0 + lax.erf(x * (2.0 ** -0.5)))


def _silu(x):
    return x * (1.0 / (1.0 + jnp.exp(-x)))


def _params(semantics):
    return pltpu.CompilerParams(dimension_semantics=semantics, vmem_limit_bytes=VMEM_LIMIT_BYTES)


def _const_spec(shape):
    nd = len(shape)
    return pl.BlockSpec(shape, lambda *_: (0,) * nd)


def _cast_plan(weights, n_steps):
    views, in_specs, out_specs, out_shapes, shapes = [], [], [], [], []
    for w in weights:
        a, layer = w if isinstance(w, tuple) else (w, None)
        shape = a.shape if layer is None else a.shape[1:]
        rows, cols = int(np.prod(shape[:-1])), shape[-1]
        n_blocks = next(nb for nb in range(n_steps, 0, -1)
                        if n_steps % nb == 0 and rows % (nb * BF16_ROWS) == 0)
        r, first = n_steps // n_blocks, 0 if layer is None else layer * n_blocks
        views.append(a.reshape(-1, cols))
        in_specs.append(pl.BlockSpec((rows // n_blocks, cols), lambda i, r=r, first=first: (first + i // r, 0)))
        out_specs.append(pl.BlockSpec((rows // n_blocks, cols), lambda i, r=r: (i // r, 0)))
        out_shapes.append(jax.ShapeDtypeStruct((rows, cols), BF16))
        shapes.append(shape)
    return views, in_specs, out_specs, out_shapes, shapes


def _cast_along(src_refs, dst_refs):
    for src, dst in zip(src_refs, dst_refs):
        dst[...] = src[...].astype(dst.dtype)


def _mixer_kernel(*refs, n_cast, tm, sub, seq, alpha):
    (x_ref, win_ref, convw_ref, vg_ref, vb_ref, ws_ref, bias_ref, outg_ref, wout_ref,
     lng_ref, lnb_ref) = refs[:11]
    o_ref, ch_ref = refs[11 + n_cast], refs[-1]
    _cast_along(refs[11:11 + n_cast], refs[12 + n_cast:12 + 2 * n_cast])
    i = pl.program_id(0)
    cw_, gw_ = CONV_WIDTH, GMLP_WIDTH

    @pl.when(lax.rem(i * tm, seq) == 0)
    def _():
        ch_ref[0:SUBLANES, :] = jnp.zeros((SUBLANES, cw_), F32)

    trow = lax.broadcasted_iota(jnp.int32, (CHUNK, CHUNK), 0)
    tcol = lax.broadcasted_iota(jnp.int32, (CHUNK, CHUNK), 1)
    causal = tcol <= trow
    ws = [jnp.where(causal, ws_ref[h], 0.0).astype(BF16) for h in range(GMLP_HEADS)]
    lane = lax.broadcasted_iota(jnp.int32, (sub, LANES), 1)
    low_half = lane < CONV_HEAD_DIM
    nch = sub // CHUNK

    for r0 in range(0, tm, sub):
        x = x_ref[r0:r0 + sub, :]
        proj = jnp.dot(x.astype(BF16), win_ref[...], preferred_element_type=F32)
        b_gate = proj[:, 0:cw_]
        c_gate = proj[:, cw_:2 * cw_]
        hh = proj[:, 2 * cw_:3 * cw_]
        u = proj[:, 3 * cw_:3 * cw_ + gw_]
        v = proj[:, 3 * cw_ + gw_:3 * cw_ + 2 * gw_]

        c0 = SUBLANES + r0
        ch_ref[c0:c0 + sub, :] = c_gate * hh
        convw = convw_ref[...]
        conv = (convw[0:1, :] * ch_ref[c0 - 2:c0 - 2 + sub, :]
                + convw[1:2, :] * ch_ref[c0 - 1:c0 - 1 + sub, :]
                + convw[2:3, :] * ch_ref[c0:c0 + sub, :])
        y_conv = b_gate * conv

        ug = _gelu(u)
        vn = _layer_norm(_gelu(v), vg_ref[...], vb_ref[...]).astype(BF16)
        zs = []
        for h in range(GMLP_HEADS):
            lo, hi = h * GMLP_HEAD_DIM, (h + 1) * GMLP_HEAD_DIM
            rhs = jnp.concatenate([vn[c * CHUNK:(c + 1) * CHUNK, lo:hi] for c in range(nch)], axis=1)
            zs.append(jnp.dot(ws[h], rhs, preferred_element_type=F32))
        z = jnp.concatenate(
            [jnp.concatenate([zs[h][:, c * GMLP_HEAD_DIM:(c + 1) * GMLP_HEAD_DIM]
                              for h in range(GMLP_HEADS)], axis=1) for c in range(nch)], axis=0)
        bias = jnp.concatenate([bias_ref[...]] * nch, axis=0)
        y_sg = ug * (z + bias)

        parts = []
        for j in range(cw_ // LANES):
            yt = y_conv[:, j * LANES:(j + 1) * LANES]
            sq = yt * yt
            ms_lo = jnp.sum(jnp.where(low_half, sq, 0.0), axis=-1, keepdims=True) * (1.0 / CONV_HEAD_DIM)
            ms_hi = jnp.sum(jnp.where(low_half, 0.0, sq), axis=-1, keepdims=True) * (1.0 / CONV_HEAD_DIM)
            parts.append(yt * jnp.where(low_half, lax.rsqrt(ms_lo + RMS_EPS), lax.rsqrt(ms_hi + RMS_EPS)))
        for h in range(GMLP_HEADS):
            yt = y_sg[:, h * GMLP_HEAD_DIM:(h + 1) * GMLP_HEAD_DIM]
            ms = jnp.mean(yt * yt, axis=-1, keepdims=True)
            parts.append(yt * lax.rsqrt(ms + RMS_EPS))
        y = jnp.concatenate(parts, axis=1) * outg_ref[...]

        mix = jnp.dot(y.astype(BF16), wout_ref[...], preferred_element_type=F32)
        o_ref[r0:r0 + sub, :] = _layer_norm(alpha * x + mix, lng_ref[...], lnb_ref[...])

    ch_ref[0:SUBLANES, :] = ch_ref[tm:tm + SUBLANES, :]


def _mixer_ln(x, w_in, conv_w, v_g, v_b, w_s, bias_t, out_g, w_out, ln_g, ln_b, cast=(), *, seq, alpha):
    n, d = x.shape
    tm, sub = TM_MIX, SUB_MIX
    assert n % tm == 0 and seq % tm == 0 and tm % sub == 0 and sub % CHUNK == 0
    views, cast_in, cast_out, cast_shapes, shapes = _cast_plan(cast, n // tm)
    kern = functools.partial(_mixer_kernel, n_cast=len(cast), tm=tm, sub=sub, seq=seq, alpha=alpha)
    row = pl.BlockSpec((tm, d), lambda i: (i, 0))
    out = pl.pallas_call(
        kern,
        out_shape=[jax.ShapeDtypeStruct((n, d), F32)] + cast_shapes,
        grid=(n // tm,),
        in_specs=[row, _const_spec(w_in.shape), _const_spec(conv_w.shape), _const_spec(v_g.shape),
                  _const_spec(v_b.shape), _const_spec(w_s.shape), _const_spec(bias_t.shape),
                  _const_spec(out_g.shape), _const_spec(w_out.shape), _const_spec(ln_g.shape),
                  _const_spec(ln_b.shape)] + cast_in,
        out_specs=[row] + cast_out,
        scratch_shapes=[pltpu.VMEM((tm + SUBLANES, CONV_WIDTH), F32)],
        compiler_params=_params(("arbitrary",)),
        name="mixer_ln",
    )(x, w_in, conv_w, v_g, v_b, w_s, bias_t, out_g, w_out, ln_g, ln_b, *views)
    return out[0], [o.reshape(shape) for o, shape in zip(out[1:], shapes)]


def _ffn_kernel(*refs, n_cast, sub, alpha):
    x_ref, wg_ref, wu_ref, wd_ref, lng_ref, lnb_ref = refs[:6]
    o_ref = refs[6 + n_cast]
    _cast_along(refs[6:6 + n_cast], refs[7 + n_cast:7 + 2 * n_cast])
    for r0 in range(0, x_ref.shape[0], sub):
        x = x_ref[r0:r0 + sub, :]
        xb = x.astype(BF16)
        g = jnp.dot(xb, wg_ref[...], preferred_element_type=F32)
        u = jnp.dot(xb, wu_ref[...], preferred_element_type=F32)
        h = (_silu(g) * u).astype(BF16)
        ffn = jnp.dot(h, wd_ref[...], preferred_element_type=F32)
        o_ref[r0:r0 + sub, :] = _layer_norm(alpha * x + ffn, lng_ref[...], lnb_ref[...])


def _ffn_ln(x, wg, wu, wd, ln_g, ln_b, cast=(), *, alpha):
    n, d = x.shape
    tm = TM_FFN
    assert n % tm == 0 and tm % SUB_FFN == 0
    views, cast_in, cast_out, cast_shapes, shapes = _cast_plan(cast, n // tm)
    row = pl.BlockSpec((tm, d), lambda i: (i, 0))
    out = pl.pallas_call(
        functools.partial(_ffn_kernel, n_cast=len(cast), sub=SUB_FFN, alpha=alpha),
        out_shape=[jax.ShapeDtypeStruct((n, d), F32)] + cast_shapes,
        grid=(n // tm,),
        in_specs=[row, _const_spec(wg.shape), _const_spec(wu.shape), _const_spec(wd.shape),
                  _const_spec(ln_g.shape), _const_spec(ln_b.shape)] + cast_in,
        out_specs=[row] + cast_out,
        compiler_params=_params(("arbitrary",)),
        name="ffn_ln",
    )(x, wg, wu, wd, ln_g, ln_b, *views)
    return out[0], [o.reshape(shape) for o, shape in zip(out[1:], shapes)]


_M_E0, _M_E1, _M_S0, _M_S1, _M_G0, _M_G1 = range(6)
_R_LSTART, _R_LEN, _R_GOFF, _R_PARTIAL = range(4)
RUN_FIELDS = 4
EXPERT_ROWS = max(N_EXPERTS, BF16_ROWS)


def _router_kernel(x_ref, wrh_ref, wrl_ref, meta_ref, metat_ref, runs_ref, tot_ref, base_ref, *, tm):
    i = pl.program_id(0)

    @pl.when(i == 0)
    def _():
        base_ref[...] = jnp.zeros(base_ref.shape, F32)

    for k in range(x_ref.shape[0] // tm):
        _route_tile(x_ref[k * tm:(k + 1) * tm, :], wrh_ref, wrl_ref, meta_ref.at[k * tm:(k + 1) * tm],
                    metat_ref.at[k * SUBLANES:(k + 1) * SUBLANES],
                    runs_ref.at[k * EXPERT_ROWS:(k + 1) * EXPERT_ROWS], tot_ref, base_ref)


def _route_tile(x, wrh_ref, wrl_ref, meta_ref, metat_ref, runs_ref, tot_ref, base_ref):
    tm = x.shape[0]
    xh = x.astype(BF16)
    xl = (x - xh.astype(F32)).astype(BF16)
    wrh = wrh_ref[...]
    logits = (jnp.dot(xh, wrh, preferred_element_type=F32)
              + jnp.dot(xl, wrh, preferred_element_type=F32)
              + jnp.dot(xh, wrl_ref[...], preferred_element_type=F32))

    er = EXPERT_ROWS
    lg = logits.T[0:er, :]
    ef = lax.broadcasted_iota(jnp.int32, (er, tm), 0).astype(F32)
    neg = jnp.float32(-jnp.inf)
    lg = jnp.where(ef < N_EXPERTS, lg, neg)
    m0 = jnp.max(lg, axis=0, keepdims=True)
    e0 = jnp.min(jnp.where(lg == m0, ef, float(er)), axis=0, keepdims=True)
    lg1 = jnp.where(ef == e0, neg, lg)
    m1 = jnp.max(lg1, axis=0, keepdims=True)
    e1 = jnp.min(jnp.where(lg1 == m1, ef, float(er)), axis=0, keepdims=True)
    t = jnp.exp(m1 - m0)
    g0 = 1.0 / (1.0 + t)
    g1 = t / (1.0 + t)

    oh0 = ef == e0
    oh1 = ef == e1
    cnt = jnp.where(oh0 | oh1, 1.0, 0.0)
    s_i = lax.broadcasted_iota(jnp.int32, (tm, tm), 0)
    t_i = lax.broadcasted_iota(jnp.int32, (tm, tm), 1)
    earlier = jnp.where(s_i < t_i, 1.0, 0.0).astype(BF16)
    rank = jnp.dot(cnt.astype(BF16), earlier, preferred_element_type=F32)

    def align_down(v):
        return jnp.floor(v * (1.0 / RUN_ALIGN)) * RUN_ALIGN

    count = jnp.broadcast_to(jnp.sum(cnt, axis=1, keepdims=True), (er, LANES))
    before = base_ref[...]
    phase = before - align_down(before)
    end = phase + count
    run_len = jnp.where(count > 0, align_down(end + (RUN_ALIGN - 1.0)), 0.0)
    partial = jnp.where((count > 0) & (end != align_down(end)), 1.0, 0.0)
    r_e = lax.broadcasted_iota(jnp.int32, (er, er), 0)
    c_e = lax.broadcasted_iota(jnp.int32, (er, er), 1)
    lower = jnp.where(c_e < r_e, 1.0, 0.0).astype(BF16)
    lstart = jnp.dot(lower, run_len.astype(BF16), preferred_element_type=F32)
    base_ref[...] = before + count
    tot_ref[...] = before + count
    field = lax.broadcasted_iota(jnp.int32, (er, LANES), 1)
    table = jnp.zeros((er, LANES), F32)
    for f, val in ((_R_LSTART, lstart), (_R_LEN, run_len), (_R_GOFF, before - phase), (_R_PARTIAL, partial)):
        table = jnp.where(field == f, val, table)
    runs_ref[...] = table

    slot = rank + (lstart + phase)[:, 0:1]
    s0 = jnp.sum(jnp.where(oh0, slot, 0.0), axis=0, keepdims=True)
    s1 = jnp.sum(jnp.where(oh1, slot, 0.0), axis=0, keepdims=True)
    row = lax.broadcasted_iota(jnp.int32, (SUBLANES, tm), 0)
    meta_t = jnp.zeros((SUBLANES, tm), F32)
    for k, val in ((_M_E0, e0), (_M_E1, e1), (_M_S0, s0), (_M_S1, s1), (_M_G0, g0), (_M_G1, g1)):
        meta_t = jnp.where(row == k, val, meta_t)
    metat_ref[...] = meta_t
    meta_ref[...] = jnp.concatenate([meta_t, jnp.zeros((LANES - SUBLANES, tm), F32)], axis=0).T


def _router(x, wr_hi, wr_lo):
    n, d = x.shape
    tm, k = TM_MOE, ROUTER_TILES
    assert n % (k * tm) == 0
    nt = n // tm
    small = (EXPERT_ROWS, LANES)
    return pl.pallas_call(
        functools.partial(_router_kernel, tm=tm),
        out_shape=(jax.ShapeDtypeStruct((n, LANES), F32), jax.ShapeDtypeStruct((nt * SUBLANES, tm), F32),
                   jax.ShapeDtypeStruct((nt * EXPERT_ROWS, LANES), F32), jax.ShapeDtypeStruct(small, F32)),
        grid=(nt // k,),
        in_specs=[pl.BlockSpec((k * tm, d), lambda i: (i, 0)), _const_spec(wr_hi.shape), _const_spec(wr_lo.shape)],
        out_specs=(pl.BlockSpec((k * tm, LANES), lambda i: (i, 0)), pl.BlockSpec((k * SUBLANES, tm), lambda i: (i, 0)),
                   pl.BlockSpec((k * EXPERT_ROWS, LANES), lambda i: (i, 0)), _const_spec(small)),
        scratch_shapes=[pltpu.VMEM(small, F32)],
        compiler_params=_params(("arbitrary",)),
        name="router",
    )(x, wr_hi, wr_lo)


def _run_copies(tab_ref, tile, local_ref, glob_hbm, sem, *, to_global, start):
    base = tile * (RUN_FIELDS * N_EXPERTS)
    for e in range(N_EXPERTS):
        lstart = tab_ref[base + e]
        length = tab_ref[base + N_EXPERTS + e]
        gstart = tab_ref[base + 2 * N_EXPERTS + e]
        for b in range(RUN_BITS):
            size = RUN_ALIGN << b
            off = (length >> (b + ALIGN_SHIFT + 1)) << (b + ALIGN_SHIFT + 1)

            def piece(size=size, off=off, lstart=lstart, gstart=gstart):
                loc = local_ref.at[pl.ds(pl.multiple_of(lstart + off, RUN_ALIGN), size)]
                glo = glob_hbm.at[pl.ds(pl.multiple_of(gstart + off, RUN_ALIGN), size)]
                cp = pltpu.make_async_copy(loc, glo, sem) if to_global else pltpu.make_async_copy(glo, loc, sem)
                if start:
                    cp.start()
                else:
                    cp.wait()

            pl.when(((length >> (b + ALIGN_SHIFT)) & 1) == 1)(piece)


def _fill_copies(fill_ref, zero_ref, glob_hbm, sem, *, te, n_spare, start):
    def go(cp):
        if start:
            cp.start()
        else:
            cp.wait()

    for e in range(N_EXPERTS):
        tstart = fill_ref[e]
        length = fill_ref[N_EXPERTS + e]
        for b in range(TAIL_BITS):
            size = RUN_ALIGN << b
            off = (length >> (b + ALIGN_SHIFT + 1)) << (b + ALIGN_SHIFT + 1)

            def piece(size=size, off=off, tstart=tstart):
                go(pltpu.make_async_copy(zero_ref.at[pl.ds(0, size)],
                                         glob_hbm.at[pl.ds(pl.multiple_of(tstart + off, RUN_ALIGN), size)], sem))

            pl.when(((length >> (b + ALIGN_SHIFT)) & 1) == 1)(piece)
    spare0 = fill_ref[2 * N_EXPERTS]
    for j in range(n_spare):
        def tile_fill(j=j):
            go(pltpu.make_async_copy(zero_ref, glob_hbm.at[pl.ds(pl.multiple_of(spare0 + j * te, te), te)], sem))

        pl.when(j < fill_ref[2 * N_EXPERTS + 1])(tile_fill)


def _sort_tile(rows, x_ref, meta_ref, meta_t, loc_ref, *, tm):
    d = x_ref.shape[1]
    meta = meta_ref[rows, :]
    s0 = meta_t[_M_S0:_M_S0 + 1, :]
    s1 = meta_t[_M_S1:_M_S1 + 1, :]
    slot = lax.broadcasted_iota(jnp.int32, (SLOTS, tm), 0).astype(F32)
    signed = jnp.where(slot == s0, 1.0, jnp.where(slot == s1, -1.0, 0.0)).astype(BF16)
    loc_ref[:, 0:d] = jnp.dot(jnp.abs(signed), x_ref[rows, :].astype(BF16),
                              preferred_element_type=F32).astype(BF16)

    lane = lax.broadcasted_iota(jnp.int32, (tm, GATE_LANES), 1)
    info = jnp.zeros((tm, GATE_LANES), F32)
    for k, g in ((0, meta[:, _M_G0:_M_G0 + 1]), (1, meta[:, _M_G1:_M_G1 + 1])):
        rest = g
        for p in range(GATE_PIECES):
            piece = rest.astype(BF16).astype(F32)
            info = jnp.where(lane == k * GATE_PIECES + p, piece, info)
            rest = rest - piece
    loc_ref[:, d:d + GATE_LANES] = jnp.dot(signed, info.astype(BF16), preferred_element_type=F32).astype(BF16)


def _merge_carry(tile, tab_ref, loc_ref, carry_ref):
    base = tile * (RUN_FIELDS * N_EXPERTS)
    for e in range(N_EXPERTS):
        lstart = tab_ref[base + _R_LSTART * N_EXPERTS + e]
        length = tab_ref[base + _R_LEN * N_EXPERTS + e]
        partial = tab_ref[base + _R_PARTIAL * N_EXPERTS + e]

        def merge(e=e, lstart=lstart, length=length, partial=partial):
            first = pl.ds(pl.multiple_of(lstart, RUN_ALIGN), RUN_ALIGN)
            loc_ref[first, :] = loc_ref[first, :] + carry_ref[e]
            last = pl.ds(pl.multiple_of(lstart + length - RUN_ALIGN, RUN_ALIGN), RUN_ALIGN)

            @pl.when(partial == 1)
            def _():
                carry_ref[e] = loc_ref[last, :]

            @pl.when(partial == 0)
            def _():
                carry_ref[e] = jnp.zeros(carry_ref.shape[1:], BF16)

        pl.when(length > 0)(merge)


def _dispatch_kernel(tab_ref, fill_ref, x_ref, meta_ref, metat_ref, xs_hbm, *scratch, tm, te, n_spare):
    *loc_refs, zero_ref, carry_ref, sem, zsem = scratch
    i = pl.program_id(0)

    @pl.when(i == 0)
    def _():
        zero_ref[...] = jnp.zeros(zero_ref.shape, BF16)
        carry_ref[...] = jnp.zeros(carry_ref.shape, BF16)
        _fill_copies(fill_ref, zero_ref, xs_hbm, zsem, te=te, n_spare=n_spare, start=True)

    n_tiles = x_ref.shape[0] // tm
    for k in range(n_tiles):
        tile = n_tiles * i + k
        _sort_tile(pl.ds(k * tm, tm), x_ref, meta_ref, metat_ref[k * SUBLANES:(k + 1) * SUBLANES, :],
                   loc_refs[k], tm=tm)
        if k > 0:
            _run_copies(tab_ref, tile - 1, loc_refs[k - 1], xs_hbm, sem, to_global=True, start=False)
        _merge_carry(tile, tab_ref, loc_refs[k], carry_ref)
        _run_copies(tab_ref, tile, loc_refs[k], xs_hbm, sem, to_global=True, start=True)
    _run_copies(tab_ref, n_tiles * i + n_tiles - 1, loc_refs[-1], xs_hbm, sem, to_global=True, start=False)

    @pl.when(i == pl.num_programs(0) - 1)
    def _():
        _fill_copies(fill_ref, zero_ref, xs_hbm, zsem, te=te, n_spare=n_spare, start=False)


def _dispatch(x, meta, meta_t, run_tab, fill_tab, n_rows, n_spare):
    n, d = x.shape
    tm, te, k = TM_MOE, TM_EXPERT, DISPATCH_TILES
    width = d + GATE_LANES
    grid_spec = pltpu.PrefetchScalarGridSpec(
        num_scalar_prefetch=2,
        grid=(n // (k * tm),),
        in_specs=[pl.BlockSpec((k * tm, d), lambda i, *_: (i, 0)),
                  pl.BlockSpec((k * tm, LANES), lambda i, *_: (i, 0)),
                  pl.BlockSpec((k * SUBLANES, tm), lambda i, *_: (i, 0))],
        out_specs=pl.BlockSpec(memory_space=pl.ANY),
        scratch_shapes=[pltpu.VMEM((SLOTS, width), BF16)] * k + [pltpu.VMEM((te, width), BF16),
                        pltpu.VMEM((N_EXPERTS, RUN_ALIGN, width), BF16),
                        pltpu.SemaphoreType.DMA, pltpu.SemaphoreType.DMA],
    )
    return pl.pallas_call(
        functools.partial(_dispatch_kernel, tm=tm, te=te, n_spare=n_spare),
        out_shape=jax.ShapeDtypeStruct((n_rows, width), BF16),
        grid_spec=grid_spec,
        compiler_params=_params(("arbitrary",)),
        name="dispatch",
    )(run_tab, fill_tab, x, meta, meta_t)


def _slot_gate(info):
    lane = lax.broadcasted_iota(jnp.int32, info.shape, 1)
    first = jnp.sum(jnp.where(lane < GATE_PIECES, info, 0.0), axis=-1, keepdims=True)
    second = jnp.sum(jnp.where((lane >= GATE_PIECES) & (lane < 2 * GATE_PIECES), info, 0.0), axis=-1, keepdims=True)
    return jnp.maximum(first, -second)


def _experts_kernel(tile_ref, expert_ref, rows_ref, xs_ref, wg_ref, wu_ref, wd_ref, ys_ref, acc_ref, *, sub):
    i = pl.program_id(0)
    f = pl.program_id(1)
    te, d = ys_ref.shape
    rows = rows_ref[i]

    @pl.when((rows == 0) & (f == 0))
    def _():
        ys_ref[...] = jnp.zeros(ys_ref.shape, ys_ref.dtype)

    def partial_out(r0):
        xb = xs_ref[r0:r0 + sub, 0:d]
        g = jnp.dot(xb, wg_ref[0], preferred_element_type=F32)
        u = jnp.dot(xb, wu_ref[0], preferred_element_type=F32)
        h = (_silu(g) * u).astype(BF16)
        return jnp.dot(h, wd_ref[0], preferred_element_type=F32)

    def first_chunk(n_rows):
        for r0 in range(0, n_rows, sub):
            acc_ref[r0:r0 + sub, :] = partial_out(r0)

    def last_chunk(n_rows):
        for r0 in range(0, n_rows, sub):
            gate = _slot_gate(xs_ref[r0:r0 + sub, d:d + GATE_LANES].astype(F32))
            ys_ref[r0:r0 + sub, :] = ((acc_ref[r0:r0 + sub, :] + partial_out(r0)) * gate).astype(ys_ref.dtype)
        if n_rows < te:
            ys_ref[n_rows:te, :] = jnp.zeros((te - n_rows, d), ys_ref.dtype)

    half = te // 2
    for n_rows in (half, te):
        fits = (rows > n_rows - half) & (rows <= n_rows)
        pl.when(fits & (f == 0))(functools.partial(first_chunk, n_rows))
        pl.when(fits & (f == 1))(functools.partial(last_chunk, n_rows))


def _experts(xs, we_gate, we_up, we_down, tile_idx, tile_expert, tile_rows):
    n_rows, width = xs.shape
    d = width - GATE_LANES
    te = TM_EXPERT
    ff = we_gate.shape[-1]
    fc = ff // FF_CHUNKS
    assert n_rows % te == 0 and ff % FF_CHUNKS == 0 and fc % LANES == 0 and FF_CHUNKS == 2
    n_tiles = n_rows // te
    last = FF_CHUNKS - 1
    grid_spec = pltpu.PrefetchScalarGridSpec(
        num_scalar_prefetch=3,
        grid=(n_tiles, FF_CHUNKS),
        in_specs=[
            pl.BlockSpec((te, width), lambda i, f, tile, ex, rows: (tile[i], 0)),
            pl.BlockSpec((1, d, fc), lambda i, f, tile, ex, rows: (ex[i], 0, jnp.where(rows[i] > 0, f, last))),
            pl.BlockSpec((1, d, fc), lambda i, f, tile, ex, rows: (ex[i], 0, jnp.where(rows[i] > 0, f, last))),
            pl.BlockSpec((1, fc, d), lambda i, f, tile, ex, rows: (ex[i], jnp.where(rows[i] > 0, f, last), 0)),
        ],
        out_specs=pl.BlockSpec((te, d), lambda i, f, tile, ex, rows: (i, 0)),
        scratch_shapes=[pltpu.VMEM((te, d), F32)],
    )
    return pl.pallas_call(
        functools.partial(_experts_kernel, sub=SUB_EXPERT),
        out_shape=jax.ShapeDtypeStruct((n_rows, d), BF16),
        grid_spec=grid_spec,
        compiler_params=_params(("arbitrary", "arbitrary")),
        name="experts",
    )(tile_idx, tile_expert, tile_rows, xs, we_gate, we_up, we_down)


def _combine_kernel(tab_ref, x_ref, meta_ref, lng_ref, lnb_ref, ys_hbm, o_ref, loc_ref, sem, *, tm, alpha):
    i = pl.program_id(0)
    n_steps = pl.num_programs(0)
    n_tiles = x_ref.shape[0] // tm
    slot = lax.rem(i, 2)

    def copies(step, s, start):
        for k in range(n_tiles):
            _run_copies(tab_ref, n_tiles * step + k, loc_ref.at[s, k], ys_hbm, sem.at[s], to_global=False,
                        start=start)

    @pl.when(i == 0)
    def _():
        loc_ref[...] = jnp.zeros(loc_ref.shape, loc_ref.dtype)
        copies(0, 0, True)

    @pl.when(i + 1 < n_steps)
    def _():
        copies(i + 1, 1 - slot, True)

    copies(i, slot, False)

    lane = lax.broadcasted_iota(jnp.int32, (tm, SLOTS), 1).astype(F32)
    for k in range(n_tiles):
        rows = pl.ds(k * tm, tm)
        meta = meta_ref[rows, :]
        s0 = meta[:, _M_S0:_M_S0 + 1]
        s1 = meta[:, _M_S1:_M_S1 + 1]
        pick = jnp.where((lane == s0) | (lane == s1), 1.0, 0.0).astype(BF16)
        moe = jnp.dot(pick, loc_ref[slot, k], preferred_element_type=F32)
        o_ref[rows, :] = _layer_norm(alpha * x_ref[rows, :] + moe, lng_ref[...], lnb_ref[...])


def _combine_ln(x, meta, run_tab, ys, ln_g, ln_b, *, alpha):
    n, d = x.shape
    tm, k = TM_MOE, COMBINE_TILES
    n_steps = n // (k * tm)
    row = pl.BlockSpec((k * tm, d), lambda i, *_: (i, 0))
    grid_spec = pltpu.PrefetchScalarGridSpec(
        num_scalar_prefetch=1,
        grid=(n_steps,),
        in_specs=[row, pl.BlockSpec((k * tm, LANES), lambda i, *_: (i, 0)),
                  pl.BlockSpec(ln_g.shape, lambda i, *_: (0, 0)), pl.BlockSpec(ln_b.shape, lambda i, *_: (0, 0)),
                  pl.BlockSpec(memory_space=pl.ANY)],
        out_specs=row,
        scratch_shapes=[pltpu.VMEM((2, k, SLOTS, d), BF16), pltpu.SemaphoreType.DMA((2,))],
    )
    return pl.pallas_call(
        functools.partial(_combine_kernel, tm=tm, alpha=alpha),
        out_shape=jax.ShapeDtypeStruct((n, d), F32),
        grid_spec=grid_spec,
        compiler_params=_params(("arbitrary",)),
        name="combine_ln",
    )(run_tab, x, meta, ln_g, ln_b, ys)


def _moe_ln(x, w_router, we_gate, we_up, we_down, ln_g, ln_b, *, alpha):
    n, d = x.shape
    tm, te = TM_MOE, TM_EXPERT
    assert n % (DISPATCH_TILES * tm) == 0 and n % (COMBINE_TILES * tm) == 0 and te % SUB_EXPERT == 0
    nt = n // tm
    wr = jnp.pad(w_router, ((0, 0), (0, LANES - N_EXPERTS)))
    wr_hi = wr.astype(BF16)
    wr_lo = (wr - wr_hi.astype(F32)).astype(BF16)
    meta, meta_t, runs, totals = _router(x, wr_hi, wr_lo)

    counts = totals[:N_EXPERTS, 0].astype(jnp.int32)
    padded = (counts + RUN_ALIGN - 1) // RUN_ALIGN * RUN_ALIGN
    tiles_e = (padded + te - 1) // te
    tile_end = jnp.cumsum(tiles_e)
    region = (tile_end - tiles_e) * te
    n_used = tile_end[-1]
    max_rows = TOP_K * n + N_EXPERTS * (te - 1)
    n_tiles = -(-max_rows // te)
    n_spare = n_tiles - (TOP_K * n) // te
    runs_i = runs.reshape(nt, EXPERT_ROWS, LANES)[:, :N_EXPERTS, :RUN_FIELDS].astype(jnp.int32)
    runs_i = jnp.swapaxes(runs_i, 1, 2)
    runs_i = runs_i.at[:, _R_GOFF, :].add(region)
    run_tab = runs_i.reshape(-1)
    fill_tab = jnp.concatenate([region + padded, tiles_e * te - padded,
                                (n_used * te)[None], (n_tiles - n_used)[None]]).astype(jnp.int32)
    tiles = jnp.arange(n_tiles, dtype=jnp.int32)
    j = jnp.minimum(tiles, n_used - 1)
    tile_expert = jnp.minimum(jnp.sum(j[:, None] >= tile_end[None, :], axis=-1), N_EXPERTS - 1).astype(jnp.int32)
    region_end = jnp.sum(jnp.where(tile_expert[:, None] == jnp.arange(N_EXPERTS), region + padded, 0), axis=-1)
    tile_rows = jnp.where(tiles < n_used, jnp.minimum(region_end - tiles * te, te), 0).astype(jnp.int32)

    xs = _dispatch(x, meta, meta_t, run_tab, fill_tab, n_tiles * te, n_spare)
    ys = _experts(xs, we_gate, we_up, we_down, j, tile_expert, tile_rows)
    return _combine_ln(x, meta, run_tab, ys, ln_g, ln_b, alpha=alpha)


def kernel(x, w_in, conv_w, v_g, v_b, w_s, b_s, out_g, w_out, ln1_g, ln1_b, ln2_g, ln2_b,
           w_gate, w_up, w_down, w_router, we_gate, we_up, we_down):
    bsz, seq, d = x.shape
    depth = w_in.shape[0]
    alpha = float((2 * depth) ** 0.25)
    h = x.reshape(bsz * seq, d)
    row = lambda a: a.reshape(1, -1)
    mix_w = [w_in[0].astype(BF16), w_out[0].astype(BF16)]
    moe_w = [None, None, None]
    for i in range(depth):
        j = i // 2
        dense, last = i % 2 == 0, i + 1 == depth
        if dense:
            cast = [w_gate[j], w_up[j], w_down[j]] + ([] if last else [we_down[j]])
        else:
            cast = [we_up[j]]
        bias_t = jnp.repeat(b_s[i].T, GMLP_HEAD_DIM, axis=1)
        h, cast_w = _mixer_ln(h, mix_w[0], conv_w[i], row(v_g[i]), row(v_b[i]), w_s[i], bias_t,
                              row(out_g[i]), mix_w[1], row(ln1_g[i]), row(ln1_b[i]), cast=cast,
                              seq=seq, alpha=alpha)
        if dense:
            ahead = [] if last else [(w_in, i + 1), (w_out, i + 1), we_gate[j]]
            h, ahead_w = _ffn_ln(h, *cast_w[:3], row(ln2_g[i]), row(ln2_b[i]), cast=ahead, alpha=alpha)
            if not last:
                mix_w, moe_w = ahead_w[:2], [ahead_w[2], None, cast_w[3]]
        else:
            moe_w[1] = cast_w[0]
            h = _moe_ln(h, w_router[j], *moe_w, row(ln2_g[i]), row(ln2_b[i]), alpha=alpha)
            moe_w = [None, None, None]
            if not last:
                mix_w = [w_in[i + 1].astype(BF16), w_out[i + 1].astype(BF16)]
    return h.reshape(bsz, seq, d)
```

```python
import functools

import numpy as np
import jax
import jax.numpy as jnp
from jax import lax
from jax.experimental import pallas as pl
from jax.experimental.pallas import tpu as pltpu

F32 = jnp.float32
BF16 = jnp.bfloat16

CONV_WIDTH = 512
CONV_HEAD_DIM = 64
GMLP_WIDTH = 512
GMLP_HEADS = 4
GMLP_HEAD_DIM = 128
CHUNK = 128
N_EXPERTS = 8
TOP_K = 2
LN_EPS = 1e-5
RMS_EPS = 1e-6

LANES = 128
SUBLANES = 8
BF16_ROWS = 16
VMEM_LIMIT_BYTES = 56 * 1024 * 1024

TM_MIX = 1024
SUB_MIX = 512
TM_FFN = 1024
SUB_FFN = 256
TM_MOE = 512
DISPATCH_TILES = 4
COMBINE_TILES = 2
ROUTER_TILES = 1
TM_EXPERT = 1024
SUB_EXPERT = 256
FF_CHUNKS = 2

RUN_ALIGN = BF16_ROWS
ALIGN_SHIFT = RUN_ALIGN.bit_length() - 1
MAX_RUN = TM_MOE + RUN_ALIGN
RUN_BITS = (MAX_RUN // RUN_ALIGN).bit_length()
SLOTS = -(-(TOP_K * TM_MOE + N_EXPERTS * 2 * (RUN_ALIGN - 1)) // LANES) * LANES
GATE_LANES = LANES
GATE_PIECES = 3
TAIL_BITS = (TM_EXPERT // RUN_ALIGN - 1).bit_length()


def _layer_norm(r, g, b):
    mu = jnp.mean(r, axis=-1, keepdims=True)
    d = r - mu
    var = jnp.mean(d * d, axis=-1, keepdims=True)
    return d * lax.rsqrt(var + LN_EPS) * g + b


def _gelu(x):
    return 0.5 * x * (1.0 + lax.erf(x * (2.0 ** -0.5)))


def _silu(x):
    return x * (1.0 / (1.0 + jnp.exp(-x)))


def _params(semantics):
    return pltpu.CompilerParams(dimension_semantics=semantics, vmem_limit_bytes=VMEM_LIMIT_BYTES)


def _const_spec(shape):
    nd = len(shape)
    return pl.BlockSpec(shape, lambda *_: (0,) * nd)


def _cast_plan(weights, n_steps):
    views, in_specs, out_specs, out_shapes, shapes = [], [], [], [], []
    for w in weights:
        a, layer = w if isinstance(w, tuple) else (w, None)
        shape = a.shape if layer is None else a.shape[1:]
        rows, cols = int(np.prod(shape[:-1])), shape[-1]
        n_blocks = next(nb for nb in range(n_steps, 0, -1)
                        if n_steps % nb == 0 and rows % (nb * BF16_ROWS) == 0)
        r, first = n_steps // n_blocks, 0 if layer is None else layer * n_blocks
        views.append(a.reshape(-1, cols))
        in_specs.append(pl.BlockSpec((rows // n_blocks, cols), lambda i, r=r, first=first: (first + i // r, 0)))
        out_specs.append(pl.BlockSpec((rows // n_blocks, cols), lambda i, r=r: (i // r, 0)))
        out_shapes.append(jax.ShapeDtypeStruct((rows, cols), BF16))
        shapes.append(shape)
    return views, in_specs, out_specs, out_shapes, shapes


def _cast_along(src_refs, dst_refs):
    for src, dst in zip(src_refs, dst_refs):
        dst[...] = src[...].astype(dst.dtype)


def _mixer_kernel(*refs, n_cast, tm, sub, seq, alpha):
    (x_ref, win_ref, convw_ref, vg_ref, vb_ref, ws_ref, bias_ref, outg_ref, wout_ref,
     lng_ref, lnb_ref) = refs[:11]
    o_ref, ch_ref = refs[11 + n_cast], refs[-1]
    _cast_along(refs[11:11 + n_cast], refs[12 + n_cast:12 + 2 * n_cast])
    i = pl.program_id(0)
    cw_, gw_ = CONV_WIDTH, GMLP_WIDTH

    @pl.when(lax.rem(i * tm, seq) == 0)
    def _():
        ch_ref[0:SUBLANES, :] = jnp.zeros((SUBLANES, cw_), F32)

    trow = lax.broadcasted_iota(jnp.int32, (CHUNK, CHUNK), 0)
    tcol = lax.broadcasted_iota(jnp.int32, (CHUNK, CHUNK), 1)
    causal = tcol <= trow
    ws = [jnp.where(causal, ws_ref[h], 0.0).astype(BF16) for h in range(GMLP_HEADS)]
    lane = lax.broadcasted_iota(jnp.int32, (sub, LANES), 1)
    low_half = lane < CONV_HEAD_DIM
    nch = sub // CHUNK

    for r0 in range(0, tm, sub):
        x = x_ref[r0:r0 + sub, :]
        proj = jnp.dot(x.astype(BF16), win_ref[...], preferred_element_type=F32)
        b_gate = proj[:, 0:cw_]
        c_gate = proj[:, cw_:2 * cw_]
        hh = proj[:, 2 * cw_:3 * cw_]
        u = proj[:, 3 * cw_:3 * cw_ + gw_]
        v = proj[:, 3 * cw_ + gw_:3 * cw_ + 2 * gw_]

        c0 = SUBLANES + r0
        ch_ref[c0:c0 + sub, :] = c_gate * hh
        convw = convw_ref[...]
        conv = (convw[0:1, :] * ch_ref[c0 - 2:c0 - 2 + sub, :]
                + convw[1:2, :] * ch_ref[c0 - 1:c0 - 1 + sub, :]
                + convw[2:3, :] * ch_ref[c0:c0 + sub, :])
        y_conv = b_gate * conv

        ug = _gelu(u)
        vn = _layer_norm(_gelu(v), vg_ref[...], vb_ref[...]).astype(BF16)
        zs = []
        for h in range(GMLP_HEADS):
            lo, hi = h * GMLP_HEAD_DIM, (h + 1) * GMLP_HEAD_DIM
            rhs = jnp.concatenate([vn[c * CHUNK:(c + 1) * CHUNK, lo:hi] for c in range(nch)], axis=1)
            zs.append(jnp.dot(ws[h], rhs, preferred_element_type=F32))
        z = jnp.concatenate(
            [jnp.concatenate([zs[h][:, c * GMLP_HEAD_DIM:(c + 1) * GMLP_HEAD_DIM]
                              for h in range(GMLP_HEADS)], axis=1) for c in range(nch)], axis=0)
        bias = jnp.concatenate([bias_ref[...]] * nch, axis=0)
        y_sg = ug * (z + bias)

        parts = []
        for j in range(cw_ // LANES):
            yt = y_conv[:, j * LANES:(j + 1) * LANES]
            sq = yt * yt
            ms_lo = jnp.sum(jnp.where(low_half, sq, 0.0), axis=-1, keepdims=True) * (1.0 / CONV_HEAD_DIM)
            ms_hi = jnp.sum(jnp.where(low_half, 0.0, sq), axis=-1, keepdims=True) * (1.0 / CONV_HEAD_DIM)
            parts.append(yt * jnp.where(low_half, lax.rsqrt(ms_lo + RMS_EPS), lax.rsqrt(ms_hi + RMS_EPS)))
        for h in range(GMLP_HEADS):
            yt = y_sg[:, h * GMLP_HEAD_DIM:(h + 1) * GMLP_HEAD_DIM]
            ms = jnp.mean(yt * yt, axis=-1, keepdims=True)
            parts.append(yt * lax.rsqrt(ms + RMS_EPS))
        y = jnp.concatenate(parts, axis=1) * outg_ref[...]

        mix = jnp.dot(y.astype(BF16), wout_ref[...], preferred_element_type=F32)
        o_ref[r0:r0 + sub, :] = _layer_norm(alpha * x + mix, lng_ref[...], lnb_ref[...])

    ch_ref[0:SUBLANES, :] = ch_ref[tm:tm + SUBLANES, :]


def _mixer_ln(x, w_in, conv_w, v_g, v_b, w_s, bias_t, out_g, w_out, ln_g, ln_b, cast=(), *, seq, alpha):
    n, d = x.shape
    tm, sub = TM_MIX, SUB_MIX
    assert n % tm == 0 and seq % tm == 0 and tm % sub == 0 and sub % CHUNK == 0
    views, cast_in, cast_out, cast_shapes, shapes = _cast_plan(cast, n // tm)
    kern = functools.partial(_mixer_kernel, n_cast=len(cast), tm=tm, sub=sub, seq=seq, alpha=alpha)
    row = pl.BlockSpec((tm, d), lambda i: (i, 0))
    out = pl.pallas_call(
        kern,
        out_shape=[jax.ShapeDtypeStruct((n, d), F32)] + cast_shapes,
        grid=(n // tm,),
        in_specs=[row, _const_spec(w_in.shape), _const_spec(conv_w.shape), _const_spec(v_g.shape),
                  _const_spec(v_b.shape), _const_spec(w_s.shape), _const_spec(bias_t.shape),
                  _const_spec(out_g.shape), _const_spec(w_out.shape), _const_spec(ln_g.shape),
                  _const_spec(ln_b.shape)] + cast_in,
        out_specs=[row] + cast_out,
        scratch_shapes=[pltpu.VMEM((tm + SUBLANES, CONV_WIDTH), F32)],
        compiler_params=_params(("arbitrary",)),
        name="mixer_ln",
    )(x, w_in, conv_w, v_g, v_b, w_s, bias_t, out_g, w_out, ln_g, ln_b, *views)
    return out[0], [o.reshape(shape) for o, shape in zip(out[1:], shapes)]


def _ffn_kernel(*refs, n_cast, sub, alpha):
    x_ref, wg_ref, wu_ref, wd_ref, lng_ref, lnb_ref = refs[:6]
    o_ref = refs[6 + n_cast]
    _cast_along(refs[6:6 + n_cast], refs[7 + n_cast:7 + 2 * n_cast])
    for r0 in range(0, x_ref.shape[0], sub):
        x = x_ref[r0:r0 + sub, :]
        xb = x.astype(BF16)
        g = jnp.dot(xb, wg_ref[...], preferred_element_type=F32)
        u = jnp.dot(xb, wu_ref[...], preferred_element_type=F32)
        h = (_silu(g) * u).astype(BF16)
        ffn = jnp.dot(h, wd_ref[...], preferred_element_type=F32)
        o_ref[r0:r0 + sub, :] = _layer_norm(alpha * x + ffn, lng_ref[...], lnb_ref[...])


def _ffn_ln(x, wg, wu, wd, ln_g, ln_b, cast=(), *, alpha):
    n, d = x.shape
    tm = TM_FFN
    assert n % tm == 0 and tm % SUB_FFN == 0
    views, cast_in, cast_out, cast_shapes, shapes = _cast_plan(cast, n // tm)
    row = pl.BlockSpec((tm, d), lambda i: (i, 0))
    out = pl.pallas_call(
        functools.partial(_ffn_kernel, n_cast=len(cast), sub=SUB_FFN, alpha=alpha),
        out_shape=[jax.ShapeDtypeStruct((n, d), F32)] + cast_shapes,
        grid=(n // tm,),
        in_specs=[row, _const_spec(wg.shape), _const_spec(wu.shape), _const_spec(wd.shape),
                  _const_spec(ln_g.shape), _const_spec(ln_b.shape)] + cast_in,
        out_specs=[row] + cast_out,
        compiler_params=_params(("arbitrary",)),
        name="ffn_ln",
    )(x, wg, wu, wd, ln_g, ln_b, *views)
    return out[0], [o.reshape(shape) for o, shape in zip(out[1:], shapes)]


_M_E0, _M_E1, _M_S0, _M_S1, _M_G0, _M_G1 = range(6)
_R_LSTART, _R_LEN, _R_GOFF, _R_PARTIAL = range(4)
RUN_FIELDS = 4
EXPERT_ROWS = max(N_EXPERTS, BF16_ROWS)


def _router_kernel(x_ref, wrh_ref, wrl_ref, meta_ref, metat_ref, runs_ref, tot_ref, base_ref, *, tm):
    i = pl.program_id(0)

    @pl.when(i == 0)
    def _():
        base_ref[...] = jnp.zeros(base_ref.shape, F32)

    for k in range(x_ref.shape[0] // tm):
        _route_tile(x_ref[k * tm:(k + 1) * tm, :], wrh_ref, wrl_ref, meta_ref.at[k * tm:(k + 1) * tm],
                    metat_ref.at[k * SUBLANES:(k + 1) * SUBLANES],
                    runs_ref.at[k * EXPERT_ROWS:(k + 1) * EXPERT_ROWS], tot_ref, base_ref)


def _route_tile(x, wrh_ref, wrl_ref, meta_ref, metat_ref, runs_ref, tot_ref, base_ref):
    tm = x.shape[0]
    xh = x.astype(BF16)
    xl = (x - xh.astype(F32)).astype(BF16)
    wrh = wrh_ref[...]
    logits = (jnp.dot(xh, wrh, preferred_element_type=F32)
              + jnp.dot(xl, wrh, preferred_element_type=F32)
              + jnp.dot(xh, wrl_ref[...], preferred_element_type=F32))

    er = EXPERT_ROWS
    lg = logits.T[0:er, :]
    ef = lax.broadcasted_iota(jnp.int32, (er, tm), 0).astype(F32)
    neg = jnp.float32(-jnp.inf)
    lg = jnp.where(ef < N_EXPERTS, lg, neg)
    m0 = jnp.max(lg, axis=0, keepdims=True)
    e0 = jnp.min(jnp.where(lg == m0, ef, float(er)), axis=0, keepdims=True)
    lg1 = jnp.where(ef == e0, neg, lg)
    m1 = jnp.max(lg1, axis=0, keepdims=True)
    e1 = jnp.min(jnp.where(lg1 == m1, ef, float(er)), axis=0, keepdims=True)
    t = jnp.exp(m1 - m0)
    g0 = 1.0 / (1.0 + t)
    g1 = t / (1.0 + t)

    oh0 = ef == e0
    oh1 = ef == e1
    cnt = jnp.where(oh0 | oh1, 1.0, 0.0)
    s_i = lax.broadcasted_iota(jnp.int32, (tm, tm), 0)
    t_i = lax.broadcasted_iota(jnp.int32, (tm, tm), 1)
    earlier = jnp.where(s_i < t_i, 1.0, 0.0).astype(BF16)
    rank = jnp.dot(cnt.astype(BF16), earlier, preferred_element_type=F32)

    def align_down(v):
        return jnp.floor(v * (1.0 / RUN_ALIGN)) * RUN_ALIGN

    count = jnp.broadcast_to(jnp.sum(cnt, axis=1, keepdims=True), (er, LANES))
    before = base_ref[...]
    phase = before - align_down(before)
    end = phase + count
    run_len = jnp.where(count > 0, align_down(end + (RUN_ALIGN - 1.0)), 0.0)
    partial = jnp.where((count > 0) & (end != align_down(end)), 1.0, 0.0)
    r_e = lax.broadcasted_iota(jnp.int32, (er, er), 0)
    c_e = lax.broadcasted_iota(jnp.int32, (er, er), 1)
    lower = jnp.where(c_e < r_e, 1.0, 0.0).astype(BF16)
    lstart = jnp.dot(lower, run_len.astype(BF16), preferred_element_type=F32)
    base_ref[...] = before + count
    tot_ref[...] = before + count
    field = lax.broadcasted_iota(jnp.int32, (er, LANES), 1)
    table = jnp.zeros((er, LANES), F32)
    for f, val in ((_R_LSTART, lstart), (_R_LEN, run_len), (_R_GOFF, before - phase), (_R_PARTIAL, partial)):
        table = jnp.where(field == f, val, table)
    runs_ref[...] = table

    slot = rank + (lstart + phase)[:, 0:1]
    s0 = jnp.sum(jnp.where(oh0, slot, 0.0), axis=0, keepdims=True)
    s1 = jnp.sum(jnp.where(oh1, slot, 0.0), axis=0, keepdims=True)
    row = lax.broadcasted_iota(jnp.int32, (SUBLANES, tm), 0)
    meta_t = jnp.zeros((SUBLANES, tm), F32)
    for k, val in ((_M_E0, e0), (_M_E1, e1), (_M_S0, s0), (_M_S1, s1), (_M_G0, g0), (_M_G1, g1)):
        meta_t = jnp.where(row == k, val, meta_t)
    metat_ref[...] = meta_t
    meta_ref[...] = jnp.concatenate([meta_t, jnp.zeros((LANES - SUBLANES, tm), F32)], axis=0).T


def _router(x, wr_hi, wr_lo):
    n, d = x.shape
    tm, k = TM_MOE, ROUTER_TILES
    assert n % (k * tm) == 0
    nt = n // tm
    small = (EXPERT_ROWS, LANES)
    return pl.pallas_call(
        functools.partial(_router_kernel, tm=tm),
        out_shape=(jax.ShapeDtypeStruct((n, LANES), F32), jax.ShapeDtypeStruct((nt * SUBLANES, tm), F32),
                   jax.ShapeDtypeStruct((nt * EXPERT_ROWS, LANES), F32), jax.ShapeDtypeStruct(small, F32)),
        grid=(nt // k,),
        in_specs=[pl.BlockSpec((k * tm, d), lambda i: (i, 0)), _const_spec(wr_hi.shape), _const_spec(wr_lo.shape)],
        out_specs=(pl.BlockSpec((k * tm, LANES), lambda i: (i, 0)), pl.BlockSpec((k * SUBLANES, tm), lambda i: (i, 0)),
                   pl.BlockSpec((k * EXPERT_ROWS, LANES), lambda i: (i, 0)), _const_spec(small)),
        scratch_shapes=[pltpu.VMEM(small, F32)],
        compiler_params=_params(("arbitrary",)),
        name="router",
    )(x, wr_hi, wr_lo)


def _run_copies(tab_ref, tile, local_ref, glob_hbm, sem, *, to_global, start):
    base = tile * (RUN_FIELDS * N_EXPERTS)
    for e in range(N_EXPERTS):
        lstart = tab_ref[base + e]
        length = tab_ref[base + N_EXPERTS + e]
        gstart = tab_ref[base + 2 * N_EXPERTS + e]
        for b in range(RUN_BITS):
            size = RUN_ALIGN << b
            off = (length >> (b + ALIGN_SHIFT + 1)) << (b + ALIGN_SHIFT + 1)

            def piece(size=size, off=off, lstart=lstart, gstart=gstart, priority=(e + b) % 2):
                loc = local_ref.at[pl.ds(pl.multiple_of(lstart + off, RUN_ALIGN), size)]
                glo = glob_hbm.at[pl.ds(pl.multiple_of(gstart + off, RUN_ALIGN), size)]
                cp = pltpu.make_async_copy(loc, glo, sem) if to_global else pltpu.make_async_copy(glo, loc, sem)
                if start:
                    cp.start(priority=priority)
                else:
                    cp.wait()

            pl.when(((length >> (b + ALIGN_SHIFT)) & 1) == 1)(piece)


def _fill_copies(fill_ref, zero_ref, glob_hbm, sem, *, te, n_spare, start):
    def go(cp):
        if start:
            cp.start()
        else:
            cp.wait()

    for e in range(N_EXPERTS):
        tstart = fill_ref[e]
        length = fill_ref[N_EXPERTS + e]
        for b in range(TAIL_BITS):
            size = RUN_ALIGN << b
            off = (length >> (b + ALIGN_SHIFT + 1)) << (b + ALIGN_SHIFT + 1)

            def piece(size=size, off=off, tstart=tstart):
                go(pltpu.make_async_copy(zero_ref.at[pl.ds(0, size)],
                                         glob_hbm.at[pl.ds(pl.multiple_of(tstart + off, RUN_ALIGN), size)], sem))

            pl.when(((length >> (b + ALIGN_SHIFT)) & 1) == 1)(piece)
    spare0 = fill_ref[2 * N_EXPERTS]
    for j in range(n_spare):
        def tile_fill(j=j):
            go(pltpu.make_async_copy(zero_ref, glob_hbm.at[pl.ds(pl.multiple_of(spare0 + j * te, te), te)], sem))

        pl.when(j < fill_ref[2 * N_EXPERTS + 1])(tile_fill)


def _sort_tile(rows, x_ref, meta_ref, meta_t, loc_ref, *, tm):
    d = x_ref.shape[1]
    meta = meta_ref[rows, :]
    s0 = meta_t[_M_S0:_M_S0 + 1, :]
    s1 = meta_t[_M_S1:_M_S1 + 1, :]
    slot = lax.broadcasted_iota(jnp.int32, (SLOTS, tm), 0).astype(F32)
    signed = jnp.where(slot == s0, 1.0, jnp.where(slot == s1, -1.0, 0.0)).astype(BF16)
    loc_ref[:, 0:d] = jnp.dot(jnp.abs(signed), x_ref[rows, :].astype(BF16),
                              preferred_element_type=F32).astype(BF16)

    lane = lax.broadcasted_iota(jnp.int32, (tm, GATE_LANES), 1)
    info = jnp.zeros((tm, GATE_LANES), F32)
    for k, g in ((0, meta[:, _M_G0:_M_G0 + 1]), (1, meta[:, _M_G1:_M_G1 + 1])):
        rest = g
        for p in range(GATE_PIECES):
            piece = rest.astype(BF16).astype(F32)
            info = jnp.where(lane == k * GATE_PIECES + p, piece, info)
            rest = rest - piece
    loc_ref[:, d:d + GATE_LANES] = jnp.dot(signed, info.astype(BF16), preferred_element_type=F32).astype(BF16)


def _merge_carry(tile, tab_ref, loc_ref, carry_ref):
    base = tile * (RUN_FIELDS * N_EXPERTS)
    for e in range(N_EXPERTS):
        lstart = tab_ref[base + _R_LSTART * N_EXPERTS + e]
        length = tab_ref[base + _R_LEN * N_EXPERTS + e]
        partial = tab_ref[base + _R_PARTIAL * N_EXPERTS + e]

        def merge(e=e, lstart=lstart, length=length, partial=partial):
            first = pl.ds(pl.multiple_of(lstart, RUN_ALIGN), RUN_ALIGN)
            loc_ref[first, :] = loc_ref[first, :] + carry_ref[e]
            last = pl.ds(pl.multiple_of(lstart + length - RUN_ALIGN, RUN_ALIGN), RUN_ALIGN)

            @pl.when(partial == 1)
            def _():
                carry_ref[e] = loc_ref[last, :]

            @pl.when(partial == 0)
            def _():
                carry_ref[e] = jnp.zeros(carry_ref.shape[1:], BF16)

        pl.when(length > 0)(merge)


def _dispatch_kernel(tab_ref, fill_ref, x_ref, meta_ref, metat_ref, xs_hbm, *scratch, tm, te, n_spare):
    *loc_refs, zero_ref, carry_ref, sem, zsem = scratch
    i = pl.program_id(0)

    @pl.when(i == 0)
    def _():
        zero_ref[...] = jnp.zeros(zero_ref.shape, BF16)
        carry_ref[...] = jnp.zeros(carry_ref.shape, BF16)
        _fill_copies(fill_ref, zero_ref, xs_hbm, zsem, te=te, n_spare=n_spare, start=True)

    n_tiles = x_ref.shape[0] // tm
    for k in range(n_tiles):
        tile = n_tiles * i + k
        _sort_tile(pl.ds(k * tm, tm), x_ref, meta_ref, metat_ref[k * SUBLANES:(k + 1) * SUBLANES, :],
                   loc_refs[k], tm=tm)
        if k > 0:
            _run_copies(tab_ref, tile - 1, loc_refs[k - 1], xs_hbm, sem, to_global=True, start=False)
        _merge_carry(tile, tab_ref, loc_refs[k], carry_ref)
        _run_copies(tab_ref, tile, loc_refs[k], xs_hbm, sem, to_global=True, start=True)
    _run_copies(tab_ref, n_tiles * i + n_tiles - 1, loc_refs[-1], xs_hbm, sem, to_global=True, start=False)

    @pl.when(i == pl.num_programs(0) - 1)
    def _():
        _fill_copies(fill_ref, zero_ref, xs_hbm, zsem, te=te, n_spare=n_spare, start=False)


def _dispatch(x, meta, meta_t, run_tab, fill_tab, n_rows, n_spare):
    n, d = x.shape
    tm, te, k = TM_MOE, TM_EXPERT, DISPATCH_TILES
    width = d + GATE_LANES
    grid_spec = pltpu.PrefetchScalarGridSpec(
        num_scalar_prefetch=2,
        grid=(n // (k * tm),),
        in_specs=[pl.BlockSpec((k * tm, d), lambda i, *_: (i, 0)),
                  pl.BlockSpec((k * tm, LANES), lambda i, *_: (i, 0)),
                  pl.BlockSpec((k * SUBLANES, tm), lambda i, *_: (i, 0))],
        out_specs=pl.BlockSpec(memory_space=pl.ANY),
        scratch_shapes=[pltpu.VMEM((SLOTS, width), BF16)] * k + [pltpu.VMEM((te, width), BF16),
                        pltpu.VMEM((N_EXPERTS, RUN_ALIGN, width), BF16),
                        pltpu.SemaphoreType.DMA, pltpu.SemaphoreType.DMA],
    )
    return pl.pallas_call(
        functools.partial(_dispatch_kernel, tm=tm, te=te, n_spare=n_spare),
        out_shape=jax.ShapeDtypeStruct((n_rows, width), BF16),
        grid_spec=grid_spec,
        compiler_params=_params(("arbitrary",)),
        name="dispatch",
    )(run_tab, fill_tab, x, meta, meta_t)


def _slot_gate(info):
    lane = lax.broadcasted_iota(jnp.int32, info.shape, 1)
    first = jnp.sum(jnp.where(lane < GATE_PIECES, info, 0.0), axis=-1, keepdims=True)
    second = jnp.sum(jnp.where((lane >= GATE_PIECES) & (lane < 2 * GATE_PIECES), info, 0.0), axis=-1, keepdims=True)
    return jnp.maximum(first, -second)


def _experts_kernel(tile_ref, expert_ref, rows_ref, xs_ref, wg_ref, wu_ref, wd_ref, ys_ref, acc_ref, *, sub):
    i = pl.program_id(0)
    f = pl.program_id(1)
    te, d = ys_ref.shape
    rows = rows_ref[i]

    @pl.when((rows == 0) & (f == 0))
    def _():
        ys_ref[...] = jnp.zeros(ys_ref.shape, ys_ref.dtype)

    def partial_out(r0):
        xb = xs_ref[r0:r0 + sub, 0:d]
        g = jnp.dot(xb, wg_ref[0], preferred_element_type=F32)
        u = jnp.dot(xb, wu_ref[0], preferred_element_type=F32)
        h = (_silu(g) * u).astype(BF16)
        return jnp.dot(h, wd_ref[0], preferred_element_type=F32)

    def first_chunk(n_rows):
        for r0 in range(0, n_rows, sub):
            acc_ref[r0:r0 + sub, :] = partial_out(r0)

    def last_chunk(n_rows):
        for r0 in range(0, n_rows, sub):
            gate = _slot_gate(xs_ref[r0:r0 + sub, d:d + GATE_LANES].astype(F32))
            ys_ref[r0:r0 + sub, :] = ((acc_ref[r0:r0 + sub, :] + partial_out(r0)) * gate).astype(ys_ref.dtype)
        if n_rows < te:
            ys_ref[n_rows:te, :] = jnp.zeros((te - n_rows, d), ys_ref.dtype)

    half = te // 2
    for n_rows in (half, te):
        fits = (rows > n_rows - half) & (rows <= n_rows)
        pl.when(fits & (f == 0))(functools.partial(first_chunk, n_rows))
        pl.when(fits & (f == 1))(functools.partial(last_chunk, n_rows))


def _experts(xs, we_gate, we_up, we_down, tile_idx, tile_expert, tile_rows):
    n_rows, width = xs.shape
    d = width - GATE_LANES
    te = TM_EXPERT
    ff = we_gate.shape[-1]
    fc = ff // FF_CHUNKS
    assert n_rows % te == 0 and ff % FF_CHUNKS == 0 and fc % LANES == 0 and FF_CHUNKS == 2
    n_tiles = n_rows // te
    last = FF_CHUNKS - 1
    grid_spec = pltpu.PrefetchScalarGridSpec(
        num_scalar_prefetch=3,
        grid=(n_tiles, FF_CHUNKS),
        in_specs=[
            pl.BlockSpec((te, width), lambda i, f, tile, ex, rows: (tile[i], 0)),
            pl.BlockSpec((1, d, fc), lambda i, f, tile, ex, rows: (ex[i], 0, jnp.where(rows[i] > 0, f, last))),
            pl.BlockSpec((1, d, fc), lambda i, f, tile, ex, rows: (ex[i], 0, jnp.where(rows[i] > 0, f, last))),
            pl.BlockSpec((1, fc, d), lambda i, f, tile, ex, rows: (ex[i], jnp.where(rows[i] > 0, f, last), 0)),
        ],
        out_specs=pl.BlockSpec((te, d), lambda i, f, tile, ex, rows: (i, 0)),
        scratch_shapes=[pltpu.VMEM((te, d), F32)],
    )
    return pl.pallas_call(
        functools.partial(_experts_kernel, sub=SUB_EXPERT),
        out_shape=jax.ShapeDtypeStruct((n_rows, d), BF16),
        grid_spec=grid_spec,
        compiler_params=_params(("arbitrary", "arbitrary")),
        name="experts",
    )(tile_idx, tile_expert, tile_rows, xs, we_gate, we_up, we_down)


def _combine_kernel(tab_ref, x_ref, meta_ref, lng_ref, lnb_ref, ys_hbm, o_ref, loc_ref, sem, *, tm, alpha):
    i = pl.program_id(0)
    n_steps = pl.num_programs(0)
    n_tiles = x_ref.shape[0] // tm
    slot = lax.rem(i, 2)

    def copies(step, s, start):
        for k in range(n_tiles):
            _run_copies(tab_ref, n_tiles * step + k, loc_ref.at[s, k], ys_hbm, sem.at[s], to_global=False,
                        start=start)

    @pl.when(i == 0)
    def _():
        loc_ref[...] = jnp.zeros(loc_ref.shape, loc_ref.dtype)
        copies(0, 0, True)

    @pl.when(i + 1 < n_steps)
    def _():
        copies(i + 1, 1 - slot, True)

    copies(i, slot, False)

    lane = lax.broadcasted_iota(jnp.int32, (tm, SLOTS), 1).astype(F32)
    for k in range(n_tiles):
        rows = pl.ds(k * tm, tm)
        meta = meta_ref[rows, :]
        s0 = meta[:, _M_S0:_M_S0 + 1]
        s1 = meta[:, _M_S1:_M_S1 + 1]
        pick = jnp.where((lane == s0) | (lane == s1), 1.0, 0.0).astype(BF16)
        moe = jnp.dot(pick, loc_ref[slot, k], preferred_element_type=F32)
        o_ref[rows, :] = _layer_norm(alpha * x_ref[rows, :] + moe, lng_ref[...], lnb_ref[...])


def _combine_ln(x, meta, run_tab, ys, ln_g, ln_b, *, alpha):
    n, d = x.shape
    tm, k = TM_MOE, COMBINE_TILES
    n_steps = n // (k * tm)
    row = pl.BlockSpec((k * tm, d), lambda i, *_: (i, 0))
    grid_spec = pltpu.PrefetchScalarGridSpec(
        num_scalar_prefetch=1,
        grid=(n_steps,),
        in_specs=[row, pl.BlockSpec((k * tm, LANES), lambda i, *_: (i, 0)),
                  pl.BlockSpec(ln_g.shape, lambda i, *_: (0, 0)), pl.BlockSpec(ln_b.shape, lambda i, *_: (0, 0)),
                  pl.BlockSpec(memory_space=pl.ANY)],
        out_specs=row,
        scratch_shapes=[pltpu.VMEM((2, k, SLOTS, d), BF16), pltpu.SemaphoreType.DMA((2,))],
    )
    return pl.pallas_call(
        functools.partial(_combine_kernel, tm=tm, alpha=alpha),
        out_shape=jax.ShapeDtypeStruct((n, d), F32),
        grid_spec=grid_spec,
        compiler_params=_params(("arbitrary",)),
        name="combine_ln",
    )(run_tab, x, meta, ln_g, ln_b, ys)


def _moe_ln(x, w_router, we_gate, we_up, we_down, ln_g, ln_b, *, alpha):
    n, d = x.shape
    tm, te = TM_MOE, TM_EXPERT
    assert n % (DISPATCH_TILES * tm) == 0 and n % (COMBINE_TILES * tm) == 0 and te % SUB_EXPERT == 0
    nt = n // tm
    wr = jnp.pad(w_router, ((0, 0), (0, LANES - N_EXPERTS)))
    wr_hi = wr.astype(BF16)
    wr_lo = (wr - wr_hi.astype(F32)).astype(BF16)
    meta, meta_t, runs, totals = _router(x, wr_hi, wr_lo)

    counts = totals[:N_EXPERTS, 0].astype(jnp.int32)
    padded = (counts + RUN_ALIGN - 1) // RUN_ALIGN * RUN_ALIGN
    tiles_e = (padded + te - 1) // te
    tile_end = jnp.cumsum(tiles_e)
    region = (tile_end - tiles_e) * te
    n_used = tile_end[-1]
    max_rows = TOP_K * n + N_EXPERTS * (te - 1)
    n_tiles = -(-max_rows // te)
    n_spare = n_tiles - (TOP_K * n) // te
    runs_i = runs.reshape(nt, EXPERT_ROWS, LANES)[:, :N_EXPERTS, :RUN_FIELDS].astype(jnp.int32)
    runs_i = jnp.swapaxes(runs_i, 1, 2)
    runs_i = runs_i.at[:, _R_GOFF, :].add(region)
    run_tab = runs_i.reshape(-1)
    fill_tab = jnp.concatenate([region + padded, tiles_e * te - padded,
                                (n_used * te)[None], (n_tiles - n_used)[None]]).astype(jnp.int32)
    tiles = jnp.arange(n_tiles, dtype=jnp.int32)
    j = jnp.minimum(tiles, n_used - 1)
    tile_expert = jnp.minimum(jnp.sum(j[:, None] >= tile_end[None, :], axis=-1), N_EXPERTS - 1).astype(jnp.int32)
    region_end = jnp.sum(jnp.where(tile_expert[:, None] == jnp.arange(N_EXPERTS), region + padded, 0), axis=-1)
    tile_rows = jnp.where(tiles < n_used, jnp.minimum(region_end - tiles * te, te), 0).astype(jnp.int32)

    xs = _dispatch(x, meta, meta_t, run_tab, fill_tab, n_tiles * te, n_spare)
    ys = _experts(xs, we_gate, we_up, we_down, j, tile_expert, tile_rows)
    return _combine_ln(x, meta, run_tab, ys, ln_g, ln_b, alpha=alpha)


def kernel(x, w_in, conv_w, v_g, v_b, w_s, b_s, out_g, w_out, ln1_g, ln1_b, ln2_g, ln2_b,
           w_gate, w_up, w_down, w_router, we_gate, we_up, we_down):
    bsz, seq, d = x.shape
    depth = w_in.shape[0]
    alpha = float((2 * depth) ** 0.25)
    h = x.reshape(bsz * seq, d)
    row = lambda a: a.reshape(1, -1)
    mix_w = [w_in[0].astype(BF16), w_out[0].astype(BF16)]
    moe_w = [None, None, None]
    for i in range(depth):
        j = i // 2
        dense, last = i % 2 == 0, i + 1 == depth
        if dense:
            cast = [w_gate[j], w_up[j], w_down[j]] + ([] if last else [we_down[j]])
        else:
            cast = [we_up[j]]
        bias_t = jnp.repeat(b_s[i].T, GMLP_HEAD_DIM, axis=1)
        h, cast_w = _mixer_ln(h, mix_w[0], conv_w[i], row(v_g[i]), row(v_b[i]), w_s[i], bias_t,
                              row(out_g[i]), mix_w[1], row(ln1_g[i]), row(ln1_b[i]), cast=cast,
                              seq=seq, alpha=alpha)
        if dense:
            ahead = [] if last else [(w_in, i + 1), (w_out, i + 1), we_gate[j]]
            h, ahead_w = _ffn_ln(h, *cast_w[:3], row(ln2_g[i]), row(ln2_b[i]), cast=ahead, alpha=alpha)
            if not last:
                mix_w, moe_w = ahead_w[:2], [ahead_w[2], None, cast_w[3]]
        else:
            moe_w[1] = cast_w[0]
            h = _moe_ln(h, w_router[j], *moe_w, row(ln2_g[i]), row(ln2_b[i]), alpha=alpha)
            moe_w = [None, None, None]
            if not last:
                mix_w = [w_in[i + 1].astype(BF16), w_out[i + 1].astype(BF16)]
    return h.reshape(bsz, seq, d)
```
